```python
import jax
import jax.numpy as jnp
from jax import lax
import numpy as np


D_MODEL = 1024
BATCH = 2
SEQ = 8192
DEPTH = 1

SSM_EXPAND = 2
D_INNER = SSM_EXPAND * D_MODEL
SSM_HEAD_DIM = 64
SSM_HEADS = D_INNER // SSM_HEAD_DIM
SSM_GROUPS = 8
SSM_HPG = SSM_HEADS // SSM_GROUPS
SSM_STATE = 128
SSM_CONV = 4
SSM_CHUNK = 128
SSM_XBC = D_INNER + 2 * SSM_GROUPS * SSM_STATE
ATT_HEAD_DIM = 64
ATT_HEADS = D_MODEL // ATT_HEAD_DIM
ATT_KV_GROUPS = 4
ATT_HPG = ATT_HEADS // ATT_KV_GROUPS
D_ATT = ATT_HEADS * ATT_HEAD_DIM
CMP_LEN = 32
CMP_STRIDE = 16
CMP_HIDDEN = 256
SEL_BLOCK = 64
SEL_TOPK = 16
WINDOW = 512
Q_BLOCK = 128
ROPE_THETA = 500000.0
ROPE_DIM = ATT_HEAD_DIM // 4
MLP_HIDDEN = 4 * D_MODEL
NORM_EPS = 1e-6
NEG = -1e30
BIG = 1e30

SPLIT_SIZES = (D_INNER, SSM_XBC, SSM_HEADS, D_ATT, 6 * ATT_KV_GROUPS * ATT_HEAD_DIM, 3 * ATT_HEADS, 2 * D_MODEL)
SPLIT_POINTS = tuple(int(v) for v in np.cumsum(SPLIT_SIZES)[:-1])
IN_COLS = sum(SPLIT_SIZES)

kernel_name = 'hybrid_ssd_nsa_gated_block'


def rms_norm(x, g):
    x32 = x.astype(jnp.float32)
    y = x32 * lax.rsqrt(jnp.mean(x32 * x32, axis=-1, keepdims=True) + NORM_EPS)
    return (y * g.astype(jnp.float32)).astype(x.dtype)


def rope_partial(t, cos, sin):
    t32 = t.astype(jnp.float32)
    half = ROPE_DIM // 2
    t1 = t32[..., :half]
    t2 = t32[..., half:ROPE_DIM]
    out = jnp.concatenate([t1 * cos - t2 * sin, t2 * cos + t1 * sin, t32[..., ROPE_DIM:]], axis=-1)
    return out.astype(t.dtype)


def masked_softmax(s, m):
    p = jax.nn.softmax(jnp.where(m, s, NEG), axis=-1)
    return p * m


def causal_dwconv(u, w, b):
    c = u.shape[-1]
    out = lax.conv_general_dilated(u, w[:, None, :].astype(u.dtype), window_strides=(1,),
                                   padding=((SSM_CONV - 1, 0),),
                                   dimension_numbers=('NWC', 'WIO', 'NWC'),
                                   feature_group_count=c)
    return out + b.astype(u.dtype)


def ssd_chunked(xh, dt, a, bm, cm):
    bsz, s = xh.shape[:2]
    nc = s // SSM_CHUNK
    L = SSM_CHUNK

    def chunk(t):
        return t.reshape((bsz, nc, L) + t.shape[2:])

    xc = chunk(xh * dt[..., None])
    a_cs = jnp.cumsum(chunk(dt * a), axis=2)
    bc, cc = chunk(bm), chunk(cm)
    a_t = jnp.moveaxis(a_cs, 2, -1)
    diff = a_t[..., :, None] - a_t[..., None, :]
    tri = jnp.tril(jnp.ones((L, L), dtype=bool))
    decay = jnp.exp(jnp.where(tri, diff, -jnp.inf))
    cb = jnp.einsum('bclgn,bcsgn->bcgls', cc, bc)
    y_diag = jnp.einsum('bcgls,bcgrls,bcsgrp->bclgrp', cb, decay, xc)
    decay_states = jnp.exp(a_cs[:, :, -1:] - a_cs)
    states = jnp.einsum('bclgn,bclgr,bclgrp->bcgrpn', bc, decay_states, xc)
    a_tot = a_cs[:, :, -1]

    def step(h, inp):
        st, at = inp
        h_new = h * jnp.exp(at)[..., None, None] + st
        return h_new, h

    h0 = jnp.zeros_like(states[:, 0])
    _, prev = lax.scan(step, h0, (jnp.moveaxis(states, 1, 0), jnp.moveaxis(a_tot, 1, 0)))
    prev = jnp.moveaxis(prev, 0, 1)
    y_off = jnp.einsum('bclgn,bcgrpn,bclgr->bclgrp', cc, prev, jnp.exp(a_cs))
    return (y_diag + y_off).reshape(xh.shape)


def compress(t, pe, w1, w2):
    bsz, s, g, hd = t.shape
    ncmp = (s - CMP_LEN) // CMP_STRIDE + 1
    idx = jnp.arange(ncmp)[:, None] * CMP_STRIDE + jnp.arange(CMP_LEN)[None, :]
    blocks = t[:, idx] + pe[:, None, :].astype(t.dtype)
    blocks = jnp.transpose(blocks, (0, 1, 3, 2, 4)).reshape(bsz, ncmp, g, CMP_LEN * hd)
    return jax.nn.silu(blocks @ w1) @ w2


def nsa_sweep(q_nope, q_rot, kc, vc, ks, vs, kw, vw):
    bsz, s, g, r, hd = q_rot.shape
    ncmp = kc.shape[1]
    nsel = s // SEL_BLOCK
    topk = min(SEL_TOPK, nsel)
    nq = s // Q_BLOCK
    scale = hd ** -0.5
    cmp_end = jnp.arange(ncmp) * CMP_STRIDE + CMP_LEN - 1
    ci = jnp.arange(ncmp)[:, None]
    sj = jnp.arange(nsel)[None, :]
    overlap = ((ci * CMP_STRIDE < (sj + 1) * SEL_BLOCK) &
               (ci * CMP_STRIDE + CMP_LEN > sj * SEL_BLOCK)).astype(jnp.float32)
    ks_blk = jnp.transpose(ks.reshape(bsz, nsel, SEL_BLOCK, g, hd), (0, 3, 1, 2, 4))
    vs_blk = jnp.transpose(vs.reshape(bsz, nsel, SEL_BLOCK, g, hd), (0, 3, 1, 2, 4))
    kw_pad = jnp.pad(kw, ((0, 0), (WINDOW, 0), (0, 0), (0, 0)))
    vw_pad = jnp.pad(vw, ((0, 0), (WINDOW, 0), (0, 0), (0, 0)))
    b_ix = jnp.arange(bsz)[:, None, None, None]
    g_ix = jnp.arange(g)[None, :, None, None]
    blk = jnp.arange(nsel)

    def block_fn(qi):
        q0 = qi * Q_BLOCK
        t = q0 + jnp.arange(Q_BLOCK)
        qn = lax.dynamic_slice_in_dim(q_nope, q0, Q_BLOCK, axis=1)
        qr = lax.dynamic_slice_in_dim(q_rot, q0, Q_BLOCK, axis=1)
        s_c = jnp.einsum('bqgrd,bkgd->bgrqk', qn, kc).astype(jnp.float32) * scale
        p_c = masked_softmax(s_c, cmp_end[None, :] <= t[:, None])
        o_c = jnp.einsum('bgrqk,bkgd->bqgrd', p_c.astype(vc.dtype), vc)
        imp = jnp.einsum('bgrqk,kj->bgqj', p_c, overlap)
        cur = t // SEL_BLOCK
        valid = blk[None, :] <= cur[:, None]
        forced = valid & ((blk[None, :] == 0) | (blk[None, :] == cur[:, None]) | (blk[None, :] == cur[:, None] - 1))
        imp = jnp.where(forced, BIG, jnp.where(valid, imp, NEG))
        _, sel = lax.top_k(imp, topk)
        k_sel = ks_blk[b_ix, g_ix, sel].reshape(bsz, g, Q_BLOCK, topk * SEL_BLOCK, hd)
        v_sel = vs_blk[b_ix, g_ix, sel].reshape(bsz, g, Q_BLOCK, topk * SEL_BLOCK, hd)
        tok = (sel[..., None] * SEL_BLOCK + jnp.arange(SEL_BLOCK)).reshape(bsz, g, Q_BLOCK, topk * SEL_BLOCK)
        m_s = (tok <= t[:, None])[:, :, None]
        s_s = jnp.einsum('bqgrd,bgqkd->bgrqk', qr, k_sel).astype(jnp.float32) * scale
        p_s = masked_softmax(s_s, m_s)
        o_s = jnp.einsum('bgrqk,bgqkd->bqgrd', p_s.astype(v_sel.dtype), v_sel)
        kwb = lax.dynamic_slice_in_dim(kw_pad, q0, Q_BLOCK + WINDOW, axis=1)
        vwb = lax.dynamic_slice_in_dim(vw_pad, q0, Q_BLOCK + WINDOW, axis=1)
        kp = q0 - WINDOW + jnp.arange(Q_BLOCK + WINDOW)
        m_w = (kp[None, :] <= t[:, None]) & (kp[None, :] > t[:, None] - WINDOW) & (kp[None, :] >= 0)
        s_w = jnp.einsum('bqgrd,bkgd->bgrqk', qr, kwb).astype(jnp.float32) * scale
        p_w = masked_softmax(s_w, m_w)
        o_w = jnp.einsum('bgrqk,bkgd->bqgrd', p_w.astype(vwb.dtype), vwb)
        return o_c, o_s, o_w

    outs = lax.map(block_fn, jnp.arange(nq))

    def unblock(o):
        return jnp.moveaxis(o, 0, 1).reshape(bsz, s, g, r, hd)

    return unblock(outs[0]), unblock(outs[1]), unblock(outs[2])


def hybrid_layer(x, cos, sin, g_norm_mix, w_in, conv_w, conv_b, dt_bias, a_log, d_skip,
                 g_ssm_norm, w_ssm_branch, cmp_pe_k, cmp_pe_v, w_cmp_k1, w_cmp_k2,
                 w_cmp_v1, w_cmp_v2, w_attn_branch, w_o, g_norm_mlp, w_up, w_down):
    bsz, s, _ = x.shape
    h = rms_norm(x, g_norm_mix)
    proj = h @ w_in
    z, xbc, dt_raw, q, kv, att_g, mix_g = jnp.split(proj, SPLIT_POINTS, axis=-1)

    xbc = jax.nn.silu(causal_dwconv(xbc, conv_w, conv_b))
    xs, bm, cm = jnp.split(xbc, [D_INNER, D_INNER + SSM_GROUPS * SSM_STATE], axis=-1)
    xs = xs.reshape(bsz, s, SSM_GROUPS, SSM_HPG, SSM_HEAD_DIM)
    bm = bm.reshape(bsz, s, SSM_GROUPS, SSM_STATE)
    cm = cm.reshape(bsz, s, SSM_GROUPS, SSM_STATE)
    dt = jax.nn.softplus(dt_raw.astype(jnp.float32) + dt_bias.astype(jnp.float32))
    dt = dt.reshape(bsz, s, SSM_GROUPS, SSM_HPG)
    a = -jnp.exp(a_log.astype(jnp.float32)).reshape(SSM_GROUPS, SSM_HPG)
    y = ssd_chunked(xs, dt, a, bm, cm) + d_skip.reshape(SSM_GROUPS, SSM_HPG)[..., None] * xs
    y = y.reshape(bsz, s, D_INNER).astype(x.dtype) * jax.nn.silu(z)
    y = rms_norm(y.reshape(bsz, s, SSM_GROUPS, D_INNER // SSM_GROUPS),
                 g_ssm_norm.reshape(SSM_GROUPS, D_INNER // SSM_GROUPS)).reshape(bsz, s, D_INNER)
    y_ssm = y @ w_ssm_branch

    q = q.reshape(bsz, s, ATT_KV_GROUPS, ATT_HPG, ATT_HEAD_DIM)
    kv = kv.reshape(bsz, s, 6, ATT_KV_GROUPS, ATT_HEAD_DIM)
    k_c, v_c, k_s, v_s, k_w, v_w = (kv[:, :, i] for i in range(6))
    q_r = rope_partial(q, cos[:, :, None, None, :], sin[:, :, None, None, :])
    k_s = rope_partial(k_s, cos[:, :, None, :], sin[:, :, None, :])
    k_w = rope_partial(k_w, cos[:, :, None, :], sin[:, :, None, :])
    kc = compress(k_c, cmp_pe_k, w_cmp_k1, w_cmp_k2)
    vc = compress(v_c, cmp_pe_v, w_cmp_v1, w_cmp_v2)
    o_c, o_s, o_w = nsa_sweep(q, q_r, kc, vc, k_s, v_s, k_w, v_w)
    gate = jax.nn.sigmoid(att_g.astype(jnp.float32)).reshape(bsz, s, 3, ATT_KV_GROUPS, ATT_HPG, 1)
    o = gate[:, :, 0] * o_c + gate[:, :, 1] * o_s + gate[:, :, 2] * o_w
    y_att = o.reshape(bsz, s, D_ATT).astype(x.dtype) @ w_attn_branch

    g_ssm, g_att = jnp.split(jax.nn.sigmoid(mix_g.astype(jnp.float32)), 2, axis=-1)
    mixed = (g_ssm * y_ssm + g_att * y_att).astype(x.dtype)
    x = x + mixed @ w_o

    h = rms_norm(x, g_norm_mlp)
    x = x + jnp.square(jax.nn.relu(h @ w_up)) @ w_down
    return x


def setup_inputs(seed: int = 0) -> dict:
    key = jax.random.key(seed)
    ks = jax.random.split(key, 24)
    f32 = jnp.float32

    def nrm(k, shape, fan_in):
        return jax.random.normal(k, shape, f32) * (fan_in ** -0.5)

    def gain(k, shape):
        return 1.0 + 0.02 * jax.random.normal(k, shape, f32)

    x = jax.random.normal(ks[0], (BATCH, SEQ, D_MODEL), f32)
    positions = jnp.broadcast_to(jnp.arange(SEQ, dtype=jnp.int32), (BATCH, SEQ)).astype(jnp.int32)
    dt0 = jnp.exp(jax.random.uniform(ks[6], (DEPTH, SSM_HEADS), f32) * (jnp.log(0.1) - jnp.log(0.001)) + jnp.log(0.001))
    dt_bias = dt0 + jnp.log(-jnp.expm1(-dt0))
    a_log = jnp.log(jax.random.uniform(ks[7], (DEPTH, SSM_HEADS), f32, 1.0, 16.0))
    return {
        'x': x,
        'positions': positions,
        'g_norm_mix': gain(ks[1], (DEPTH, D_MODEL)),
        'w_in': nrm(ks[2], (DEPTH, D_MODEL, IN_COLS), D_MODEL),
        'conv_w': nrm(ks[3], (DEPTH, SSM_CONV, SSM_XBC), SSM_CONV),
        'conv_b': 0.02 * jax.random.normal(ks[4], (DEPTH, SSM_XBC), f32),
        'dt_bias': dt_bias,
        'a_log': a_log,
        'd_skip': 1.0 + 0.1 * jax.random.normal(ks[5], (DEPTH, SSM_HEADS), f32),
        'g_ssm_norm': gain(ks[8], (DEPTH, D_INNER)),
        'w_ssm_branch': nrm(ks[9], (DEPTH, D_INNER, D_MODEL), D_INNER),
        'cmp_pe_k': 0.1 * jax.random.normal(ks[10], (DEPTH, CMP_LEN, ATT_HEAD_DIM), f32),
        'cmp_pe_v': 0.1 * jax.random.normal(ks[11], (DEPTH, CMP_LEN, ATT_HEAD_DIM), f32),
        'w_cmp_k1': nrm(ks[12], (DEPTH, CMP_LEN * ATT_HEAD_DIM, CMP_HIDDEN), CMP_LEN * ATT_HEAD_DIM),
        'w_cmp_k2': nrm(ks[13], (DEPTH, CMP_HIDDEN, ATT_HEAD_DIM), CMP_HIDDEN),
        'w_cmp_v1': nrm(ks[14], (DEPTH, CMP_LEN * ATT_HEAD_DIM, CMP_HIDDEN), CMP_LEN * ATT_HEAD_DIM),
        'w_cmp_v2': nrm(ks[15], (DEPTH, CMP_HIDDEN, ATT_HEAD_DIM), CMP_HIDDEN),
        'w_attn_branch': nrm(ks[16], (DEPTH, D_ATT, D_MODEL), D_ATT),
        'w_o': nrm(ks[17], (DEPTH, D_MODEL, D_MODEL), D_MODEL),
        'g_norm_mlp': gain(ks[18], (DEPTH, D_MODEL)),
        'w_up': nrm(ks[19], (DEPTH, D_MODEL, MLP_HIDDEN), D_MODEL),
        'w_down': nrm(ks[20], (DEPTH, MLP_HIDDEN, D_MODEL), MLP_HIDDEN),
        'g_norm_final': gain(ks[21], (D_MODEL,)),
    }


def reference(x, positions, g_norm_mix, w_in, conv_w, conv_b, dt_bias, a_log, d_skip,
              g_ssm_norm, w_ssm_branch, cmp_pe_k, cmp_pe_v, w_cmp_k1, w_cmp_k2,
              w_cmp_v1, w_cmp_v2, w_attn_branch, w_o, g_norm_mlp, w_up, w_down, g_norm_final):
    inv_freq = ROPE_THETA ** (-jnp.arange(0, ROPE_DIM, 2, dtype=jnp.float32) / ROPE_DIM)
    ang = positions.astype(jnp.float32)[..., None] * inv_freq
    cos, sin = jnp.cos(ang), jnp.sin(ang)
    for l in range(DEPTH):
        x = hybrid_layer(x, cos, sin, g_norm_mix[l], w_in[l], conv_w[l], conv_b[l], dt_bias[l],
                         a_log[l], d_skip[l], g_ssm_norm[l], w_ssm_branch[l], cmp_pe_k[l],
                         cmp_pe_v[l], w_cmp_k1[l], w_cmp_k2[l], w_cmp_v1[l], w_cmp_v2[l],
                         w_attn_branch[l], w_o[l], g_norm_mlp[l], w_up[l], w_down[l])
    return rms_norm(x, g_norm_final)
```

```python
import functools

import jax
import jax.numpy as jnp
from jax import lax
from jax.experimental import pallas as pl
from jax.experimental.pallas import tpu as pltpu

F32 = jnp.float32
BF16 = jnp.bfloat16

D_MODEL = 1024
D_INNER = 2048
SSM_HEADS = 32
SSM_GROUPS = 8
SSM_HPG = 4
SSM_HEAD_DIM = 64
SSM_STATE = 128
SSM_CONV = 4
CHUNK = 128
SSM_XBC = 4096
HEAD_DIM = 64
ATT_HEADS = 16
KV_GROUPS = 4
ATT_HPG = 4
CMP_LEN = 32
CMP_STRIDE = 16
CMP_HIDDEN = 256
SEL_BLOCK = 64
SEL_TOPK = 16
WINDOW = 512
Q_BLOCK = 128
ROPE_THETA = 500000.0
ROPE_DIM = 16
MLP_HIDDEN = 4096
NORM_EPS = 1e-6
NEG = -1e30
BIG = 1e30
REMOVED = -3e38
SCALE = HEAD_DIM ** -0.5

LANES = 128
ROW_TILE = 512
VMEM_LIMIT = 56 * 1024 * 1024

RAW_Z, RAW_XS, RAW_BC, RAW_MIX, RAW_MISC = 0, 2048, 4096, 6144, 8192
RAW_COLS = 8320
RAW_TN = 1664
MISC_GATE0 = SSM_HEADS


def _cparams(sem):
    return pltpu.CompilerParams(dimension_semantics=sem, vmem_limit_bytes=VMEM_LIMIT)


def _sigmoid(x):
    return 1.0 / (1.0 + jnp.exp(-x))


def _split3(x):
    hi = x.astype(BF16)
    r1 = x - hi.astype(F32)
    mid = r1.astype(BF16)
    lo = (r1 - mid.astype(F32)).astype(BF16)
    return hi, mid, lo


def _rmsnorm_kernel(x_ref, g_ref, o_ref):
    x = x_ref[...]
    ms = jnp.mean(x * x, axis=-1, keepdims=True)
    o_ref[...] = (x * lax.rsqrt(ms + NORM_EPS) * g_ref[...]).astype(o_ref.dtype)


def _rmsnorm(x2d, g):
    t, d = x2d.shape
    return pl.pallas_call(
        _rmsnorm_kernel,
        grid=(t // ROW_TILE,),
        in_specs=[pl.BlockSpec((ROW_TILE, d), lambda i: (i, 0)),
                  pl.BlockSpec((1, d), lambda i: (0, 0))],
        out_specs=pl.BlockSpec((ROW_TILE, d), lambda i: (i, 0)),
        out_shape=jax.ShapeDtypeStruct((t, d), BF16),
        compiler_params=_cparams(("parallel",)),
        name="rmsnorm",
    )(x2d, g.reshape(1, d))


def _proj_raw_kernel(a_ref, w_ref, o_ref):
    o_ref[...] = jnp.dot(a_ref[...], w_ref[...], preferred_element_type=F32)


def _proj_raw(h, w):
    t, k = h.shape
    n = w.shape[1]
    return pl.pallas_call(
        _proj_raw_kernel,
        grid=(n // RAW_TN, t // ROW_TILE),
        in_specs=[pl.BlockSpec((ROW_TILE, k), lambda j, i: (i, 0)),
                  pl.BlockSpec((k, RAW_TN), lambda j, i: (0, j))],
        out_specs=pl.BlockSpec((ROW_TILE, RAW_TN), lambda j, i: (i, j)),
        out_shape=jax.ShapeDtypeStruct((t, n), F32),
        compiler_params=_cparams(("parallel", "parallel")),
        name="proj_raw",
    )(h, w)


def _rope128(t, cos_e, sin_a, sin_b):
    return t * cos_e + pltpu.roll(t, LANES - ROPE_DIM // 2, 1) * sin_a + pltpu.roll(t, ROPE_DIM // 2, 1) * sin_b


def _proj_q_kernel(a_ref, w_ref, cos_ref, sa_ref, sb_ref, qn_ref, qr_ref):
    acc = jnp.dot(a_ref[...], w_ref[...], preferred_element_type=F32)
    cos_e, sin_a, sin_b = cos_ref[...], sa_ref[...], sb_ref[...]
    for c in range(acc.shape[1] // LANES):
        t = acc[:, c * LANES:(c + 1) * LANES]
        qn_ref[0, c * LANES:(c + 1) * LANES, :] = (t * SCALE).T.astype(qn_ref.dtype)
        qr_ref[0, c * LANES:(c + 1) * LANES, :] = (_rope128(t, cos_e, sin_a, sin_b) * SCALE).T.astype(qr_ref.dtype)


def _proj_q(h, w, tabs, bsz, s):
    t, k = h.shape
    n = w.shape[1]
    spb = s // ROW_TILE
    tab_spec = pl.BlockSpec((ROW_TILE, LANES), lambda i: (i, 0))
    out_spec = pl.BlockSpec((1, n, ROW_TILE), lambda i: (i // spb, 0, i % spb))
    return pl.pallas_call(
        _proj_q_kernel,
        grid=(t // ROW_TILE,),
        in_specs=[pl.BlockSpec((ROW_TILE, k), lambda i: (i, 0)),
                  pl.BlockSpec((k, n), lambda i: (0, 0)),
                  tab_spec, tab_spec, tab_spec],
        out_specs=[out_spec, out_spec],
        out_shape=[jax.ShapeDtypeStruct((bsz, n, s), BF16)] * 2,
        compiler_params=_cparams(("parallel",)),
        name="proj_q",
    )(h, w, *tabs)


def _proj_kv_kernel(a_ref, w_ref, cos_ref, sa_ref, sb_ref, kvc_ref, ks_ref, kw_ref, vst_ref, vwt_ref):
    acc = jnp.dot(a_ref[...], w_ref[...], preferred_element_type=F32)
    cos_e, sin_a, sin_b = cos_ref[...], sa_ref[...], sb_ref[...]
    gw = KV_GROUPS * HEAD_DIM
    for n in range(2 * KV_GROUPS):
        kvc_ref[0, n] = acc[:, n * HEAD_DIM:(n + 1) * HEAD_DIM]
    for base, k_ref, vt_ref in ((2 * gw, ks_ref, vst_ref), (4 * gw, kw_ref, vwt_ref)):
        for c in range(gw // LANES):
            kk = _rope128(acc[:, base + c * LANES: base + (c + 1) * LANES], cos_e, sin_a, sin_b)
            vv = acc[:, base + gw + c * LANES: base + gw + (c + 1) * LANES].T
            for half in range(2):
                g = 2 * c + half
                k_ref[0, g] = kk[:, half * HEAD_DIM:(half + 1) * HEAD_DIM].astype(k_ref.dtype)
                for j in range(ROW_TILE // LANES):
                    vt_ref[0, g, j] = vv[half * HEAD_DIM:(half + 1) * HEAD_DIM,
                                         j * LANES:(j + 1) * LANES].astype(vt_ref.dtype)


def _proj_kv(h, w, tabs, bsz, s):
    t, k = h.shape
    n = w.shape[1]
    spb = s // ROW_TILE
    tpr = ROW_TILE // LANES
    tab_spec = pl.BlockSpec((ROW_TILE, LANES), lambda i: (i, 0))
    k_spec = pl.BlockSpec((1, KV_GROUPS, ROW_TILE, HEAD_DIM), lambda i: (i // spb, 0, i % spb, 0))
    vt_spec = pl.BlockSpec((1, KV_GROUPS, tpr, HEAD_DIM, LANES), lambda i: (i // spb, 0, i % spb, 0, 0))
    return pl.pallas_call(
        _proj_kv_kernel,
        grid=(t // ROW_TILE,),
        in_specs=[pl.BlockSpec((ROW_TILE, k), lambda i: (i, 0)),
                  pl.BlockSpec((k, n), lambda i: (0, 0)),
                  tab_spec, tab_spec, tab_spec],
        out_specs=[pl.BlockSpec((1, 2 * KV_GROUPS, ROW_TILE, HEAD_DIM), lambda i: (i // spb, 0, i % spb, 0)),
                   k_spec, k_spec, vt_spec, vt_spec],
        out_shape=[jax.ShapeDtypeStruct((bsz, 2 * KV_GROUPS, s, HEAD_DIM), F32),
                   jax.ShapeDtypeStruct((bsz, KV_GROUPS, s, HEAD_DIM), BF16),
                   jax.ShapeDtypeStruct((bsz, KV_GROUPS, s, HEAD_DIM), BF16),
                   jax.ShapeDtypeStruct((bsz, KV_GROUPS, s // LANES, HEAD_DIM, LANES), BF16),
                   jax.ShapeDtypeStruct((bsz, KV_GROUPS, s // LANES, HEAD_DIM, LANES), BF16)],
        compiler_params=_cparams(("parallel",)),
        name="proj_kv",
    )(h, w, *tabs)


def _ssd_kernel(z_ref, xs_ref, bc_ref, misc_ref, convw_ref, convb_ref, dtb_ref, alog_ref,
                dskip_ref, gnorm_ref, selb_ref, sele_ref, wout_ref, o_ref,
                ext_ref, act_ref, state_ref):
    c = pl.program_id(1)
    L = CHUNK
    gw = SSM_HPG * SSM_HEAD_DIM

    @pl.when(c == 0)
    def _():
        state_ref[...] = jnp.zeros_like(state_ref)
        ext_ref[0:8, :] = jnp.zeros((8, SSM_XBC), F32)

    ext_ref[8:8 + L, 0:D_INNER] = xs_ref[...]
    ext_ref[8:8 + L, D_INNER:SSM_XBC] = bc_ref[...]
    cw = 512
    for cc in range(SSM_XBC // cw):
        cols = slice(cc * cw, (cc + 1) * cw)
        acc = convb_ref[:, cols] + convw_ref[0:1, cols] * ext_ref[5:5 + L, cols]
        for k in range(1, SSM_CONV):
            acc = acc + convw_ref[k:k + 1, cols] * ext_ref[5 + k:5 + k + L, cols]
        act_ref[:, cols] = acc * _sigmoid(acc)
    ext_ref[0:8, :] = ext_ref[L:L + 8, :]

    lane = lax.broadcasted_iota(jnp.int32, (L, LANES), 1)
    row = lax.broadcasted_iota(jnp.int32, (L, LANES), 0)
    head_lane = lane < SSM_HEADS
    raw = misc_ref[...] + dtb_ref[...]
    dt = jnp.where(head_lane, jnp.maximum(raw, 0.0) + jnp.log1p(jnp.exp(-jnp.abs(raw))), 0.0)
    a_row = jnp.where(head_lane[0:1], -jnp.exp(alog_ref[...]), 0.0)
    cs = dt * a_row
    sh = 1
    while sh < L:
        cs = cs + jnp.where(row >= sh, pltpu.roll(cs, sh, 0), 0.0)
        sh *= 2
    tot = cs[L - 1:L, :]
    w_state = dt * jnp.exp(tot - cs)
    cs_t, dt_t, w_t = cs.T, dt.T, w_state.T

    hi, mid, lo = _split3(cs)
    packed = (hi.astype(F32) + pltpu.roll(mid.astype(F32), SSM_HEADS, 1)
              + pltpu.roll(lo.astype(F32), 2 * SSM_HEADS, 1)).astype(BF16)
    col_b = jnp.dot(packed, selb_ref[...], preferred_element_type=F32)
    col_e = jnp.dot(packed, sele_ref[...], preferred_element_type=F32)
    exp_e = jnp.exp(col_e)
    etot_e = exp_e[L - 1:L, :]

    tri = lax.broadcasted_iota(jnp.int32, (L, L), 0) >= lax.broadcasted_iota(jnp.int32, (L, L), 1)
    brow = lax.broadcasted_iota(jnp.int32, (SSM_HPG * L, gw), 0) // L
    bcol = lax.broadcasted_iota(jnp.int32, (SSM_HPG * L, gw), 1) // SSM_HEAD_DIM
    blockmask = brow == bcol

    for g in range(SSM_GROUPS):
        xs_g = act_ref[:, g * gw:(g + 1) * gw]
        b_g = act_ref[:, D_INNER + g * SSM_STATE: D_INNER + (g + 1) * SSM_STATE]
        c_g = act_ref[:, D_INNER + SSM_GROUPS * SSM_STATE + g * SSM_STATE:
                      D_INNER + SSM_GROUPS * SSM_STATE + (g + 1) * SSM_STATE]
        c_bf = c_g.astype(BF16)
        cb = lax.dot_general(c_bf, b_g.astype(BF16), (((1,), (1,)), ((), ())),
                             preferred_element_type=F32)
        b_t = b_g.T
        tops, bots = [], []
        for j in range(SSM_HPG):
            h = g * SSM_HPG + j
            decay = jnp.exp(jnp.where(tri, col_b[:, h * L:(h + 1) * L] - cs_t[h:h + 1, :], NEG))
            tops.append((cb * decay * dt_t[h:h + 1, :]).astype(BF16))
            bots.append((b_t * w_t[h:h + 1, :]).astype(BF16))
        lhs = jnp.concatenate([jnp.concatenate(tops, axis=1), jnp.concatenate(bots, axis=1)], axis=0)
        xs_bf = xs_g.astype(BF16)
        xs_bd = jnp.where(blockmask, jnp.concatenate([xs_bf] * SSM_HPG, axis=0), jnp.zeros((), BF16))
        res = jnp.dot(lhs, xs_bd, preferred_element_type=F32)
        st = state_ref[g]
        y_off = jnp.dot(c_bf, st.astype(BF16), preferred_element_type=F32) * exp_e[:, g * gw:(g + 1) * gw]
        state_ref[g] = st * etot_e[:, g * gw:(g + 1) * gw] + res[L:2 * L]
        y = res[0:L] + y_off + dskip_ref[:, g * gw:(g + 1) * gw] * xs_g
        zg = z_ref[:, g * gw:(g + 1) * gw]
        y = y * (zg * _sigmoid(zg))
        ms = jnp.mean(y * y, axis=-1, keepdims=True)
        act_ref[:, g * gw:(g + 1) * gw] = y * lax.rsqrt(ms + NORM_EPS) * gnorm_ref[:, g * gw:(g + 1) * gw]

    o_ref[...] = jnp.dot(act_ref[:, 0:D_INNER].astype(BF16), wout_ref[...], preferred_element_type=F32)


def _ssd(raw, conv_w, conv_b, dt_bias, a_log, d_skip, g_ssm_norm, w_ssm, bsz, s):
    t = raw.shape[0]
    nc = s // CHUNK
    blk = lambda cb: pl.BlockSpec((CHUNK, 2048), lambda b, c, cb=cb: (b * nc + c, cb))
    const = lambda shape: pl.BlockSpec(shape, lambda b, c: (0,) * len(shape))
    k = jnp.arange(LANES)
    selb = ((k[:, None] < 3 * SSM_HEADS) & ((k[:, None] % SSM_HEADS) == (jnp.arange(SSM_HEADS * LANES)[None, :] // LANES)))
    sele = ((k[:, None] < 3 * SSM_HEADS) & ((k[:, None] % SSM_HEADS) == (jnp.arange(D_INNER)[None, :] // SSM_HEAD_DIM)))
    pad = lambda v: jnp.pad(v.astype(F32), (0, LANES - SSM_HEADS)).reshape(1, LANES)
    return pl.pallas_call(
        _ssd_kernel,
        grid=(bsz, nc),
        in_specs=[blk(RAW_Z // 2048), blk(RAW_XS // 2048), blk(RAW_BC // 2048),
                  pl.BlockSpec((CHUNK, LANES), lambda b, c: (b * nc + c, RAW_MISC // LANES)),
                  const((SSM_CONV, SSM_XBC)), const((1, SSM_XBC)), const((1, LANES)), const((1, LANES)),
                  const((1, D_INNER)), const((1, D_INNER)),
                  const((LANES, SSM_HEADS * LANES)), const((LANES, D_INNER)),
                  const((D_INNER, D_MODEL))],
        out_specs=pl.BlockSpec((CHUNK, D_MODEL), lambda b, c: (b * nc + c, 0)),
        out_shape=jax.ShapeDtypeStruct((t, D_MODEL), F32),
        scratch_shapes=[pltpu.VMEM((CHUNK + 8, SSM_XBC), F32),
                        pltpu.VMEM((CHUNK, SSM_XBC), F32),
                        pltpu.VMEM((SSM_GROUPS, SSM_STATE, SSM_HPG * SSM_HEAD_DIM), F32)],
        compiler_params=_cparams(("parallel", "arbitrary")),
        name="ssd_mixer",
    )(raw, raw, raw, raw, conv_w.astype(F32), conv_b.reshape(1, SSM_XBC).astype(F32),
      pad(dt_bias), pad(a_log), jnp.repeat(d_skip.astype(F32), SSM_HEAD_DIM).reshape(1, D_INNER),
      g_ssm_norm.reshape(1, D_INNER).astype(F32), selb.astype(BF16), sele.astype(BF16), w_ssm.astype(BF16))


def _compress_kernel(u_ref, pelo_ref, pehi_ref, w1_ref, w2_ref, o_ref):
    u = u_ref[0]
    half = CMP_STRIDE * HEAD_DIM
    a = jnp.dot((u + pelo_ref[0, 0]).astype(BF16), w1_ref[0, 0:half, :], preferred_element_type=F32)
    b = jnp.dot((u + pehi_ref[0, 0]).astype(BF16), w1_ref[0, half:2 * half, :], preferred_element_type=F32)
    pre = a + pltpu.roll(b, u.shape[0] - 1, 0)
    hidden = pre * _sigmoid(pre)
    o_ref[0] = jnp.dot(hidden.astype(BF16), w2_ref[0], preferred_element_type=F32).astype(o_ref.dtype)


def _compress(u, pe, w1, w2):
    n, rows, width = u.shape
    kv_of = lambda i: (i // KV_GROUPS) % 2
    return pl.pallas_call(
        _compress_kernel,
        grid=(n,),
        in_specs=[pl.BlockSpec((1, rows, width), lambda i: (i, 0, 0)),
                  pl.BlockSpec((1, 1, 1, width), lambda i: (kv_of(i), 0, 0, 0)),
                  pl.BlockSpec((1, 1, 1, width), lambda i: (kv_of(i), 1, 0, 0)),
                  pl.BlockSpec((1, 2 * width, CMP_HIDDEN), lambda i: (kv_of(i), 0, 0)),
                  pl.BlockSpec((1, CMP_HIDDEN, HEAD_DIM), lambda i: (kv_of(i), 0, 0))],
        out_specs=pl.BlockSpec((1, rows, HEAD_DIM), lambda i: (i, 0, 0)),
        out_shape=jax.ShapeDtypeStruct((n, rows, HEAD_DIM), BF16),
        compiler_params=_cparams(("parallel",)),
        name="compress",
    )(u, pe, pe, w1, w2)


def _online_update(carry, s_m, vt):
    m, l, acc = carry
    m_new = jnp.maximum(m, jnp.max(s_m, axis=0, keepdims=True))
    alpha = jnp.exp(m - m_new)
    p = jnp.exp(s_m - m_new)
    l = l * alpha + jnp.sum(p, axis=0, keepdims=True)
    acc = acc * alpha + jnp.dot(vt, p.astype(BF16), preferred_element_type=F32)
    return m_new, l, acc


def _nsa_kernel(qn_ref, qr_ref, kc_ref, vct_ref, ks_ref, vst_ref, kw_ref, vwt_ref, gate_ref, ovt_ref,
                o_ref, sel_ref, gt_ref, *, topk):
    g = pl.program_id(1)
    qb = pl.program_id(2)
    nq = ATT_HPG * Q_BLOCK
    ncmp = kc_ref.shape[1]
    nsel = ovt_ref.shape[0]
    q0 = qb * Q_BLOCK
    tq = q0 + lax.broadcasted_iota(jnp.int32, (1, nq), 1) % Q_BLOCK

    def heads_on_lanes(ref):
        return jnp.concatenate([ref[0, r * HEAD_DIM:(r + 1) * HEAD_DIM, :] for r in range(ATT_HPG)], axis=1)

    qn_t = heads_on_lanes(qn_ref)
    qr_t = heads_on_lanes(qr_ref)

    s_c = jnp.dot(kc_ref[0], qn_t, preferred_element_type=F32)
    cmp_end = lax.broadcasted_iota(jnp.int32, (ncmp, nq), 0) * CMP_STRIDE + (CMP_LEN - 1)
    mask_c = cmp_end <= tq
    s_cm = jnp.where(mask_c, s_c, NEG)
    m_c = jnp.max(s_cm, axis=0, keepdims=True)
    p_c = jnp.where(mask_c, jnp.exp(s_cm - m_c), 0.0)
    l_c = jnp.sum(p_c, axis=0, keepdims=True)
    p_c = p_c * jnp.where(l_c > 0.0, 1.0 / l_c, 0.0)
    o_c = jnp.dot(vct_ref[0], p_c.astype(BF16), preferred_element_type=F32)

    p_sum = p_c[:, 0:Q_BLOCK]
    for r in range(1, ATT_HPG):
        p_sum = p_sum + p_c[:, r * Q_BLOCK:(r + 1) * Q_BLOCK]
    imp = jnp.dot(ovt_ref[...], jnp.concatenate(_split3(p_sum), axis=0), preferred_element_type=F32)

    blk = lax.broadcasted_iota(jnp.int32, (nsel, Q_BLOCK), 0)
    cur = (q0 + lax.broadcasted_iota(jnp.int32, (nsel, Q_BLOCK), 1)) // SEL_BLOCK
    valid = blk <= cur
    forced = valid & ((blk == 0) | (blk == cur) | (blk == cur - 1))
    val = jnp.where(forced, BIG, jnp.where(valid, imp, NEG))
    chosen = jnp.zeros((nsel, Q_BLOCK), F32)
    blk_f = blk.astype(F32)
    for _ in range(topk):
        mx = jnp.max(val, axis=0, keepdims=True)
        first = jnp.min(jnp.where(val == mx, blk_f, float(nsel)), axis=0, keepdims=True)
        hit = blk_f == first
        chosen = jnp.where(hit, 1.0, chosen)
        val = jnp.where(hit, REMOVED, val)
    sel_ref[...] = jnp.concatenate([chosen] * ATT_HPG, axis=1)

    row = lax.broadcasted_iota(jnp.int32, (Q_BLOCK, nq), 0)
    init = (jnp.full((1, nq), NEG, F32), jnp.zeros((1, nq), F32), jnp.zeros((HEAD_DIM, nq), F32))
    bpt = Q_BLOCK // SEL_BLOCK

    def sel_step(kt, carry):
        k0 = pl.multiple_of(kt * Q_BLOCK, Q_BLOCK)
        s = jnp.dot(ks_ref[0, pl.ds(k0, Q_BLOCK), :], qr_t, preferred_element_type=F32)
        picked = jnp.where(row < SEL_BLOCK, sel_ref[pl.ds(bpt * kt, 1), :], sel_ref[pl.ds(bpt * kt + 1, 1), :])
        mask = (picked > 0.0) & (k0 + row <= tq)
        return _online_update(carry, jnp.where(mask, s, NEG), vst_ref[0, kt])

    _, l_s, acc_s = lax.fori_loop(0, qb + 1, sel_step, init)

    def win_step(kt, carry):
        k0 = pl.multiple_of(kt * Q_BLOCK, Q_BLOCK)
        s = jnp.dot(kw_ref[0, pl.ds(k0, Q_BLOCK), :], qr_t, preferred_element_type=F32)
        kp = k0 + row
        mask = (kp <= tq) & (kp > tq - WINDOW)
        return _online_update(carry, jnp.where(mask, s, NEG), vwt_ref[0, kt])

    _, l_w, acc_w = lax.fori_loop(jnp.maximum(qb - WINDOW // Q_BLOCK, 0), qb + 1, win_step, init)

    gt_ref[...] = _sigmoid(gate_ref[...]).T

    def gate_row(branch):
        base = MISC_GATE0 + branch * ATT_HEADS + g * ATT_HPG
        return jnp.concatenate([gt_ref[pl.ds(base + r, 1), :] for r in range(ATT_HPG)], axis=1)

    o = gate_row(0) * o_c + (gate_row(1) / l_s) * acc_s + (gate_row(2) / l_w) * acc_w
    for r in range(ATT_HPG):
        o_ref[0, r * HEAD_DIM:(r + 1) * HEAD_DIM, :] = o[:, r * Q_BLOCK:(r + 1) * Q_BLOCK].astype(o_ref.dtype)


def _nsa(qn_t, qr_t, kc, vc_t, ks, vs_t, kw, vw_t, raw, bsz, s):
    nqb = s // Q_BLOCK
    ncmp = s // CMP_STRIDE
    nsel = s // SEL_BLOCK
    topk = min(SEL_TOPK, nsel)
    gw = ATT_HPG * HEAD_DIM
    ci = jnp.arange(ncmp)[None, :]
    sj = jnp.arange(nsel)[:, None]
    ov_t = (ci * CMP_STRIDE < (sj + 1) * SEL_BLOCK) & (ci * CMP_STRIDE + CMP_LEN > sj * SEL_BLOCK) \
        & (ci < ncmp - 1)
    ov_t3 = jnp.concatenate([ov_t.astype(BF16)] * 3, axis=1)
    q_spec = pl.BlockSpec((1, gw, Q_BLOCK), lambda b, g, i: (b, g, i))
    per_bg = lambda shape: pl.BlockSpec((1,) + shape, lambda b, g, i: (b * KV_GROUPS + g,) + (0,) * len(shape))
    return pl.pallas_call(
        functools.partial(_nsa_kernel, topk=topk),
        grid=(bsz, KV_GROUPS, nqb),
        in_specs=[q_spec, q_spec,
                  per_bg((ncmp, HEAD_DIM)), per_bg((HEAD_DIM, ncmp)),
                  per_bg((s, HEAD_DIM)), per_bg((s // LANES, HEAD_DIM, LANES)),
                  per_bg((s, HEAD_DIM)), per_bg((s // LANES, HEAD_DIM, LANES)),
                  pl.BlockSpec((Q_BLOCK, LANES), lambda b, g, i: (b * nqb + i, RAW_MISC // LANES)),
                  pl.BlockSpec((nsel, 3 * ncmp), lambda b, g, i: (0, 0))],
        out_specs=pl.BlockSpec((1, gw, Q_BLOCK), lambda b, g, i: (b, g, i)),
        out_shape=jax.ShapeDtypeStruct((bsz, ATT_HEADS * HEAD_DIM, s), BF16),
        scratch_shapes=[pltpu.VMEM((nsel, ATT_HPG * Q_BLOCK), F32),
                        pltpu.VMEM((LANES, Q_BLOCK), F32)],
        compiler_params=_cparams(("parallel", "parallel", "arbitrary")),
        name="nsa_sweep",
    )(qn_t, qr_t, kc, vc_t, ks, vs_t, kw, vw_t, raw, ov_t3)


def _merge_kernel(ot_ref, yssm_ref, mix_ref, x_ref, wab_ref, wo_ref, o_ref):
    o = ot_ref[0].astype(F32).T.astype(BF16)
    y_att = jnp.dot(o, wab_ref[...], preferred_element_type=F32)
    gates = _sigmoid(mix_ref[...])
    mixed = gates[:, 0:D_MODEL] * yssm_ref[...] + gates[:, D_MODEL:2 * D_MODEL] * y_att
    o_ref[...] = x_ref[...] + jnp.dot(mixed.astype(BF16), wo_ref[...], preferred_element_type=F32)


def _merge(o_t, y_ssm, raw, x2d, w_ab, w_o, s):
    t = x2d.shape[0]
    spb = s // ROW_TILE
    row_spec = pl.BlockSpec((ROW_TILE, D_MODEL), lambda i: (i, 0))
    w_spec = pl.BlockSpec((D_MODEL, D_MODEL), lambda i: (0, 0))
    return pl.pallas_call(
        _merge_kernel,
        grid=(t // ROW_TILE,),
        in_specs=[pl.BlockSpec((1, D_MODEL, ROW_TILE), lambda i: (i // spb, 0, i % spb)),
                  row_spec,
                  pl.BlockSpec((ROW_TILE, 2 * D_MODEL), lambda i: (i, RAW_MIX // (2 * D_MODEL))),
                  row_spec, w_spec, w_spec],
        out_specs=row_spec,
        out_shape=jax.ShapeDtypeStruct((t, D_MODEL), F32),
        compiler_params=_cparams(("parallel",)),
        name="merge",
    )(o_t, y_ssm, raw, x2d, w_ab, w_o)


def _mlp_kernel(x_ref, g1_ref, wup_ref, wdn_ref, g2_ref, o_ref):
    x = x_ref[...]
    ms = jnp.mean(x * x, axis=-1, keepdims=True)
    h = (x * lax.rsqrt(ms + NORM_EPS) * g1_ref[...]).astype(BF16)
    up = jnp.maximum(jnp.dot(h, wup_ref[...], preferred_element_type=F32), 0.0)
    y = x + jnp.dot((up * up).astype(BF16), wdn_ref[...], preferred_element_type=F32)
    ms2 = jnp.mean(y * y, axis=-1, keepdims=True)
    o_ref[...] = y * lax.rsqrt(ms2 + NORM_EPS) * g2_ref[...]


def _mlp(x1, g1, w_up, w_down, g2):
    t = x1.shape[0]
    tm = ROW_TILE // 2
    row_spec = pl.BlockSpec((tm, D_MODEL), lambda i: (i, 0))
    g_spec = pl.BlockSpec((1, D_MODEL), lambda i: (0, 0))
    return pl.pallas_call(
        _mlp_kernel,
        grid=(t // tm,),
        in_specs=[row_spec, g_spec,
                  pl.BlockSpec((D_MODEL, MLP_HIDDEN), lambda i: (0, 0)),
                  pl.BlockSpec((MLP_HIDDEN, D_MODEL), lambda i: (0, 0)),
                  g_spec],
        out_specs=row_spec,
        out_shape=jax.ShapeDtypeStruct((t, D_MODEL), F32),
        compiler_params=_cparams(("parallel",)),
        name="mlp",
    )(x1, g1.reshape(1, D_MODEL), w_up, w_down, g2.reshape(1, D_MODEL))


def _rope_tables(positions):
    half = ROPE_DIM // 2
    inv_freq = ROPE_THETA ** (-jnp.arange(0, ROPE_DIM, 2, dtype=F32) / ROPE_DIM)
    ang = positions.astype(F32).reshape(-1, 1) * inv_freq[None, :]
    cos, sin = jnp.cos(ang), jnp.sin(ang)
    t = cos.shape[0]
    rest = HEAD_DIM - ROPE_DIM
    cos_h = jnp.concatenate([cos, cos, jnp.ones((t, rest), F32)], axis=1)
    sa_h = jnp.concatenate([-sin, jnp.zeros((t, half + rest), F32)], axis=1)
    sb_h = jnp.concatenate([jnp.zeros((t, half), F32), sin, jnp.zeros((t, rest), F32)], axis=1)
    two = lambda a: jnp.concatenate([a, a], axis=1)
    return two(cos_h), two(sa_h), two(sb_h)


def kernel(x, positions, g_norm_mix, w_in, conv_w, conv_b, dt_bias, a_log, d_skip, g_ssm_norm, w_ssm_branch, cmp_pe_k, cmp_pe_v, w_cmp_k1, w_cmp_k2, w_cmp_v1, w_cmp_v2, w_attn_branch, w_o, g_norm_mlp, w_up, w_down, g_norm_final):
    bsz, s, d = x.shape
    assert d == D_MODEL and s % ROW_TILE == 0 and w_in.shape[0] == 1
    t = bsz * s
    x2d = x.reshape(t, d)
    tabs = _rope_tables(positions)

    w = w_in[0]
    o_z, o_xbc, o_dt, o_q, o_kv, o_ag, o_mg = 0, 2048, 6144, 6176, 7200, 8736, 8784
    misc_pad = jnp.zeros((d, RAW_COLS - RAW_MISC - SSM_HEADS - 3 * ATT_HEADS), w.dtype)
    w_raw = jnp.concatenate([w[:, o_z:o_dt], w[:, o_mg:], w[:, o_dt:o_q], w[:, o_ag:o_mg], misc_pad],
                            axis=1).astype(BF16)
    w_q = w[:, o_q:o_kv].astype(BF16)
    w_kv = w[:, o_kv:o_ag].astype(BF16)

    h = _rmsnorm(x2d, g_norm_mix[0])
    raw = _proj_raw(h, w_raw)
    qn_t, qr_t = _proj_q(h, w_q, tabs, bsz, s)
    kvc, ks, kw, vs_t, vw_t = _proj_kv(h, w_kv, tabs, bsz, s)

    y_ssm = _ssd(raw, conv_w[0], conv_b[0], dt_bias[0], a_log[0], d_skip[0], g_ssm_norm[0],
                 w_ssm_branch[0], bsz, s)

    rows = s // CMP_STRIDE
    u = kvc.reshape(bsz * 2 * KV_GROUPS, rows, CMP_STRIDE * HEAD_DIM)
    pe = jnp.stack([cmp_pe_k[0], cmp_pe_v[0]]).astype(F32).reshape(2, 2, 1, CMP_STRIDE * HEAD_DIM)
    w1 = jnp.stack([w_cmp_k1[0], w_cmp_v1[0]]).astype(BF16)
    w2 = jnp.stack([w_cmp_k2[0], w_cmp_v2[0]]).astype(BF16)
    cmp = _compress(u, pe, w1, w2).reshape(bsz, 2, KV_GROUPS, rows, HEAD_DIM)
    kc = cmp[:, 0].reshape(bsz * KV_GROUPS, rows, HEAD_DIM)
    vc_t = jnp.swapaxes(cmp[:, 1], -1, -2).reshape(bsz * KV_GROUPS, HEAD_DIM, rows)

    flat = lambda a: a.reshape((bsz * KV_GROUPS,) + a.shape[2:])
    o_t = _nsa(qn_t, qr_t, kc, vc_t, flat(ks), flat(vs_t), flat(kw), flat(vw_t), raw, bsz, s)

    x1 = _merge(o_t, y_ssm, raw, x2d, w_attn_branch[0].astype(BF16), w_o[0].astype(BF16), s)
    out = _mlp(x1, g_norm_mlp[0], w_up[0].astype(BF16), w_down[0].astype(BF16), g_norm_final)
    return out.reshape(bsz, s, d)
```

```python
import functools

import jax
import jax.numpy as jnp
from jax import lax
from jax.experimental import pallas as pl
from jax.experimental.pallas import tpu as pltpu

F32 = jnp.float32
BF16 = jnp.bfloat16

D_MODEL = 1024
D_INNER = 2048
SSM_HEADS = 32
SSM_GROUPS = 8
SSM_HPG = 4
SSM_HEAD_DIM = 64
SSM_STATE = 128
SSM_CONV = 4
CHUNK = 128
SSM_XBC = 4096
HEAD_DIM = 64
ATT_HEADS = 16
KV_GROUPS = 4
ATT_HPG = 4
CMP_LEN = 32
CMP_STRIDE = 16
CMP_HIDDEN = 256
SEL_BLOCK = 64
SEL_TOPK = 16
WINDOW = 512
Q_BLOCK = 128
ROPE_THETA = 500000.0
ROPE_DIM = 16
MLP_HIDDEN = 4096
NORM_EPS = 1e-6
NEG = -1e30
BIG = 1e30
REMOVED = -3e38
LOG2E = 1.4426950408889634
SCALE = HEAD_DIM ** -0.5 * LOG2E
KEY_STEP = 256

LANES = 128
ROW_TILE = 512
VMEM_LIMIT = 56 * 1024 * 1024

RAW_Z, RAW_XS, RAW_BC, RAW_MIX, RAW_MISC = 0, 2048, 4096, 6144, 8192
RAW_COLS = 8320
RAW_TN = 1664
MISC_GATE0 = SSM_HEADS


def _cparams(sem):
    return pltpu.CompilerParams(dimension_semantics=sem, vmem_limit_bytes=VMEM_LIMIT)


def _sigmoid(x):
    return 1.0 / (1.0 + jnp.exp(-x))


def _split3(x):
    hi = x.astype(BF16)
    r1 = x - hi.astype(F32)
    mid = r1.astype(BF16)
    lo = (r1 - mid.astype(F32)).astype(BF16)
    return hi, mid, lo


def _rmsnorm_kernel(x_ref, g_ref, o_ref):
    x = x_ref[...]
    ms = jnp.mean(x * x, axis=-1, keepdims=True)
    o_ref[...] = (x * lax.rsqrt(ms + NORM_EPS) * g_ref[...]).astype(o_ref.dtype)


def _rmsnorm(x2d, g):
    t, d = x2d.shape
    return pl.pallas_call(
        _rmsnorm_kernel,
        grid=(t // ROW_TILE,),
        in_specs=[pl.BlockSpec((ROW_TILE, d), lambda i: (i, 0)),
                  pl.BlockSpec((1, d), lambda i: (0, 0))],
        out_specs=pl.BlockSpec((ROW_TILE, d), lambda i: (i, 0)),
        out_shape=jax.ShapeDtypeStruct((t, d), BF16),
        compiler_params=_cparams(("parallel",)),
        name="rmsnorm",
    )(x2d, g.reshape(1, d))


def _proj_raw_kernel(a_ref, w_ref, o_ref):
    o_ref[...] = jnp.dot(a_ref[...], w_ref[...], preferred_element_type=F32)


def _proj_raw(h, w):
    t, k = h.shape
    n = w.shape[1]
    return pl.pallas_call(
        _proj_raw_kernel,
        grid=(n // RAW_TN, t // ROW_TILE),
        in_specs=[pl.BlockSpec((ROW_TILE, k), lambda j, i: (i, 0)),
                  pl.BlockSpec((k, RAW_TN), lambda j, i: (0, j))],
        out_specs=pl.BlockSpec((ROW_TILE, RAW_TN), lambda j, i: (i, j)),
        out_shape=jax.ShapeDtypeStruct((t, n), F32),
        compiler_params=_cparams(("parallel", "parallel")),
        name="proj_raw",
    )(h, w)


def _rope128(t, cos_e, sin_a, sin_b):
    return t * cos_e + pltpu.roll(t, LANES - ROPE_DIM // 2, 1) * sin_a + pltpu.roll(t, ROPE_DIM // 2, 1) * sin_b


def _proj_q_kernel(a_ref, w_ref, cos_ref, sa_ref, sb_ref, qn_ref, qr_ref):
    acc = jnp.dot(a_ref[...], w_ref[...], preferred_element_type=F32)
    cos_e, sin_a, sin_b = cos_ref[...], sa_ref[...], sb_ref[...]
    for c in range(acc.shape[1] // LANES):
        t = acc[:, c * LANES:(c + 1) * LANES]
        qn_ref[0, c * LANES:(c + 1) * LANES, :] = (t * SCALE).T.astype(qn_ref.dtype)
        qr_ref[0, c * LANES:(c + 1) * LANES, :] = (_rope128(t, cos_e, sin_a, sin_b) * SCALE).T.astype(qr_ref.dtype)


def _proj_q(h, w, tabs, bsz, s):
    t, k = h.shape
    n = w.shape[1]
    spb = s // ROW_TILE
    tab_spec = pl.BlockSpec((ROW_TILE, LANES), lambda i: (i, 0))
    out_spec = pl.BlockSpec((1, n, ROW_TILE), lambda i: (i // spb, 0, i % spb))
    return pl.pallas_call(
        _proj_q_kernel,
        grid=(t // ROW_TILE,),
        in_specs=[pl.BlockSpec((ROW_TILE, k), lambda i: (i, 0)),
                  pl.BlockSpec((k, n), lambda i: (0, 0)),
                  tab_spec, tab_spec, tab_spec],
        out_specs=[out_spec, out_spec],
        out_shape=[jax.ShapeDtypeStruct((bsz, n, s), BF16)] * 2,
        compiler_params=_cparams(("parallel",)),
        name="proj_q",
    )(h, w, *tabs)


def _proj_kv_kernel(a_ref, w_ref, cos_ref, sa_ref, sb_ref, kvc_ref, ks_ref, kw_ref, vst_ref, vwt_ref):
    acc = jnp.dot(a_ref[...], w_ref[...], preferred_element_type=F32)
    cos_e, sin_a, sin_b = cos_ref[...], sa_ref[...], sb_ref[...]
    gw = KV_GROUPS * HEAD_DIM
    for n in range(2 * KV_GROUPS):
        kvc_ref[0, n] = acc[:, n * HEAD_DIM:(n + 1) * HEAD_DIM]
    low_lanes = lax.broadcasted_iota(jnp.int32, (acc.shape[0], LANES), 1) < HEAD_DIM
    for base, k_ref, vt_ref in ((2 * gw, ks_ref, vst_ref), (4 * gw, kw_ref, vwt_ref)):
        vtile = vt_ref.shape[-1]
        for c in range(gw // LANES):
            kk = _rope128(acc[:, base + c * LANES: base + (c + 1) * LANES], cos_e, sin_a, sin_b)
            vv = acc[:, base + gw + c * LANES: base + gw + (c + 1) * LANES].T
            for half in range(2):
                g = 2 * c + half
                k_lo = kk if half == 0 else pltpu.roll(kk, HEAD_DIM, 1)
                k_ref[0, g] = jnp.where(low_lanes, k_lo, 0.0).astype(k_ref.dtype)
                for j in range(ROW_TILE // vtile):
                    vt_ref[0, g, j] = vv[half * HEAD_DIM:(half + 1) * HEAD_DIM,
                                         j * vtile:(j + 1) * vtile].astype(vt_ref.dtype)


def _proj_kv(h, w, tabs, bsz, s):
    t, k = h.shape
    n = w.shape[1]
    spb = s // ROW_TILE
    tab_spec = pl.BlockSpec((ROW_TILE, LANES), lambda i: (i, 0))
    k_spec = pl.BlockSpec((1, KV_GROUPS, ROW_TILE, LANES), lambda i: (i // spb, 0, i % spb, 0))
    k_shape = jax.ShapeDtypeStruct((bsz, KV_GROUPS, s, LANES), BF16)

    def vt(tile):
        return (pl.BlockSpec((1, KV_GROUPS, ROW_TILE // tile, HEAD_DIM, tile), lambda i: (i // spb, 0, i % spb, 0, 0)),
                jax.ShapeDtypeStruct((bsz, KV_GROUPS, s // tile, HEAD_DIM, tile), BF16))

    (vs_spec, vs_shape), (vw_spec, vw_shape) = vt(KEY_STEP), vt(Q_BLOCK)
    return pl.pallas_call(
        _proj_kv_kernel,
        grid=(t // ROW_TILE,),
        in_specs=[pl.BlockSpec((ROW_TILE, k), lambda i: (i, 0)),
                  pl.BlockSpec((k, n), lambda i: (0, 0)),
                  tab_spec, tab_spec, tab_spec],
        out_specs=[pl.BlockSpec((1, 2 * KV_GROUPS, ROW_TILE, HEAD_DIM), lambda i: (i // spb, 0, i % spb, 0)),
                   k_spec, k_spec, vs_spec, vw_spec],
        out_shape=[jax.ShapeDtypeStruct((bsz, 2 * KV_GROUPS, s, HEAD_DIM), F32),
                   k_shape, k_shape, vs_shape, vw_shape],
        compiler_params=_cparams(("parallel",)),
        name="proj_kv",
    )(h, w, *tabs)


def _ssd_kernel(z_ref, xs_ref, bc_ref, misc_ref, convw_ref, convb_ref, dtb_ref, alog_ref,
                dskip_ref, gnorm_ref, selb_ref, sele_ref, wout_ref, o_ref,
                ext_ref, act_ref, state_ref):
    c = pl.program_id(1)
    L = CHUNK
    gw = SSM_HPG * SSM_HEAD_DIM

    @pl.when(c == 0)
    def _():
        state_ref[...] = jnp.zeros_like(state_ref)
        ext_ref[0:8, :] = jnp.zeros((8, SSM_XBC), F32)

    ext_ref[8:8 + L, 0:D_INNER] = xs_ref[...]
    ext_ref[8:8 + L, D_INNER:SSM_XBC] = bc_ref[...]
    cw = 512
    for cc in range(SSM_XBC // cw):
        cols = slice(cc * cw, (cc + 1) * cw)
        acc = convb_ref[:, cols] + convw_ref[0:1, cols] * ext_ref[5:5 + L, cols]
        for k in range(1, SSM_CONV):
            acc = acc + convw_ref[k:k + 1, cols] * ext_ref[5 + k:5 + k + L, cols]
        act_ref[:, cols] = acc * _sigmoid(acc)
    ext_ref[0:8, :] = ext_ref[L:L + 8, :]

    lane = lax.broadcasted_iota(jnp.int32, (L, LANES), 1)
    row = lax.broadcasted_iota(jnp.int32, (L, LANES), 0)
    head_lane = lane < SSM_HEADS
    raw = misc_ref[...] + dtb_ref[...]
    dt = jnp.where(head_lane, jnp.maximum(raw, 0.0) + jnp.log1p(jnp.exp(-jnp.abs(raw))), 0.0)
    a_row = jnp.where(head_lane[0:1], -jnp.exp(alog_ref[...]), 0.0)
    cs = dt * a_row
    sh = 1
    while sh < L:
        cs = cs + jnp.where(row >= sh, pltpu.roll(cs, sh, 0), 0.0)
        sh *= 2
    tot = cs[L - 1:L, :]
    w_state = dt * jnp.exp(tot - cs)
    cs_t, dt_t, w_t = cs.T, dt.T, w_state.T

    hi, mid, lo = _split3(cs)
    packed = (hi.astype(F32) + pltpu.roll(mid.astype(F32), SSM_HEADS, 1)
              + pltpu.roll(lo.astype(F32), 2 * SSM_HEADS, 1)).astype(BF16)
    col_b = jnp.dot(packed, selb_ref[...], preferred_element_type=F32)
    col_e = jnp.dot(packed, sele_ref[...], preferred_element_type=F32)
    exp_e = jnp.exp(col_e)
    etot_e = exp_e[L - 1:L, :]

    tri = lax.broadcasted_iota(jnp.int32, (L, L), 0) >= lax.broadcasted_iota(jnp.int32, (L, L), 1)
    brow = lax.broadcasted_iota(jnp.int32, (SSM_HPG * L, gw), 0) // L
    bcol = lax.broadcasted_iota(jnp.int32, (SSM_HPG * L, gw), 1) // SSM_HEAD_DIM
    blockmask = brow == bcol

    for g in range(SSM_GROUPS):
        xs_g = act_ref[:, g * gw:(g + 1) * gw]
        b_g = act_ref[:, D_INNER + g * SSM_STATE: D_INNER + (g + 1) * SSM_STATE]
        c_g = act_ref[:, D_INNER + SSM_GROUPS * SSM_STATE + g * SSM_STATE:
                      D_INNER + SSM_GROUPS * SSM_STATE + (g + 1) * SSM_STATE]
        c_bf = c_g.astype(BF16)
        cb = lax.dot_general(c_bf, b_g.astype(BF16), (((1,), (1,)), ((), ())),
                             preferred_element_type=F32)
        b_t = b_g.T
        tops, bots = [], []
        for j in range(SSM_HPG):
            h = g * SSM_HPG + j
            decay = jnp.exp(jnp.where(tri, col_b[:, h * L:(h + 1) * L] - cs_t[h:h + 1, :], NEG))
            tops.append((cb * decay * dt_t[h:h + 1, :]).astype(BF16))
            bots.append((b_t * w_t[h:h + 1, :]).astype(BF16))
        lhs = jnp.concatenate([jnp.concatenate(tops, axis=1), jnp.concatenate(bots, axis=1)], axis=0)
        xs_bf = xs_g.astype(BF16)
        xs_bd = jnp.where(blockmask, jnp.concatenate([xs_bf] * SSM_HPG, axis=0), jnp.zeros((), BF16))
        res = jnp.dot(lhs, xs_bd, preferred_element_type=F32)
        st = state_ref[g]
        y_off = jnp.dot(c_bf, st.astype(BF16), preferred_element_type=F32) * exp_e[:, g * gw:(g + 1) * gw]
        state_ref[g] = st * etot_e[:, g * gw:(g + 1) * gw] + res[L:2 * L]
        y = res[0:L] + y_off + dskip_ref[:, g * gw:(g + 1) * gw] * xs_g
        zg = z_ref[:, g * gw:(g + 1) * gw]
        y = y * (zg * _sigmoid(zg))
        ms = jnp.mean(y * y, axis=-1, keepdims=True)
        act_ref[:, g * gw:(g + 1) * gw] = y * lax.rsqrt(ms + NORM_EPS) * gnorm_ref[:, g * gw:(g + 1) * gw]

    o_ref[...] = jnp.dot(act_ref[:, 0:D_INNER].astype(BF16), wout_ref[...], preferred_element_type=F32)


def _ssd(raw, conv_w, conv_b, dt_bias, a_log, d_skip, g_ssm_norm, w_ssm, bsz, s):
    t = raw.shape[0]
    nc = s // CHUNK
    blk = lambda cb: pl.BlockSpec((CHUNK, 2048), lambda b, c, cb=cb: (b * nc + c, cb))
    const = lambda shape: pl.BlockSpec(shape, lambda b, c: (0,) * len(shape))
    k = jnp.arange(LANES)
    selb = ((k[:, None] < 3 * SSM_HEADS) & ((k[:, None] % SSM_HEADS) == (jnp.arange(SSM_HEADS * LANES)[None, :] // LANES)))
    sele = ((k[:, None] < 3 * SSM_HEADS) & ((k[:, None] % SSM_HEADS) == (jnp.arange(D_INNER)[None, :] // SSM_HEAD_DIM)))
    pad = lambda v: jnp.pad(v.astype(F32), (0, LANES - SSM_HEADS)).reshape(1, LANES)
    return pl.pallas_call(
        _ssd_kernel,
        grid=(bsz, nc),
        in_specs=[blk(RAW_Z // 2048), blk(RAW_XS // 2048), blk(RAW_BC // 2048),
                  pl.BlockSpec((CHUNK, LANES), lambda b, c: (b * nc + c, RAW_MISC // LANES)),
                  const((SSM_CONV, SSM_XBC)), const((1, SSM_XBC)), const((1, LANES)), const((1, LANES)),
                  const((1, D_INNER)), const((1, D_INNER)),
                  const((LANES, SSM_HEADS * LANES)), const((LANES, D_INNER)),
                  const((D_INNER, D_MODEL))],
        out_specs=pl.BlockSpec((CHUNK, D_MODEL), lambda b, c: (b * nc + c, 0)),
        out_shape=jax.ShapeDtypeStruct((t, D_MODEL), F32),
        scratch_shapes=[pltpu.VMEM((CHUNK + 8, SSM_XBC), F32),
                        pltpu.VMEM((CHUNK, SSM_XBC), F32),
                        pltpu.VMEM((SSM_GROUPS, SSM_STATE, SSM_HPG * SSM_HEAD_DIM), F32)],
        compiler_params=_cparams(("parallel", "arbitrary")),
        name="ssd_mixer",
    )(raw, raw, raw, raw, conv_w.astype(F32), conv_b.reshape(1, SSM_XBC).astype(F32),
      pad(dt_bias), pad(a_log), jnp.repeat(d_skip.astype(F32), SSM_HEAD_DIM).reshape(1, D_INNER),
      g_ssm_norm.reshape(1, D_INNER).astype(F32), selb.astype(BF16), sele.astype(BF16), w_ssm.astype(BF16))


def _compress_kernel(u_ref, pelo_ref, pehi_ref, w1_ref, w2_ref, o_ref):
    u = u_ref[0]
    half = CMP_STRIDE * HEAD_DIM
    a = jnp.dot((u + pelo_ref[0, 0]).astype(BF16), w1_ref[0, 0:half, :], preferred_element_type=F32)
    b = jnp.dot((u + pehi_ref[0, 0]).astype(BF16), w1_ref[0, half:2 * half, :], preferred_element_type=F32)
    pre = a + pltpu.roll(b, u.shape[0] - 1, 0)
    hidden = pre * _sigmoid(pre)
    o_ref[0] = jnp.dot(hidden.astype(BF16), w2_ref[0], preferred_element_type=F32).astype(o_ref.dtype)


def _compress(u, pe, w1, w2):
    n, rows, width = u.shape
    kv_of = lambda i: (i // KV_GROUPS) % 2
    return pl.pallas_call(
        _compress_kernel,
        grid=(n,),
        in_specs=[pl.BlockSpec((1, rows, width), lambda i: (i, 0, 0)),
                  pl.BlockSpec((1, 1, 1, width), lambda i: (kv_of(i), 0, 0, 0)),
                  pl.BlockSpec((1, 1, 1, width), lambda i: (kv_of(i), 1, 0, 0)),
                  pl.BlockSpec((1, 2 * width, CMP_HIDDEN), lambda i: (kv_of(i), 0, 0)),
                  pl.BlockSpec((1, CMP_HIDDEN, HEAD_DIM), lambda i: (kv_of(i), 0, 0))],
        out_specs=pl.BlockSpec((1, rows, HEAD_DIM), lambda i: (i, 0, 0)),
        out_shape=jax.ShapeDtypeStruct((n, rows, HEAD_DIM), BF16),
        compiler_params=_cparams(("parallel",)),
        name="compress",
    )(u, pe, pe, w1, w2)


def _online_update(carry, s_m, vt):
    m, l, acc = carry
    m_new = jnp.maximum(m, jnp.max(s_m, axis=0, keepdims=True))
    alpha = jnp.exp2(m - m_new)
    p = jnp.exp2(s_m - m_new)
    l = l * alpha + jnp.sum(p, axis=0, keepdims=True)
    acc = acc * alpha + jnp.dot(vt, p.astype(BF16), preferred_element_type=F32)
    return m_new, l, acc


def _nsa_kernel(qn_ref, qr_ref, kc_ref, vct_ref, ks_ref, vst_ref, kw_ref, vwt_ref, gate_ref, ovt_ref, oh_ref,
                o_ref, gt_ref, qa_ref, sa_ref, sb_ref, pa_ref, pb_ref, *, topk):
    g = pl.program_id(1)
    qb = pl.program_id(2)
    nq = ATT_HPG * Q_BLOCK
    ncmp = kc_ref.shape[1]
    nsel = ovt_ref.shape[0]
    q0 = qb * Q_BLOCK
    tq = q0 + lax.broadcasted_iota(jnp.int32, (1, nq), 1) % Q_BLOCK

    def heads_on_lanes(ref):
        return jnp.concatenate([ref[0, r * HEAD_DIM:(r + 1) * HEAD_DIM, :] for r in range(ATT_HPG)], axis=1)

    qn_t = heads_on_lanes(qn_ref)
    qr_t = heads_on_lanes(qr_ref)

    s_c = jnp.dot(kc_ref[0], qn_t, preferred_element_type=F32)
    cmp_end = lax.broadcasted_iota(jnp.int32, (ncmp, nq), 0) * CMP_STRIDE + (CMP_LEN - 1)
    mask_c = cmp_end <= tq
    s_cm = jnp.where(mask_c, s_c, NEG)
    m_c = jnp.max(s_cm, axis=0, keepdims=True)
    p_c = jnp.where(mask_c, jnp.exp2(s_cm - m_c), 0.0)
    l_c = jnp.sum(p_c, axis=0, keepdims=True)
    p_c = p_c * jnp.where(l_c > 0.0, 1.0 / l_c, 0.0)
    o_c = jnp.dot(vct_ref[0], p_c.astype(BF16), preferred_element_type=F32)

    p_sum = p_c[:, 0:Q_BLOCK]
    for r in range(1, ATT_HPG):
        p_sum = p_sum + p_c[:, r * Q_BLOCK:(r + 1) * Q_BLOCK]
    imp = jnp.dot(ovt_ref[...], jnp.concatenate(_split3(p_sum), axis=0), preferred_element_type=F32)

    blk = lax.broadcasted_iota(jnp.int32, (nsel, Q_BLOCK), 0)
    cur = (q0 + lax.broadcasted_iota(jnp.int32, (nsel, Q_BLOCK), 1)) // SEL_BLOCK
    valid = blk <= cur
    forced = valid & ((blk == 0) | (blk == cur) | (blk == cur - 1))
    val = jnp.where(forced, BIG, jnp.where(valid, imp, NEG))
    chosen = jnp.zeros((nsel, Q_BLOCK), F32)
    blk_f = blk.astype(F32)
    for _ in range(topk):
        mx = jnp.max(val, axis=0, keepdims=True)
        first = jnp.min(jnp.where(val == mx, blk_f, float(nsel)), axis=0, keepdims=True)
        hit = blk_f == first
        chosen = jnp.where(hit, 1.0, chosen)
        val = jnp.where(hit, REMOVED, val)

    bias = jnp.concatenate([(chosen - 1.0) * BIG] * ATT_HPG, axis=1)
    pad_rows = [jnp.zeros((LANES - nsel, nq), F32)] if nsel < LANES else []
    q_aug = jnp.concatenate([qr_t, jnp.zeros((LANES - HEAD_DIM, nq), BF16),
                             jnp.concatenate([bias] + pad_rows, axis=0).astype(BF16)], axis=0)
    q_pad = q_aug[0:LANES]
    qa_ref[...] = q_aug

    def sel_scores(i):
        k0 = pl.multiple_of(i * KEY_STEP, KEY_STEP)
        lhs = jnp.concatenate([ks_ref[0, pl.ds(k0, KEY_STEP), :], oh_ref[pl.ds(k0, KEY_STEP), :]], axis=1)
        return jnp.dot(lhs, qa_ref[...], preferred_element_type=F32)

    row_k = lax.broadcasted_iota(jnp.int32, (KEY_STEP, nq), 0)

    def softmax_step(s_ref, p_ref, m, l, first_key):
        s = s_ref[...]
        if first_key is not None:
            s = jnp.where(row_k <= tq - first_key, s, NEG)
        m_new = jnp.maximum(m, jnp.max(s, axis=0, keepdims=True))
        alpha = jnp.exp2(m - m_new)
        p = jnp.exp2(s - m_new)
        p_ref[...] = p.astype(BF16)
        return m_new, l * alpha + jnp.sum(p, axis=0, keepdims=True), alpha

    def pv(acc, alpha, p_ref, i):
        return acc * alpha + jnp.dot(vst_ref[0, i], p_ref[...], preferred_element_type=F32)

    def sel_pair(j, carry, final):
        m, l, acc, alpha_b = carry
        a = 2 * j
        sb_ref[...] = sel_scores(a + 1)
        acc = pv(acc, alpha_b, pb_ref, jnp.maximum(a - 1, 0))
        m, l, alpha_a = softmax_step(sa_ref, pa_ref, m, l, a * KEY_STEP if final else None)
        if not final:
            sa_ref[...] = sel_scores(a + 2)
        acc = pv(acc, alpha_a, pa_ref, a)
        m, l, alpha_b = softmax_step(sb_ref, pb_ref, m, l, (a + 1) * KEY_STEP if final else None)
        return m, l, acc, alpha_b

    n_pairs = (q0 + Q_BLOCK - 1) // (2 * KEY_STEP) + 1
    sa_ref[...] = sel_scores(0)
    pb_ref[...] = jnp.zeros_like(pb_ref)
    init = (jnp.full((1, nq), NEG, F32), jnp.zeros((1, nq), F32), jnp.zeros((HEAD_DIM, nq), F32),
            jnp.ones((1, nq), F32))
    carry = lax.fori_loop(0, n_pairs - 1, lambda j, c: sel_pair(j, c, False), init)
    _, l_s, acc_s, alpha_b = sel_pair(n_pairs - 1, carry, True)
    acc_s = pv(acc_s, alpha_b, pb_ref, 2 * n_pairs - 1)

    row = lax.broadcasted_iota(jnp.int32, (Q_BLOCK, nq), 0)
    n_wt = WINDOW // Q_BLOCK + 1
    s_w, v_w = [], []
    for i in range(n_wt):
        kt = qb - (n_wt - 1) + i
        k0 = pl.multiple_of(jnp.maximum(kt, 0) * Q_BLOCK, Q_BLOCK)
        s = jnp.dot(kw_ref[0, pl.ds(k0, Q_BLOCK), :], q_pad, preferred_element_type=F32)
        if i == 0:
            s = jnp.where((k0 + row > tq - WINDOW) & (kt >= 0), s, NEG)
        elif i == n_wt - 1:
            s = jnp.where(k0 + row <= tq, s, NEG)
        else:
            s = jnp.where(kt >= 0, s, NEG)
        s_w.append(s)
        v_w.append(vwt_ref[0, jnp.maximum(kt, 0)])
    s_w = jnp.concatenate(s_w, axis=0)
    p_w = jnp.exp2(s_w - jnp.max(s_w, axis=0, keepdims=True))
    l_w = jnp.sum(p_w, axis=0, keepdims=True)
    acc_w = jnp.dot(jnp.concatenate(v_w, axis=1), p_w.astype(BF16), preferred_element_type=F32)

    gt_ref[...] = _sigmoid(gate_ref[...]).T

    def gate_row(branch):
        base = MISC_GATE0 + branch * ATT_HEADS + g * ATT_HPG
        return jnp.concatenate([gt_ref[pl.ds(base + r, 1), :] for r in range(ATT_HPG)], axis=1)

    o = gate_row(0) * o_c + (gate_row(1) / l_s) * acc_s + (gate_row(2) / l_w) * acc_w
    for r in range(ATT_HPG):
        o_ref[0, r * HEAD_DIM:(r + 1) * HEAD_DIM, :] = o[:, r * Q_BLOCK:(r + 1) * Q_BLOCK].astype(o_ref.dtype)


def _nsa(qn_t, qr_t, kc, vc_t, ks, vs_t, kw, vw_t, raw, bsz, s):
    nqb = s // Q_BLOCK
    ncmp = s // CMP_STRIDE
    nsel = s // SEL_BLOCK
    topk = min(SEL_TOPK, nsel)
    gw = ATT_HPG * HEAD_DIM
    ci = jnp.arange(ncmp)[None, :]
    sj = jnp.arange(nsel)[:, None]
    ov_t = (ci * CMP_STRIDE < (sj + 1) * SEL_BLOCK) & (ci * CMP_STRIDE + CMP_LEN > sj * SEL_BLOCK) \
        & (ci < ncmp - 1)
    ov_t3 = jnp.concatenate([ov_t.astype(BF16)] * 3, axis=1)
    assert nsel <= LANES
    onehot = (jnp.arange(s)[:, None] // SEL_BLOCK == jnp.arange(LANES)[None, :]).astype(BF16)
    q_spec = pl.BlockSpec((1, gw, Q_BLOCK), lambda b, g, i: (b, g, i))
    per_bg = lambda shape: pl.BlockSpec((1,) + shape, lambda b, g, i: (b * KV_GROUPS + g,) + (0,) * len(shape))
    return pl.pallas_call(
        functools.partial(_nsa_kernel, topk=topk),
        grid=(bsz, KV_GROUPS, nqb),
        in_specs=[q_spec, q_spec,
                  per_bg((ncmp, HEAD_DIM)), per_bg((HEAD_DIM, ncmp)),
                  per_bg((s, LANES)), per_bg((s // KEY_STEP, HEAD_DIM, KEY_STEP)),
                  per_bg((s, LANES)), per_bg((s // Q_BLOCK, HEAD_DIM, Q_BLOCK)),
                  pl.BlockSpec((Q_BLOCK, LANES), lambda b, g, i: (b * nqb + i, RAW_MISC // LANES)),
                  pl.BlockSpec((nsel, 3 * ncmp), lambda b, g, i: (0, 0)),
                  pl.BlockSpec((s, LANES), lambda b, g, i: (0, 0))],
        out_specs=pl.BlockSpec((1, gw, Q_BLOCK), lambda b, g, i: (b, g, i)),
        out_shape=jax.ShapeDtypeStruct((bsz, ATT_HEADS * HEAD_DIM, s), BF16),
        scratch_shapes=[pltpu.VMEM((LANES, Q_BLOCK), F32),
                        pltpu.VMEM((2 * LANES, ATT_HPG * Q_BLOCK), BF16),
                        pltpu.VMEM((KEY_STEP, ATT_HPG * Q_BLOCK), F32),
                        pltpu.VMEM((KEY_STEP, ATT_HPG * Q_BLOCK), F32),
                        pltpu.VMEM((KEY_STEP, ATT_HPG * Q_BLOCK), BF16),
                        pltpu.VMEM((KEY_STEP, ATT_HPG * Q_BLOCK), BF16)],
        compiler_params=_cparams(("parallel", "parallel", "arbitrary")),
        name="nsa_sweep",
    )(qn_t, qr_t, kc, vc_t, ks, vs_t, kw, vw_t, raw, ov_t3, onehot)


def _merge_kernel(ot_ref, yssm_ref, mix_ref, x_ref, wab_ref, wo_ref, o_ref):
    o = ot_ref[0].astype(F32).T.astype(BF16)
    y_att = jnp.dot(o, wab_ref[...], preferred_element_type=F32)
    gates = _sigmoid(mix_ref[...])
    mixed = gates[:, 0:D_MODEL] * yssm_ref[...] + gates[:, D_MODEL:2 * D_MODEL] * y_att
    o_ref[...] = x_ref[...] + jnp.dot(mixed.astype(BF16), wo_ref[...], preferred_element_type=F32)


def _merge(o_t, y_ssm, raw, x2d, w_ab, w_o, s):
    t = x2d.shape[0]
    spb = s // ROW_TILE
    row_spec = pl.BlockSpec((ROW_TILE, D_MODEL), lambda i: (i, 0))
    w_spec = pl.BlockSpec((D_MODEL, D_MODEL), lambda i: (0, 0))
    return pl.pallas_call(
        _merge_kernel,
        grid=(t // ROW_TILE,),
        in_specs=[pl.BlockSpec((1, D_MODEL, ROW_TILE), lambda i: (i // spb, 0, i % spb)),
                  row_spec,
                  pl.BlockSpec((ROW_TILE, 2 * D_MODEL), lambda i: (i, RAW_MIX // (2 * D_MODEL))),
                  row_spec, w_spec, w_spec],
        out_specs=row_spec,
        out_shape=jax.ShapeDtypeStruct((t, D_MODEL), F32),
        compiler_params=_cparams(("parallel",)),
        name="merge",
    )(o_t, y_ssm, raw, x2d, w_ab, w_o)


def _mlp_kernel(x_ref, g1_ref, wup_ref, wdn_ref, g2_ref, o_ref):
    x = x_ref[...]
    ms = jnp.mean(x * x, axis=-1, keepdims=True)
    h = (x * lax.rsqrt(ms + NORM_EPS) * g1_ref[...]).astype(BF16)
    up = jnp.maximum(jnp.dot(h, wup_ref[...], preferred_element_type=F32), 0.0)
    y = x + jnp.dot((up * up).astype(BF16), wdn_ref[...], preferred_element_type=F32)
    ms2 = jnp.mean(y * y, axis=-1, keepdims=True)
    o_ref[...] = y * lax.rsqrt(ms2 + NORM_EPS) * g2_ref[...]


def _mlp(x1, g1, w_up, w_down, g2):
    t = x1.shape[0]
    tm = ROW_TILE // 2
    row_spec = pl.BlockSpec((tm, D_MODEL), lambda i: (i, 0))
    g_spec = pl.BlockSpec((1, D_MODEL), lambda i: (0, 0))
    return pl.pallas_call(
        _mlp_kernel,
        grid=(t // tm,),
        in_specs=[row_spec, g_spec,
                  pl.BlockSpec((D_MODEL, MLP_HIDDEN), lambda i: (0, 0)),
                  pl.BlockSpec((MLP_HIDDEN, D_MODEL), lambda i: (0, 0)),
                  g_spec],
        out_specs=row_spec,
        out_shape=jax.ShapeDtypeStruct((t, D_MODEL), F32),
        compiler_params=_cparams(("parallel",)),
        name="mlp",
    )(x1, g1.reshape(1, D_MODEL), w_up, w_down, g2.reshape(1, D_MODEL))


def _rope_tables(positions):
    half = ROPE_DIM // 2
    inv_freq = ROPE_THETA ** (-jnp.arange(0, ROPE_DIM, 2, dtype=F32) / ROPE_DIM)
    ang = positions.astype(F32).reshape(-1, 1) * inv_freq[None, :]
    cos, sin = jnp.cos(ang), jnp.sin(ang)
    t = cos.shape[0]
    rest = HEAD_DIM - ROPE_DIM
    cos_h = jnp.concatenate([cos, cos, jnp.ones((t, rest), F32)], axis=1)
    sa_h = jnp.concatenate([-sin, jnp.zeros((t, half + rest), F32)], axis=1)
    sb_h = jnp.concatenate([jnp.zeros((t, half), F32), sin, jnp.zeros((t, rest), F32)], axis=1)
    two = lambda a: jnp.concatenate([a, a], axis=1)
    return two(cos_h), two(sa_h), two(sb_h)


def kernel(x, positions, g_norm_mix, w_in, conv_w, conv_b, dt_bias, a_log, d_skip, g_ssm_norm, w_ssm_branch, cmp_pe_k, cmp_pe_v, w_cmp_k1, w_cmp_k2, w_cmp_v1, w_cmp_v2, w_attn_branch, w_o, g_norm_mlp, w_up, w_down, g_norm_final):
    bsz, s, d = x.shape
    assert d == D_MODEL and s % ROW_TILE == 0 and w_in.shape[0] == 1
    t = bsz * s
    x2d = x.reshape(t, d)
    tabs = _rope_tables(positions)

    w = w_in[0]
    o_z, o_xbc, o_dt, o_q, o_kv, o_ag, o_mg = 0, 2048, 6144, 6176, 7200, 8736, 8784
    misc_pad = jnp.zeros((d, RAW_COLS - RAW_MISC - SSM_HEADS - 3 * ATT_HEADS), w.dtype)
    w_raw = jnp.concatenate([w[:, o_z:o_dt], w[:, o_mg:], w[:, o_dt:o_q], w[:, o_ag:o_mg], misc_pad],
                            axis=1).astype(BF16)
    w_q = w[:, o_q:o_kv].astype(BF16)
    w_kv = w[:, o_kv:o_ag].astype(BF16)

    h = _rmsnorm(x2d, g_norm_mix[0])
    raw = _proj_raw(h, w_raw)
    qn_t, qr_t = _proj_q(h, w_q, tabs, bsz, s)
    kvc, ks, kw, vs_t, vw_t = _proj_kv(h, w_kv, tabs, bsz, s)

    y_ssm = _ssd(raw, conv_w[0], conv_b[0], dt_bias[0], a_log[0], d_skip[0], g_ssm_norm[0],
                 w_ssm_branch[0], bsz, s)

    rows = s // CMP_STRIDE
    u = kvc.reshape(bsz * 2 * KV_GROUPS, rows, CMP_STRIDE * HEAD_DIM)
    pe = jnp.stack([cmp_pe_k[0], cmp_pe_v[0]]).astype(F32).reshape(2, 2, 1, CMP_STRIDE * HEAD_DIM)
    w1 = jnp.stack([w_cmp_k1[0], w_cmp_v1[0]]).astype(BF16)
    w2 = jnp.stack([w_cmp_k2[0], w_cmp_v2[0]]).astype(BF16)
    cmp = _compress(u, pe, w1, w2).reshape(bsz, 2, KV_GROUPS, rows, HEAD_DIM)
    kc = cmp[:, 0].reshape(bsz * KV_GROUPS, rows, HEAD_DIM)
    vc_t = jnp.swapaxes(cmp[:, 1], -1, -2).reshape(bsz * KV_GROUPS, HEAD_DIM, rows)

    flat = lambda a: a.reshape((bsz * KV_GROUPS,) + a.shape[2:])
    o_t = _nsa(qn_t, qr_t, kc, vc_t, flat(ks), flat(vs_t), flat(kw), flat(vw_t), raw, bsz, s)

    x1 = _merge(o_t, y_ssm, raw, x2d, w_attn_branch[0].astype(BF16), w_o[0].astype(BF16), s)
    out = _mlp(x1, g_norm_mlp[0], w_up[0].astype(BF16), w_down[0].astype(BF16), g_norm_final)
    return out.reshape(bsz, s, d)
```

```python
import functools

import jax
import jax.numpy as jnp
from jax import lax
from jax.experimental import pallas as pl
from jax.experimental.pallas import tpu as pltpu

F32 = jnp.float32
BF16 = jnp.bfloat16

D_MODEL = 1024
D_INNER = 2048
SSM_HEADS = 32
SSM_GROUPS = 8
SSM_HPG = 4
SSM_HEAD_DIM = 64
SSM_STATE = 128
SSM_CONV = 4
CHUNK = 128
SSM_XBC = 4096
HEAD_DIM = 64
ATT_HEADS = 16
KV_GROUPS = 4
ATT_HPG = 4
CMP_LEN = 32
CMP_STRIDE = 16
CMP_HIDDEN = 256
SEL_BLOCK = 64
SEL_TOPK = 16
WINDOW = 512
Q_BLOCK = 128
ROPE_THETA = 500000.0
ROPE_DIM = 16
MLP_HIDDEN = 4096
NORM_EPS = 1e-6
NEG = -1e30
BIG = 1e30
REMOVED = -3e38
LOG2E = 1.4426950408889634
SCALE = HEAD_DIM ** -0.5 * LOG2E
KEY_STEP = 256
NSA_GROUPS = 2

LANES = 128
ROW_TILE = 512
VMEM_LIMIT = 56 * 1024 * 1024

RAW_Z, RAW_XS, RAW_BC = 0, 2048, 4096
SSM_COLS, SSM_TN = 6144, 1536
GATE_MIX, GATE_MISC, GATE_COLS = 0, 2048, 2176
MISC_GATE0 = SSM_HEADS
V_ROWS = HEAD_DIM + 16


def _cparams(sem):
    return pltpu.CompilerParams(dimension_semantics=sem, vmem_limit_bytes=VMEM_LIMIT)


def _sigmoid(x):
    return 0.5 * jnp.tanh(0.5 * x) + 0.5


def _silu(x):
    h = 0.5 * x
    return h + h * jnp.tanh(h)


def _split3(x):
    hi = x.astype(BF16)
    r1 = x - hi.astype(F32)
    mid = r1.astype(BF16)
    lo = (r1 - mid.astype(F32)).astype(BF16)
    return hi, mid, lo


def _rmsnorm_kernel(x_ref, g_ref, o_ref):
    x = x_ref[...]
    ms = jnp.mean(x * x, axis=-1, keepdims=True)
    o_ref[...] = (x * lax.rsqrt(ms + NORM_EPS) * g_ref[...]).astype(o_ref.dtype)


def _rmsnorm(x2d, g):
    t, d = x2d.shape
    return pl.pallas_call(
        _rmsnorm_kernel,
        grid=(t // ROW_TILE,),
        in_specs=[pl.BlockSpec((ROW_TILE, d), lambda i: (i, 0)),
                  pl.BlockSpec((1, d), lambda i: (0, 0))],
        out_specs=pl.BlockSpec((ROW_TILE, d), lambda i: (i, 0)),
        out_shape=jax.ShapeDtypeStruct((t, d), BF16),
        compiler_params=_cparams(("parallel",)),
        name="rmsnorm",
    )(x2d, g.reshape(1, d))


def _proj_raw_kernel(a_ref, w_ref, o_ref):
    o_ref[...] = jnp.dot(a_ref[...], w_ref[...], preferred_element_type=F32)


def _proj_raw(h, w, n, tn, name):
    t, k = h.shape
    return pl.pallas_call(
        _proj_raw_kernel,
        grid=(n // tn, t // ROW_TILE),
        in_specs=[pl.BlockSpec((ROW_TILE, k), lambda j, i: (i, 0)),
                  pl.BlockSpec((k, tn), lambda j, i: (0, j))],
        out_specs=pl.BlockSpec((ROW_TILE, tn), lambda j, i: (i, j)),
        out_shape=jax.ShapeDtypeStruct((t, n), F32),
        compiler_params=_cparams(("parallel", "parallel")),
        name=name,
    )(h, w)


def _rope_t(t_t, cos_t, sin_t):
    half = ROPE_DIM // 2
    rows = []
    for base in range(0, t_t.shape[0], HEAD_DIM):
        t1, t2 = t_t[base:base + half], t_t[base + half:base + ROPE_DIM]
        rows += [t1 * cos_t - t2 * sin_t, t2 * cos_t + t1 * sin_t, t_t[base + ROPE_DIM:base + HEAD_DIM]]
    return jnp.concatenate(rows, axis=0)


def _proj_q_kernel(a_ref, w_ref, cos_ref, sin_ref, qn_ref, qr_ref):
    acc = jnp.dot(a_ref[...], w_ref[...], preferred_element_type=F32)
    cos_t, sin_t = cos_ref[...], sin_ref[...]
    for c in range(acc.shape[1] // LANES):
        t_t = acc[:, c * LANES:(c + 1) * LANES].T
        qn_ref[0, c * LANES:(c + 1) * LANES, :] = (t_t * SCALE).astype(qn_ref.dtype)
        qr_ref[0, c * LANES:(c + 1) * LANES, :] = (_rope_t(t_t, cos_t, sin_t) * SCALE).astype(qr_ref.dtype)


def _proj_q(h, w, tabs, bsz, s):
    t, k = h.shape
    n = w.shape[1]
    spb = s // ROW_TILE
    tab_spec = pl.BlockSpec((ROPE_DIM // 2, ROW_TILE), lambda i: (0, i))
    out_spec = pl.BlockSpec((1, n, ROW_TILE), lambda i: (i // spb, 0, i % spb))
    return pl.pallas_call(
        _proj_q_kernel,
        grid=(t // ROW_TILE,),
        in_specs=[pl.BlockSpec((ROW_TILE, k), lambda i: (i, 0)),
                  pl.BlockSpec((k, n), lambda i: (0, 0)),
                  tab_spec, tab_spec],
        out_specs=[out_spec, out_spec],
        out_shape=[jax.ShapeDtypeStruct((bsz, n, s), BF16)] * 2,
        compiler_params=_cparams(("parallel",)),
        name="proj_q",
    )(h, w, *tabs)


def _proj_kv_kernel(a_ref, w_ref, cos_ref, sin_ref, kvc_ref, ks_ref, kw_ref, vst_ref, vwt_ref):
    acc = jnp.dot(a_ref[...], w_ref[...], preferred_element_type=F32)
    cos_t, sin_t = cos_ref[...], sin_ref[...]
    tm = acc.shape[0]
    gw = KV_GROUPS * HEAD_DIM
    for n in range(2 * KV_GROUPS):
        kvc_ref[0, n] = acc[:, n * HEAD_DIM:(n + 1) * HEAD_DIM]
    lane = lax.broadcasted_iota(jnp.int32, (tm, LANES), 1)
    key_pad = jnp.where(lane == HEAD_DIM, 1.0, 0.0)
    ones_rows = jnp.where(lax.broadcasted_iota(jnp.int32, (V_ROWS - HEAD_DIM, tm), 0) == 0, 1.0, 0.0)
    for base, k_ref, vt_ref in ((2 * gw, ks_ref, vst_ref), (4 * gw, kw_ref, vwt_ref)):
        vtile = vt_ref.shape[-1]
        for c in range(gw // LANES):
            kk = _rope_t(acc[:, base + c * LANES: base + (c + 1) * LANES].T, cos_t, sin_t).T
            vv = acc[:, base + gw + c * LANES: base + gw + (c + 1) * LANES].T
            for half in range(2):
                g = 2 * c + half
                k_lo = kk if half == 0 else pltpu.roll(kk, HEAD_DIM, 1)
                k_ref[0, g] = jnp.where(lane < HEAD_DIM, k_lo, key_pad).astype(k_ref.dtype)
                v_aug = jnp.concatenate([vv[half * HEAD_DIM:(half + 1) * HEAD_DIM], ones_rows], axis=0)
                for j in range(ROW_TILE // vtile):
                    vt_ref[0, g, j] = v_aug[:, j * vtile:(j + 1) * vtile].astype(vt_ref.dtype)


def _proj_kv(h, w, tabs, bsz, s):
    t, k = h.shape
    n = w.shape[1]
    spb = s // ROW_TILE
    tab_spec = pl.BlockSpec((ROPE_DIM // 2, ROW_TILE), lambda i: (0, i))
    k_spec = pl.BlockSpec((1, KV_GROUPS, ROW_TILE, LANES), lambda i: (i // spb, 0, i % spb, 0))
    k_shape = jax.ShapeDtypeStruct((bsz, KV_GROUPS, s, LANES), BF16)

    def vt(tile):
        return (pl.BlockSpec((1, KV_GROUPS, ROW_TILE // tile, V_ROWS, tile), lambda i: (i // spb, 0, i % spb, 0, 0)),
                jax.ShapeDtypeStruct((bsz, KV_GROUPS, s // tile, V_ROWS, tile), BF16))

    (vs_spec, vs_shape), (vw_spec, vw_shape) = vt(KEY_STEP), vt(Q_BLOCK)
    return pl.pallas_call(
        _proj_kv_kernel,
        grid=(t // ROW_TILE,),
        in_specs=[pl.BlockSpec((ROW_TILE, k), lambda i: (i, 0)),
                  pl.BlockSpec((k, n), lambda i: (0, 0)),
                  tab_spec, tab_spec],
        out_specs=[pl.BlockSpec((1, 2 * KV_GROUPS, ROW_TILE, HEAD_DIM), lambda i: (i // spb, 0, i % spb, 0)),
                   k_spec, k_spec, vs_spec, vw_spec],
        out_shape=[jax.ShapeDtypeStruct((bsz, 2 * KV_GROUPS, s, HEAD_DIM), F32),
                   k_shape, k_shape, vs_shape, vw_shape],
        compiler_params=_cparams(("parallel",)),
        name="proj_kv",
    )(h, w, *tabs)


def _ssd_kernel(z_ref, xs_ref, bc_ref, misc_ref, convw_ref, convb_ref, dtb_ref, alog_ref,
                dskip_ref, gnorm_ref, selb_ref, sele_ref, wout_ref, o_ref,
                tail_ref, act_ref, state_ref):
    c = pl.program_id(1)
    L = CHUNK
    gw = SSM_HPG * SSM_HEAD_DIM

    @pl.when(c == 0)
    def _():
        state_ref[...] = jnp.zeros_like(state_ref)
        tail_ref[...] = jnp.zeros_like(tail_ref)

    cw = 512
    first_row = lax.broadcasted_iota(jnp.int32, (8, cw), 0) == 0
    for cc in range(SSM_XBC // cw):
        cols = slice(cc * cw, (cc + 1) * cw)
        u = xs_ref[:, cols] if cc < D_INNER // cw else bc_ref[:, cc * cw - D_INNER:(cc + 1) * cw - D_INNER]
        a = convw_ref[0:1, cols] * u
        for k in range(1, SSM_CONV):
            shifted = pltpu.roll(a, 1, 0)
            shifted = jnp.concatenate([jnp.where(first_row, tail_ref[k - 1:k, cols], shifted[0:8]), shifted[8:]], axis=0)
            tail_ref[k - 1:k, cols] = a[L - 1:L]
            a = convw_ref[k:k + 1, cols] * u + shifted
        act_ref[:, cols] = _silu(a + convb_ref[:, cols])

    lane = lax.broadcasted_iota(jnp.int32, (L, LANES), 1)
    row = lax.broadcasted_iota(jnp.int32, (L, LANES), 0)
    head_lane = lane < SSM_HEADS
    raw = misc_ref[...] + dtb_ref[...]
    dt = jnp.where(head_lane, jnp.maximum(raw, 0.0) + jnp.log1p(jnp.exp(-jnp.abs(raw))), 0.0)
    a_row = jnp.where(head_lane[0:1], -jnp.exp(alog_ref[...]), 0.0)
    cs = dt * a_row
    sh = 1
    while sh < L:
        cs = cs + jnp.where(row >= sh, pltpu.roll(cs, sh, 0), 0.0)
        sh *= 2
    tot = cs[L - 1:L, :]
    w_state = dt * jnp.exp(tot - cs)
    cs_t, dt_t, w_t = cs.T, dt.T, w_state.T

    hi, mid, lo = _split3(cs)
    packed = (hi.astype(F32) + pltpu.roll(mid.astype(F32), SSM_HEADS, 1)
              + pltpu.roll(lo.astype(F32), 2 * SSM_HEADS, 1)).astype(BF16)
    col_b = jnp.dot(packed, selb_ref[...], preferred_element_type=F32)
    col_e = jnp.dot(packed, sele_ref[...], preferred_element_type=F32)
    exp_e = jnp.exp(col_e)
    etot_e = exp_e[L - 1:L, :]

    tri = lax.broadcasted_iota(jnp.int32, (L, L), 0) >= lax.broadcasted_iota(jnp.int32, (L, L), 1)
    brow = lax.broadcasted_iota(jnp.int32, (SSM_HPG * L, gw), 0) // L
    bcol = lax.broadcasted_iota(jnp.int32, (SSM_HPG * L, gw), 1) // SSM_HEAD_DIM
    blockmask = brow == bcol

    for g in range(SSM_GROUPS):
        xs_g = act_ref[:, g * gw:(g + 1) * gw]
        b_g = act_ref[:, D_INNER + g * SSM_STATE: D_INNER + (g + 1) * SSM_STATE]
        c_g = act_ref[:, D_INNER + SSM_GROUPS * SSM_STATE + g * SSM_STATE:
                      D_INNER + SSM_GROUPS * SSM_STATE + (g + 1) * SSM_STATE]
        c_bf = c_g.astype(BF16)
        cb = lax.dot_general(c_bf, b_g.astype(BF16), (((1,), (1,)), ((), ())),
                             preferred_element_type=F32)
        b_t = b_g.T
        tops, bots = [], []
        for j in range(SSM_HPG):
            h = g * SSM_HPG + j
            decay = jnp.exp(jnp.where(tri, col_b[:, h * L:(h + 1) * L] - cs_t[h:h + 1, :], NEG))
            tops.append((cb * decay * dt_t[h:h + 1, :]).astype(BF16))
            bots.append((b_t * w_t[h:h + 1, :]).astype(BF16))
        lhs = jnp.concatenate([jnp.concatenate(tops, axis=1), jnp.concatenate(bots, axis=1)], axis=0)
        xs_bf = xs_g.astype(BF16)
        xs_bd = jnp.where(blockmask, jnp.concatenate([xs_bf] * SSM_HPG, axis=0), jnp.zeros((), BF16))
        res = jnp.dot(lhs, xs_bd, preferred_element_type=F32)
        st = state_ref[g]
        y_off = jnp.dot(c_bf, st.astype(BF16), preferred_element_type=F32) * exp_e[:, g * gw:(g + 1) * gw]
        state_ref[g] = st * etot_e[:, g * gw:(g + 1) * gw] + res[L:2 * L]
        y = res[0:L] + y_off + dskip_ref[:, g * gw:(g + 1) * gw] * xs_g
        zg = z_ref[:, g * gw:(g + 1) * gw]
        y = y * _silu(zg)
        ms = jnp.mean(y * y, axis=-1, keepdims=True)
        act_ref[:, g * gw:(g + 1) * gw] = y * lax.rsqrt(ms + NORM_EPS) * gnorm_ref[:, g * gw:(g + 1) * gw]

    o_ref[...] = jnp.dot(act_ref[:, 0:D_INNER].astype(BF16), wout_ref[...], preferred_element_type=F32)


def _ssd(raw, raw_gate, conv_w, conv_b, dt_bias, a_log, d_skip, g_ssm_norm, w_ssm, bsz, s):
    t = raw.shape[0]
    nc = s // CHUNK
    blk = lambda cb: pl.BlockSpec((CHUNK, 2048), lambda b, c, cb=cb: (b * nc + c, cb))
    const = lambda shape: pl.BlockSpec(shape, lambda b, c: (0,) * len(shape))
    k = jnp.arange(LANES)
    selb = ((k[:, None] < 3 * SSM_HEADS) & ((k[:, None] % SSM_HEADS) == (jnp.arange(SSM_HEADS * LANES)[None, :] // LANES)))
    sele = ((k[:, None] < 3 * SSM_HEADS) & ((k[:, None] % SSM_HEADS) == (jnp.arange(D_INNER)[None, :] // SSM_HEAD_DIM)))
    pad = lambda v: jnp.pad(v.astype(F32), (0, LANES - SSM_HEADS)).reshape(1, LANES)
    return pl.pallas_call(
        _ssd_kernel,
        grid=(bsz, nc),
        in_specs=[blk(RAW_Z // 2048), blk(RAW_XS // 2048), blk(RAW_BC // 2048),
                  pl.BlockSpec((CHUNK, LANES), lambda b, c: (b * nc + c, GATE_MISC // LANES)),
                  const((SSM_CONV, SSM_XBC)), const((1, SSM_XBC)), const((1, LANES)), const((1, LANES)),
                  const((1, D_INNER)), const((1, D_INNER)),
                  const((LANES, SSM_HEADS * LANES)), const((LANES, D_INNER)),
                  const((D_INNER, D_MODEL))],
        out_specs=pl.BlockSpec((CHUNK, D_MODEL), lambda b, c: (b * nc + c, 0)),
        out_shape=jax.ShapeDtypeStruct((t, D_MODEL), F32),
        scratch_shapes=[pltpu.VMEM((8, SSM_XBC), F32),
                        pltpu.VMEM((CHUNK, SSM_XBC), F32),
                        pltpu.VMEM((SSM_GROUPS, SSM_STATE, SSM_HPG * SSM_HEAD_DIM), F32)],
        compiler_params=_cparams(("parallel", "arbitrary")),
        name="ssd_mixer",
    )(raw, raw, raw, raw_gate, conv_w.astype(F32), conv_b.reshape(1, SSM_XBC).astype(F32),
      pad(dt_bias), pad(a_log), jnp.repeat(d_skip.astype(F32), SSM_HEAD_DIM).reshape(1, D_INNER),
      g_ssm_norm.reshape(1, D_INNER).astype(F32), selb.astype(BF16), sele.astype(BF16), w_ssm.astype(BF16))


def _compress_kernel(u_ref, pelo_ref, pehi_ref, w1_ref, w2_ref, o_ref):
    u = u_ref[0]
    half = CMP_STRIDE * HEAD_DIM
    a = jnp.dot((u + pelo_ref[0, 0]).astype(BF16), w1_ref[0, 0:half, :], preferred_element_type=F32)
    b = jnp.dot((u + pehi_ref[0, 0]).astype(BF16), w1_ref[0, half:2 * half, :], preferred_element_type=F32)
    pre = a + pltpu.roll(b, u.shape[0] - 1, 0)
    hidden = _silu(pre)
    out = jnp.dot(hidden.astype(BF16), w2_ref[0], preferred_element_type=F32)
    lane = lax.broadcasted_iota(jnp.int32, out.shape, 1)
    o_ref[0] = jnp.where(lane == HEAD_DIM, 1.0, out).astype(o_ref.dtype)


def _compress(u, pe, w1, w2):
    n, rows, width = u.shape
    kv_of = lambda i: (i // KV_GROUPS) % 2
    return pl.pallas_call(
        _compress_kernel,
        grid=(n,),
        in_specs=[pl.BlockSpec((1, rows, width), lambda i: (i, 0, 0)),
                  pl.BlockSpec((1, 1, 1, width), lambda i: (kv_of(i), 0, 0, 0)),
                  pl.BlockSpec((1, 1, 1, width), lambda i: (kv_of(i), 1, 0, 0)),
                  pl.BlockSpec((1, 2 * width, CMP_HIDDEN), lambda i: (kv_of(i), 0, 0)),
                  pl.BlockSpec((1, CMP_HIDDEN, LANES), lambda i: (kv_of(i), 0, 0))],
        out_specs=pl.BlockSpec((1, rows, LANES), lambda i: (i, 0, 0)),
        out_shape=jax.ShapeDtypeStruct((n, rows, LANES), BF16),
        compiler_params=_cparams(("parallel",)),
        name="compress",
    )(u, pe, pe, w1, w2)


def _nsa_kernel(qn_ref, qr_ref, kc_ref, vct_ref, ks_ref, vst_ref, kw_ref, vwt_ref, gate_ref, ovt_ref, oh_ref,
                eye_ref, wlo_ref, whi_ref,
                o_ref, gt_ref, qa_ref, sa_ref, sb_ref, pa_ref, pb_ref, sc_ref, *, topk):
    gp = pl.program_id(1)
    qb = pl.program_id(2)
    groups = range(NSA_GROUPS)
    nq = ATT_HPG * Q_BLOCK
    ncmp = kc_ref.shape[1]
    nsel = ovt_ref.shape[0]
    q0 = qb * Q_BLOCK
    tq = q0 + lax.broadcasted_iota(jnp.int32, (1, nq), 1) % Q_BLOCK

    def heads_on_lanes(ref, gi):
        base = gi * ATT_HPG * HEAD_DIM
        return jnp.concatenate([ref[0, base + r * HEAD_DIM:base + (r + 1) * HEAD_DIM, :]
                                for r in range(ATT_HPG)], axis=1)

    qn_t = [heads_on_lanes(qn_ref, gi) for gi in groups]
    qr_t = [heads_on_lanes(qr_ref, gi) for gi in groups]

    def q_operand(q_t, flag, table):
        flag_rows = jnp.concatenate([flag, jnp.zeros((LANES - HEAD_DIM - 1, nq), F32)], axis=0).astype(BF16)
        return jnp.concatenate([q_t, flag_rows] + ([table] if table is not None else []), axis=0)

    no_flag = jnp.zeros((1, nq), F32)

    n_chunks = ncmp // LANES
    chunk_span = LANES * CMP_STRIDE
    for c in range(n_chunks):
        none_valid = c * chunk_span + CMP_LEN - 1 > q0 + Q_BLOCK - 1
        for gi in groups:
            sc_ref[gi, c * LANES:(c + 1) * LANES, :] = jnp.dot(
                kc_ref[gi, c * LANES:(c + 1) * LANES, :],
                q_operand(qn_t[gi], jnp.where(none_valid, NEG, no_flag), None), preferred_element_type=F32)
    c_hi = jnp.minimum((q0 + Q_BLOCK - CMP_LEN) // chunk_span, n_chunks - 1)
    for cc in (c_hi, jnp.maximum(c_hi - 1, 0)):
        rows = pl.ds(pl.multiple_of(cc * LANES, LANES), LANES)
        ends = cc * chunk_span + CMP_LEN - 1 + lax.broadcasted_iota(jnp.int32, (LANES, nq), 0) * CMP_STRIDE
        for gi in groups:
            sc_ref[gi, rows, :] = jnp.where(ends <= tq, sc_ref[gi, rows, :], NEG)
    p_c, acc_c, inv_c = [], [], []
    for gi in groups:
        s_c = sc_ref[gi]
        m_c = jnp.max(s_c, axis=0, keepdims=True)
        p_c.append(jnp.exp2(s_c - m_c))
        acc_c.append(jnp.dot(vct_ref[gi], p_c[gi].astype(BF16), preferred_element_type=F32))
        inv_c.append(jnp.where(m_c > 0.5 * NEG, 1.0 / acc_c[gi][HEAD_DIM:HEAD_DIM + 1], 0.0))

    n_wt = WINDOW // Q_BLOCK + 1
    eye = eye_ref[...]
    acc_w = []
    for gi in groups:
        s_w, v_w = [], []
        for i in range(n_wt):
            kt = qb - (n_wt - 1) + i
            k0 = pl.multiple_of(jnp.maximum(kt, 0) * Q_BLOCK, Q_BLOCK)
            keys = kw_ref[gi, pl.ds(k0, Q_BLOCK), :]
            flag = jnp.where(kt >= 0, no_flag, NEG)
            if i == 0:
                s = jnp.dot(jnp.concatenate([keys, eye], axis=1), q_operand(qr_t[gi], flag, wlo_ref[...]),
                            preferred_element_type=F32)
            elif i == n_wt - 1:
                s = jnp.dot(jnp.concatenate([keys, eye], axis=1), q_operand(qr_t[gi], flag, whi_ref[...]),
                            preferred_element_type=F32)
            else:
                s = jnp.dot(keys, q_operand(qr_t[gi], flag, None), preferred_element_type=F32)
            s_w.append(s)
            v_w.append(vwt_ref[gi, jnp.maximum(kt, 0)])
        s_w = jnp.concatenate(s_w, axis=0)
        p_w = jnp.exp2(s_w - jnp.max(s_w, axis=0, keepdims=True))
        acc_w.append(jnp.dot(jnp.concatenate(v_w, axis=1), p_w.astype(BF16), preferred_element_type=F32))

    imp = []
    for gi in groups:
        p_sum = p_c[gi][:, 0:Q_BLOCK] * inv_c[gi][:, 0:Q_BLOCK]
        for r in range(1, ATT_HPG):
            p_sum = p_sum + p_c[gi][:, r * Q_BLOCK:(r + 1) * Q_BLOCK] * inv_c[gi][:, r * Q_BLOCK:(r + 1) * Q_BLOCK]
        imp.append(jnp.dot(ovt_ref[...], jnp.concatenate(_split3(p_sum), axis=0), preferred_element_type=F32))

    blk = lax.broadcasted_iota(jnp.int32, (nsel, Q_BLOCK), 0)
    cur = (q0 + lax.broadcasted_iota(jnp.int32, (nsel, Q_BLOCK), 1)) // SEL_BLOCK
    valid = blk <= cur
    forced = valid & ((blk == 0) | (blk == cur) | (blk == cur - 1))
    val = [jnp.where(forced, REMOVED, jnp.where(valid, imp[gi], NEG)) for gi in groups]
    chosen = [jnp.where(forced, 1.0, 0.0) for gi in groups]
    blk_f = blk.astype(F32)
    for _ in range(topk - 3):
        for gi in groups:
            mx = jnp.max(val[gi], axis=0, keepdims=True)
            first = jnp.min(jnp.where(val[gi] == mx, blk_f, float(nsel)), axis=0, keepdims=True)
            hit = blk_f == first
            chosen[gi] = jnp.where(hit, 1.0, chosen[gi])
            val[gi] = jnp.where(hit, REMOVED, val[gi])

    pad_rows = [jnp.zeros((LANES - nsel, nq), F32)] if nsel < LANES else []
    for gi in groups:
        bias = jnp.concatenate([(chosen[gi] - 1.0) * BIG] * ATT_HPG, axis=1)
        qa_ref[gi] = q_operand(qr_t[gi], no_flag, jnp.concatenate([bias] + pad_rows, axis=0).astype(BF16))

    def sel_scores(gi, i):
        k0 = pl.multiple_of(i * KEY_STEP, KEY_STEP)
        lhs = jnp.concatenate([ks_ref[gi, pl.ds(k0, KEY_STEP), :], oh_ref[pl.ds(k0, KEY_STEP), :]], axis=1)
        return jnp.dot(lhs, qa_ref[gi], preferred_element_type=F32)

    row_k = lax.broadcasted_iota(jnp.int32, (KEY_STEP, nq), 0)

    def softmax_step(s_ref, p_ref, gi, m, first_key):
        s = s_ref[gi]
        if first_key is not None:
            s = jnp.where(row_k <= tq - first_key, s, NEG)
        m_new = jnp.maximum(m, jnp.max(s, axis=0, keepdims=True))
        p_ref[gi] = jnp.exp2(s - m_new).astype(BF16)
        return m_new, jnp.exp2(m - m_new)

    def pv(gi, acc, alpha, p_ref, i):
        return acc * alpha + jnp.dot(vst_ref[gi, i], p_ref[gi], preferred_element_type=F32)

    def sel_pair(j, carries, final):
        a = 2 * j
        m, acc, alpha_b = ([c[k] for c in carries] for k in range(3))
        alpha_a = [None] * NSA_GROUPS
        for gi in groups:
            sb_ref[gi] = sel_scores(gi, a + 1)
            acc[gi] = pv(gi, acc[gi], alpha_b[gi], pb_ref, jnp.maximum(a - 1, 0))
            m[gi], alpha_a[gi] = softmax_step(sa_ref, pa_ref, gi, m[gi], a * KEY_STEP if final else None)
            if not final:
                sa_ref[gi] = sel_scores(gi, a + 2)
            acc[gi] = pv(gi, acc[gi], alpha_a[gi], pa_ref, a)
            m[gi], alpha_b[gi] = softmax_step(sb_ref, pb_ref, gi, m[gi], (a + 1) * KEY_STEP if final else None)
        return tuple((m[gi], acc[gi], alpha_b[gi]) for gi in groups)

    n_pairs = (q0 + Q_BLOCK - 1) // (2 * KEY_STEP) + 1
    for gi in groups:
        sa_ref[gi] = sel_scores(gi, 0)
    pb_ref[...] = jnp.zeros_like(pb_ref)
    init = tuple((jnp.full((1, nq), NEG, F32), jnp.zeros((V_ROWS, nq), F32), jnp.ones((1, nq), F32))
                 for gi in groups)
    carries = lax.fori_loop(0, n_pairs - 1, lambda j, c: sel_pair(j, c, False), init)
    carries = sel_pair(n_pairs - 1, carries, True)
    acc_s = [pv(gi, carries[gi][1], carries[gi][2], pb_ref, 2 * n_pairs - 1) for gi in groups]

    gt_ref[...] = _sigmoid(gate_ref[...]).T
    hd = HEAD_DIM
    for gi in groups:
        def gate_row(branch, gi=gi):
            base = MISC_GATE0 + branch * ATT_HEADS + (gp * NSA_GROUPS + gi) * ATT_HPG
            return jnp.concatenate([gt_ref[pl.ds(base + r, 1), :] for r in range(ATT_HPG)], axis=1)

        o = ((gate_row(0) * inv_c[gi]) * acc_c[gi][0:hd] + (gate_row(1) / acc_s[gi][hd:hd + 1]) * acc_s[gi][0:hd]
             + (gate_row(2) / acc_w[gi][hd:hd + 1]) * acc_w[gi][0:hd])
        for r in range(ATT_HPG):
            row0 = (gi * ATT_HPG + r) * HEAD_DIM
            o_ref[0, row0:row0 + HEAD_DIM, :] = o[:, r * Q_BLOCK:(r + 1) * Q_BLOCK].astype(o_ref.dtype)


def _nsa(qn_t, qr_t, kc, vc_t, ks, vs_t, kw, vw_t, raw_gate, bsz, s):
    nqb = s // Q_BLOCK
    ncmp = s // CMP_STRIDE
    nsel = s // SEL_BLOCK
    topk = min(SEL_TOPK, nsel)
    gw = ATT_HPG * HEAD_DIM
    ci = jnp.arange(ncmp)[None, :]
    sj = jnp.arange(nsel)[:, None]
    ov_t = (ci * CMP_STRIDE < (sj + 1) * SEL_BLOCK) & (ci * CMP_STRIDE + CMP_LEN > sj * SEL_BLOCK) \
        & (ci < ncmp - 1)
    ov_t3 = jnp.concatenate([ov_t.astype(BF16)] * 3, axis=1)
    assert nsel <= LANES
    onehot = (jnp.arange(s)[:, None] // SEL_BLOCK == jnp.arange(LANES)[None, :]).astype(BF16)
    eye = jnp.eye(Q_BLOCK, dtype=BF16)
    key_row = jnp.arange(Q_BLOCK)[:, None]
    q_lane = jnp.arange(ATT_HPG * Q_BLOCK)[None, :] % Q_BLOCK
    win_lo = jnp.where(key_row > q_lane, 0.0, NEG).astype(BF16)
    win_hi = jnp.where(key_row <= q_lane, 0.0, NEG).astype(BF16)
    const2 = lambda shape: pl.BlockSpec(shape, lambda b, g, i: (0, 0))
    ng = NSA_GROUPS
    gpb = KV_GROUPS // ng
    q_spec = pl.BlockSpec((1, ng * gw, Q_BLOCK), lambda b, g, i: (b, g, i))
    per_bg = lambda shape: pl.BlockSpec((ng,) + shape, lambda b, g, i: (b * gpb + g,) + (0,) * len(shape))
    nql = ATT_HPG * Q_BLOCK
    return pl.pallas_call(
        functools.partial(_nsa_kernel, topk=topk),
        grid=(bsz, gpb, nqb),
        in_specs=[q_spec, q_spec,
                  per_bg((ncmp, LANES)), per_bg((V_ROWS, ncmp)),
                  per_bg((s, LANES)), per_bg((s // KEY_STEP, V_ROWS, KEY_STEP)),
                  per_bg((s, LANES)), per_bg((s // Q_BLOCK, V_ROWS, Q_BLOCK)),
                  pl.BlockSpec((Q_BLOCK, LANES), lambda b, g, i: (b * nqb + i, GATE_MISC // LANES)),
                  const2((nsel, 3 * ncmp)), const2((s, LANES)),
                  const2((Q_BLOCK, Q_BLOCK)), const2((Q_BLOCK, ATT_HPG * Q_BLOCK)),
                  const2((Q_BLOCK, ATT_HPG * Q_BLOCK))],
        out_specs=pl.BlockSpec((1, ng * gw, Q_BLOCK), lambda b, g, i: (b, g, i)),
        out_shape=jax.ShapeDtypeStruct((bsz, ATT_HEADS * HEAD_DIM, s), BF16),
        scratch_shapes=[pltpu.VMEM((LANES, Q_BLOCK), F32),
                        pltpu.VMEM((ng, 2 * LANES, nql), BF16),
                        pltpu.VMEM((ng, KEY_STEP, nql), F32),
                        pltpu.VMEM((ng, KEY_STEP, nql), F32),
                        pltpu.VMEM((ng, KEY_STEP, nql), BF16),
                        pltpu.VMEM((ng, KEY_STEP, nql), BF16),
                        pltpu.VMEM((ng, ncmp, nql), F32)],
        compiler_params=_cparams(("parallel", "parallel", "arbitrary")),
        name="nsa_sweep",
    )(qn_t, qr_t, kc, vc_t, ks, vs_t, kw, vw_t, raw_gate, ov_t3, onehot, eye, win_lo, win_hi)


def _merge_kernel(ot_ref, yssm_ref, mix_ref, x_ref, wab_ref, wo_ref, o_ref):
    o = ot_ref[0].astype(F32).T.astype(BF16)
    y_att = jnp.dot(o, wab_ref[...], preferred_element_type=F32)
    gates = _sigmoid(mix_ref[...])
    mixed = gates[:, 0:D_MODEL] * yssm_ref[...] + gates[:, D_MODEL:2 * D_MODEL] * y_att
    o_ref[...] = x_ref[...] + jnp.dot(mixed.astype(BF16), wo_ref[...], preferred_element_type=F32)


def _merge(o_t, y_ssm, raw, x2d, w_ab, w_o, s):
    t = x2d.shape[0]
    spb = s // ROW_TILE
    row_spec = pl.BlockSpec((ROW_TILE, D_MODEL), lambda i: (i, 0))
    w_spec = pl.BlockSpec((D_MODEL, D_MODEL), lambda i: (0, 0))
    return pl.pallas_call(
        _merge_kernel,
        grid=(t // ROW_TILE,),
        in_specs=[pl.BlockSpec((1, D_MODEL, ROW_TILE), lambda i: (i // spb, 0, i % spb)),
                  row_spec,
                  pl.BlockSpec((ROW_TILE, 2 * D_MODEL), lambda i: (i, GATE_MIX // (2 * D_MODEL))),
                  row_spec, w_spec, w_spec],
        out_specs=row_spec,
        out_shape=jax.ShapeDtypeStruct((t, D_MODEL), F32),
        compiler_params=_cparams(("parallel",)),
        name="merge",
    )(o_t, y_ssm, raw, x2d, w_ab, w_o)


def _mlp_kernel(x_ref, g1_ref, wup_ref, wdn_ref, g2_ref, o_ref):
    x = x_ref[...]
    ms = jnp.mean(x * x, axis=-1, keepdims=True)
    h = (x * lax.rsqrt(ms + NORM_EPS) * g1_ref[...]).astype(BF16)
    up = jnp.maximum(jnp.dot(h, wup_ref[...], preferred_element_type=F32), 0.0)
    y = x + jnp.dot((up * up).astype(BF16), wdn_ref[...], preferred_element_type=F32)
    ms2 = jnp.mean(y * y, axis=-1, keepdims=True)
    o_ref[...] = y * lax.rsqrt(ms2 + NORM_EPS) * g2_ref[...]


def _mlp(x1, g1, w_up, w_down, g2):
    t = x1.shape[0]
    tm = ROW_TILE // 2
    row_spec = pl.BlockSpec((tm, D_MODEL), lambda i: (i, 0))
    g_spec = pl.BlockSpec((1, D_MODEL), lambda i: (0, 0))
    return pl.pallas_call(
        _mlp_kernel,
        grid=(t // tm,),
        in_specs=[row_spec, g_spec,
                  pl.BlockSpec((D_MODEL, MLP_HIDDEN), lambda i: (0, 0)),
                  pl.BlockSpec((MLP_HIDDEN, D_MODEL), lambda i: (0, 0)),
                  g_spec],
        out_specs=row_spec,
        out_shape=jax.ShapeDtypeStruct((t, D_MODEL), F32),
        compiler_params=_cparams(("parallel",)),
        name="mlp",
    )(x1, g1.reshape(1, D_MODEL), w_up, w_down, g2.reshape(1, D_MODEL))


def _rope_tables(positions):
    inv_freq = ROPE_THETA ** (-jnp.arange(0, ROPE_DIM, 2, dtype=F32) / ROPE_DIM)
    ang = positions.astype(F32).reshape(1, -1) * inv_freq[:, None]
    return jnp.cos(ang), jnp.sin(ang)


def kernel(x, positions, g_norm_mix, w_in, conv_w, conv_b, dt_bias, a_log, d_skip, g_ssm_norm, w_ssm_branch, cmp_pe_k, cmp_pe_v, w_cmp_k1, w_cmp_k2, w_cmp_v1, w_cmp_v2, w_attn_branch, w_o, g_norm_mlp, w_up, w_down, g_norm_final):
    bsz, s, d = x.shape
    assert d == D_MODEL and s % ROW_TILE == 0 and w_in.shape[0] == 1
    t = bsz * s
    x2d = x.reshape(t, d)
    tabs = _rope_tables(positions)

    w = w_in[0].astype(BF16)
    o_dt, o_q, o_kv, o_ag, o_mg = 6144, 6176, 7200, 8736, 8784
    misc_pad = jnp.zeros((d, GATE_COLS - GATE_MISC - SSM_HEADS - 3 * ATT_HEADS), BF16)
    w_gate = jnp.concatenate([w[:, o_mg:], w[:, o_dt:o_q], w[:, o_ag:o_mg], misc_pad], axis=1)
    w_q = w[:, o_q:o_kv]
    w_kv = w[:, o_kv:o_ag]

    h = _rmsnorm(x2d, g_norm_mix[0])
    raw = _proj_raw(h, w, SSM_COLS, SSM_TN, "proj_ssm")
    raw_gate = _proj_raw(h, w_gate, GATE_COLS, GATE_COLS, "proj_gate")
    qn_t, qr_t = _proj_q(h, w_q, tabs, bsz, s)
    kvc, ks, kw, vs_t, vw_t = _proj_kv(h, w_kv, tabs, bsz, s)

    y_ssm = _ssd(raw, raw_gate, conv_w[0], conv_b[0], dt_bias[0], a_log[0], d_skip[0], g_ssm_norm[0],
                 w_ssm_branch[0], bsz, s)

    rows = s // CMP_STRIDE
    u = kvc.reshape(bsz * 2 * KV_GROUPS, rows, CMP_STRIDE * HEAD_DIM)
    pe = jnp.stack([cmp_pe_k[0], cmp_pe_v[0]]).astype(F32).reshape(2, 2, 1, CMP_STRIDE * HEAD_DIM)
    w1 = jnp.stack([w_cmp_k1[0], w_cmp_v1[0]]).astype(BF16)
    w2 = jnp.pad(jnp.stack([w_cmp_k2[0], w_cmp_v2[0]]).astype(BF16), ((0, 0), (0, 0), (0, LANES - HEAD_DIM)))
    cmp = _compress(u, pe, w1, w2).reshape(bsz, 2, KV_GROUPS, rows, LANES)
    kc = cmp[:, 0].reshape(bsz * KV_GROUPS, rows, LANES)
    vc_t = jnp.swapaxes(cmp[:, 1], -1, -2)[:, :, :V_ROWS].reshape(bsz * KV_GROUPS, V_ROWS, rows)

    flat = lambda a: a.reshape((bsz * KV_GROUPS,) + a.shape[2:])
    o_t = _nsa(qn_t, qr_t, kc, vc_t, flat(ks), flat(vs_t), flat(kw), flat(vw_t), raw_gate, bsz, s)

    x1 = _merge(o_t, y_ssm, raw_gate, x2d, w_attn_branch[0].astype(BF16), w_o[0].astype(BF16), s)
    out = _mlp(x1, g_norm_mlp[0], w_up[0].astype(BF16), w_down[0].astype(BF16), g_norm_final)
    return out.reshape(bsz, s, d)
```

```python
import functools

import jax
import jax.numpy as jnp
from jax import lax
from jax.experimental import pallas as pl
from jax.experimental.pallas import tpu as pltpu

F32 = jnp.float32
BF16 = jnp.bfloat16

D_MODEL = 1024
D_INNER = 2048
SSM_HEADS = 32
SSM_GROUPS = 8
SSM_HPG = 4
SSM_HEAD_DIM = 64
SSM_STATE = 128
SSM_CONV = 4
CHUNK = 128
SSM_XBC = 4096
HEAD_DIM = 64
ATT_HEADS = 16
KV_GROUPS = 4
ATT_HPG = 4
CMP_LEN = 32
CMP_STRIDE = 16
CMP_HIDDEN = 256
SEL_BLOCK = 64
SEL_TOPK = 16
WINDOW = 512
Q_BLOCK = 128
ROPE_THETA = 500000.0
ROPE_DIM = 16
MLP_HIDDEN = 4096
NORM_EPS = 1e-6
NEG = -1e30
BIG = 1e30
REMOVED = -3e38
LOG2E = 1.4426950408889634
SCALE = HEAD_DIM ** -0.5 * LOG2E
KEY_STEP = 256
NSA_GROUPS = 4

LANES = 128
ROW_TILE = 512
VMEM_LIMIT = 56 * 1024 * 1024

RAW_Z, RAW_XS, RAW_BC = 0, 2048, 4096
SSM_COLS, SSM_TN = 6144, 1536
GATE_MIX, GATE_MISC, GATE_COLS = 0, 2048, 2176
MISC_GATE0 = SSM_HEADS
V_ROWS = HEAD_DIM + 16


def _cparams(sem):
    return pltpu.CompilerParams(dimension_semantics=sem, vmem_limit_bytes=VMEM_LIMIT)


def _sigmoid(x):
    return 0.5 * jnp.tanh(0.5 * x) + 0.5


def _silu(x):
    h = 0.5 * x
    return h + h * jnp.tanh(h)


def _split3(x):
    hi = x.astype(BF16)
    r1 = x - hi.astype(F32)
    mid = r1.astype(BF16)
    lo = (r1 - mid.astype(F32)).astype(BF16)
    return hi, mid, lo


def _rmsnorm_kernel(x_ref, g_ref, o_ref):
    x = x_ref[...]
    ms = jnp.mean(x * x, axis=-1, keepdims=True)
    o_ref[...] = (x * lax.rsqrt(ms + NORM_EPS) * g_ref[...]).astype(o_ref.dtype)


def _rmsnorm(x2d, g):
    t, d = x2d.shape
    return pl.pallas_call(
        _rmsnorm_kernel,
        grid=(t // ROW_TILE,),
        in_specs=[pl.BlockSpec((ROW_TILE, d), lambda i: (i, 0)),
                  pl.BlockSpec((1, d), lambda i: (0, 0))],
        out_specs=pl.BlockSpec((ROW_TILE, d), lambda i: (i, 0)),
        out_shape=jax.ShapeDtypeStruct((t, d), BF16),
        compiler_params=_cparams(("parallel",)),
        name="rmsnorm",
    )(x2d, g.reshape(1, d))


def _proj_raw_kernel(a_ref, w_ref, o_ref):
    o_ref[...] = jnp.dot(a_ref[...], w_ref[...], preferred_element_type=F32)


def _proj_raw(h, w, n, tn, name):
    t, k = h.shape
    return pl.pallas_call(
        _proj_raw_kernel,
        grid=(n // tn, t // ROW_TILE),
        in_specs=[pl.BlockSpec((ROW_TILE, k), lambda j, i: (i, 0)),
                  pl.BlockSpec((k, tn), lambda j, i: (0, j))],
        out_specs=pl.BlockSpec((ROW_TILE, tn), lambda j, i: (i, j)),
        out_shape=jax.ShapeDtypeStruct((t, n), F32),
        compiler_params=_cparams(("parallel", "parallel")),
        name=name,
    )(h, w)


def _rope_t(t_t, cos_t, sin_t):
    half = ROPE_DIM // 2
    rows = []
    for base in range(0, t_t.shape[0], HEAD_DIM):
        t1, t2 = t_t[base:base + half], t_t[base + half:base + ROPE_DIM]
        rows += [t1 * cos_t - t2 * sin_t, t2 * cos_t + t1 * sin_t, t_t[base + ROPE_DIM:base + HEAD_DIM]]
    return jnp.concatenate(rows, axis=0)


def _proj_q_kernel(a_ref, w_ref, cos_ref, sin_ref, qn_ref, qr_ref):
    acc = jnp.dot(a_ref[...], w_ref[...], preferred_element_type=F32)
    cos_t, sin_t = cos_ref[...], sin_ref[...]
    for c in range(acc.shape[1] // LANES):
        t_t = acc[:, c * LANES:(c + 1) * LANES].T
        qn_ref[0, c * LANES:(c + 1) * LANES, :] = (t_t * SCALE).astype(qn_ref.dtype)
        qr_ref[0, c * LANES:(c + 1) * LANES, :] = (_rope_t(t_t, cos_t, sin_t) * SCALE).astype(qr_ref.dtype)


def _proj_q(h, w, tabs, bsz, s):
    t, k = h.shape
    n = w.shape[1]
    spb = s // ROW_TILE
    tab_spec = pl.BlockSpec((ROPE_DIM // 2, ROW_TILE), lambda i: (0, i))
    out_spec = pl.BlockSpec((1, n, ROW_TILE), lambda i: (i // spb, 0, i % spb))
    return pl.pallas_call(
        _proj_q_kernel,
        grid=(t // ROW_TILE,),
        in_specs=[pl.BlockSpec((ROW_TILE, k), lambda i: (i, 0)),
                  pl.BlockSpec((k, n), lambda i: (0, 0)),
                  tab_spec, tab_spec],
        out_specs=[out_spec, out_spec],
        out_shape=[jax.ShapeDtypeStruct((bsz, n, s), BF16)] * 2,
        compiler_params=_cparams(("parallel",)),
        name="proj_q",
    )(h, w, *tabs)


def _proj_kv_kernel(a_ref, w_ref, cos_ref, sin_ref, kvc_ref, ks_ref, kw_ref, vst_ref, vwt_ref, kvc_scr):
    acc = jnp.dot(a_ref[...], w_ref[...], preferred_element_type=F32)
    cos_t, sin_t = cos_ref[...], sin_ref[...]
    tm = acc.shape[0]
    gw = KV_GROUPS * HEAD_DIM
    rows = tm // CMP_STRIDE
    low = lax.broadcasted_iota(jnp.int32, (rows, LANES), 1) < HEAD_DIM
    for c in range(2 * gw // LANES):
        kvc_scr[c] = acc[:, c * LANES:(c + 1) * LANES]
        for j in range(CMP_STRIDE // 2):
            e = kvc_scr[c, pl.ds(2 * j, rows, stride=CMP_STRIDE), :]
            o = kvc_scr[c, pl.ds(2 * j + 1, rows, stride=CMP_STRIDE), :]
            kvc_ref[0, 2 * c, :, j * LANES:(j + 1) * LANES] = jnp.where(low, e, pltpu.roll(o, HEAD_DIM, 1))
            kvc_ref[0, 2 * c + 1, :, j * LANES:(j + 1) * LANES] = jnp.where(low, pltpu.roll(e, HEAD_DIM, 1), o)
    lane = lax.broadcasted_iota(jnp.int32, (tm, LANES), 1)
    key_pad = jnp.where(lane == HEAD_DIM, 1.0, 0.0)
    ones_rows = jnp.where(lax.broadcasted_iota(jnp.int32, (V_ROWS - HEAD_DIM, tm), 0) == 0, 1.0, 0.0)
    for base, k_ref, vt_ref in ((2 * gw, ks_ref, vst_ref), (4 * gw, kw_ref, vwt_ref)):
        vtile = vt_ref.shape[-1]
        for c in range(gw // LANES):
            kk = _rope_t(acc[:, base + c * LANES: base + (c + 1) * LANES].T, cos_t, sin_t).T
            vv = acc[:, base + gw + c * LANES: base + gw + (c + 1) * LANES].T
            for half in range(2):
                g = 2 * c + half
                k_lo = kk if half == 0 else pltpu.roll(kk, HEAD_DIM, 1)
                k_ref[0, g] = jnp.where(lane < HEAD_DIM, k_lo, key_pad).astype(k_ref.dtype)
                v_aug = jnp.concatenate([vv[half * HEAD_DIM:(half + 1) * HEAD_DIM], ones_rows], axis=0)
                for j in range(ROW_TILE // vtile):
                    vt_ref[0, g, j] = v_aug[:, j * vtile:(j + 1) * vtile].astype(vt_ref.dtype)


def _proj_kv(h, w, tabs, bsz, s):
    t, k = h.shape
    n = w.shape[1]
    spb = s // ROW_TILE
    tab_spec = pl.BlockSpec((ROPE_DIM // 2, ROW_TILE), lambda i: (0, i))
    k_spec = pl.BlockSpec((1, KV_GROUPS, ROW_TILE, LANES), lambda i: (i // spb, 0, i % spb, 0))
    k_shape = jax.ShapeDtypeStruct((bsz, KV_GROUPS, s, LANES), BF16)

    def vt(tile):
        return (pl.BlockSpec((1, KV_GROUPS, ROW_TILE // tile, V_ROWS, tile), lambda i: (i // spb, 0, i % spb, 0, 0)),
                jax.ShapeDtypeStruct((bsz, KV_GROUPS, s // tile, V_ROWS, tile), BF16))

    (vs_spec, vs_shape), (vw_spec, vw_shape) = vt(KEY_STEP), vt(Q_BLOCK)
    return pl.pallas_call(
        _proj_kv_kernel,
        grid=(t // ROW_TILE,),
        in_specs=[pl.BlockSpec((ROW_TILE, k), lambda i: (i, 0)),
                  pl.BlockSpec((k, n), lambda i: (0, 0)),
                  tab_spec, tab_spec],
        out_specs=[pl.BlockSpec((1, 2 * KV_GROUPS, ROW_TILE // CMP_STRIDE, CMP_STRIDE * HEAD_DIM),
                                lambda i: (i // spb, 0, i % spb, 0)),
                   k_spec, k_spec, vs_spec, vw_spec],
        out_shape=[jax.ShapeDtypeStruct((bsz, 2 * KV_GROUPS, s // CMP_STRIDE, CMP_STRIDE * HEAD_DIM), F32),
                   k_shape, k_shape, vs_shape, vw_shape],
        scratch_shapes=[pltpu.VMEM((2 * KV_GROUPS * HEAD_DIM // LANES, ROW_TILE, LANES), F32)],
        compiler_params=_cparams(("parallel",)),
        name="proj_kv",
    )(h, w, *tabs)


def _ssd_kernel(z_ref, xs_ref, bc_ref, misc_ref, convw_ref, convb_ref, dtb_ref, alog_ref,
                dskip_ref, gnorm_ref, selb_ref, sele_ref, wout_ref, o_ref,
                tail_ref, act_ref, state_ref):
    c = pl.program_id(1)
    L = CHUNK
    gw = SSM_HPG * SSM_HEAD_DIM

    @pl.when(c == 0)
    def _():
        state_ref[...] = jnp.zeros_like(state_ref)
        tail_ref[...] = jnp.zeros_like(tail_ref)

    cw = 512
    first_row = lax.broadcasted_iota(jnp.int32, (8, cw), 0) == 0
    for cc in range(SSM_XBC // cw):
        cols = slice(cc * cw, (cc + 1) * cw)
        u = xs_ref[:, cols] if cc < D_INNER // cw else bc_ref[:, cc * cw - D_INNER:(cc + 1) * cw - D_INNER]
        a = convw_ref[0:1, cols] * u
        for k in range(1, SSM_CONV):
            shifted = pltpu.roll(a, 1, 0)
            shifted = jnp.concatenate([jnp.where(first_row, tail_ref[k - 1:k, cols], shifted[0:8]), shifted[8:]], axis=0)
            tail_ref[k - 1:k, cols] = a[L - 1:L]
            a = convw_ref[k:k + 1, cols] * u + shifted
        act_ref[:, cols] = _silu(a + convb_ref[:, cols])

    lane = lax.broadcasted_iota(jnp.int32, (L, LANES), 1)
    row = lax.broadcasted_iota(jnp.int32, (L, LANES), 0)
    head_lane = lane < SSM_HEADS
    raw = misc_ref[...] + dtb_ref[...]
    dt = jnp.where(head_lane, jnp.maximum(raw, 0.0) + jnp.log1p(jnp.exp(-jnp.abs(raw))), 0.0)
    a_row = jnp.where(head_lane[0:1], -jnp.exp(alog_ref[...]), 0.0)
    cs = dt * a_row
    sh = 1
    while sh < L:
        cs = cs + jnp.where(row >= sh, pltpu.roll(cs, sh, 0), 0.0)
        sh *= 2
    tot = cs[L - 1:L, :]
    w_state = dt * jnp.exp(tot - cs)
    cs_t, dt_t, w_t = cs.T, dt.T, w_state.T

    hi, mid, lo = _split3(cs)
    packed = (hi.astype(F32) + pltpu.roll(mid.astype(F32), SSM_HEADS, 1)
              + pltpu.roll(lo.astype(F32), 2 * SSM_HEADS, 1)).astype(BF16)
    col_b = jnp.dot(packed, selb_ref[...], preferred_element_type=F32)
    col_e = jnp.dot(packed, sele_ref[...], preferred_element_type=F32)
    exp_e = jnp.exp(col_e)
    etot_e = exp_e[L - 1:L, :]

    tri = lax.broadcasted_iota(jnp.int32, (L, L), 0) >= lax.broadcasted_iota(jnp.int32, (L, L), 1)
    brow = lax.broadcasted_iota(jnp.int32, (SSM_HPG * L, gw), 0) // L
    bcol = lax.broadcasted_iota(jnp.int32, (SSM_HPG * L, gw), 1) // SSM_HEAD_DIM
    blockmask = brow == bcol

    for g in range(SSM_GROUPS):
        xs_g = act_ref[:, g * gw:(g + 1) * gw]
        b_g = act_ref[:, D_INNER + g * SSM_STATE: D_INNER + (g + 1) * SSM_STATE]
        c_g = act_ref[:, D_INNER + SSM_GROUPS * SSM_STATE + g * SSM_STATE:
                      D_INNER + SSM_GROUPS * SSM_STATE + (g + 1) * SSM_STATE]
        c_bf = c_g.astype(BF16)
        cb = lax.dot_general(c_bf, b_g.astype(BF16), (((1,), (1,)), ((), ())),
                             preferred_element_type=F32)
        b_t = b_g.T
        tops, bots = [], []
        for j in range(SSM_HPG):
            h = g * SSM_HPG + j
            decay = jnp.exp(jnp.where(tri, col_b[:, h * L:(h + 1) * L] - cs_t[h:h + 1, :], NEG))
            tops.append((cb * decay * dt_t[h:h + 1, :]).astype(BF16))
            bots.append((b_t * w_t[h:h + 1, :]).astype(BF16))
        lhs = jnp.concatenate([jnp.concatenate(tops, axis=1), jnp.concatenate(bots, axis=1)], axis=0)
        xs_bf = xs_g.astype(BF16)
        xs_bd = jnp.where(blockmask, jnp.concatenate([xs_bf] * SSM_HPG, axis=0), jnp.zeros((), BF16))
        res = jnp.dot(lhs, xs_bd, preferred_element_type=F32)
        st = state_ref[g]
        y_off = jnp.dot(c_bf, st.astype(BF16), preferred_element_type=F32) * exp_e[:, g * gw:(g + 1) * gw]
        state_ref[g] = st * etot_e[:, g * gw:(g + 1) * gw] + res[L:2 * L]
        y = res[0:L] + y_off + dskip_ref[:, g * gw:(g + 1) * gw] * xs_g
        zg = z_ref[:, g * gw:(g + 1) * gw]
        y = y * _silu(zg)
        ms = jnp.mean(y * y, axis=-1, keepdims=True)
        act_ref[:, g * gw:(g + 1) * gw] = y * lax.rsqrt(ms + NORM_EPS) * gnorm_ref[:, g * gw:(g + 1) * gw]

    o_ref[...] = jnp.dot(act_ref[:, 0:D_INNER].astype(BF16), wout_ref[...], preferred_element_type=F32)


def _ssd(raw, raw_gate, conv_w, conv_b, dt_bias, a_log, d_skip, g_ssm_norm, w_ssm, bsz, s):
    t = raw.shape[0]
    nc = s // CHUNK
    blk = lambda cb: pl.BlockSpec((CHUNK, 2048), lambda b, c, cb=cb: (b * nc + c, cb))
    const = lambda shape: pl.BlockSpec(shape, lambda b, c: (0,) * len(shape))
    k = jnp.arange(LANES)
    selb = ((k[:, None] < 3 * SSM_HEADS) & ((k[:, None] % SSM_HEADS) == (jnp.arange(SSM_HEADS * LANES)[None, :] // LANES)))
    sele = ((k[:, None] < 3 * SSM_HEADS) & ((k[:, None] % SSM_HEADS) == (jnp.arange(D_INNER)[None, :] // SSM_HEAD_DIM)))
    pad = lambda v: jnp.pad(v.astype(F32), (0, LANES - SSM_HEADS)).reshape(1, LANES)
    return pl.pallas_call(
        _ssd_kernel,
        grid=(bsz, nc),
        in_specs=[blk(RAW_Z // 2048), blk(RAW_XS // 2048), blk(RAW_BC // 2048),
                  pl.BlockSpec((CHUNK, LANES), lambda b, c: (b * nc + c, GATE_MISC // LANES)),
                  const((SSM_CONV, SSM_XBC)), const((1, SSM_XBC)), const((1, LANES)), const((1, LANES)),
                  const((1, D_INNER)), const((1, D_INNER)),
                  const((LANES, SSM_HEADS * LANES)), const((LANES, D_INNER)),
                  const((D_INNER, D_MODEL))],
        out_specs=pl.BlockSpec((CHUNK, D_MODEL), lambda b, c: (b * nc + c, 0)),
        out_shape=jax.ShapeDtypeStruct((t, D_MODEL), F32),
        scratch_shapes=[pltpu.VMEM((8, SSM_XBC), F32),
                        pltpu.VMEM((CHUNK, SSM_XBC), F32),
                        pltpu.VMEM((SSM_GROUPS, SSM_STATE, SSM_HPG * SSM_HEAD_DIM), F32)],
        compiler_params=_cparams(("parallel", "arbitrary")),
        name="ssd_mixer",
    )(raw, raw, raw, raw_gate, conv_w.astype(F32), conv_b.reshape(1, SSM_XBC).astype(F32),
      pad(dt_bias), pad(a_log), jnp.repeat(d_skip.astype(F32), SSM_HEAD_DIM).reshape(1, D_INNER),
      g_ssm_norm.reshape(1, D_INNER).astype(F32), selb.astype(BF16), sele.astype(BF16), w_ssm.astype(BF16))


def _compress_kernel(u_ref, pelo_ref, pehi_ref, w1_ref, w2_ref, o_ref):
    u = u_ref[0]
    half = CMP_STRIDE * HEAD_DIM
    a = jnp.dot((u + pelo_ref[0, 0]).astype(BF16), w1_ref[0, 0:half, :], preferred_element_type=F32)
    b = jnp.dot((u + pehi_ref[0, 0]).astype(BF16), w1_ref[0, half:2 * half, :], preferred_element_type=F32)
    pre = a + pltpu.roll(b, u.shape[0] - 1, 0)
    hidden = _silu(pre)
    out = jnp.dot(hidden.astype(BF16), w2_ref[0], preferred_element_type=F32)
    lane = lax.broadcasted_iota(jnp.int32, out.shape, 1)
    o_ref[0] = jnp.where(lane == HEAD_DIM, 1.0, out).astype(o_ref.dtype)


def _compress(u, pe, w1, w2):
    n, rows, width = u.shape
    kv_of = lambda i: (i // KV_GROUPS) % 2
    return pl.pallas_call(
        _compress_kernel,
        grid=(n,),
        in_specs=[pl.BlockSpec((1, rows, width), lambda i: (i, 0, 0)),
                  pl.BlockSpec((1, 1, 1, width), lambda i: (kv_of(i), 0, 0, 0)),
                  pl.BlockSpec((1, 1, 1, width), lambda i: (kv_of(i), 1, 0, 0)),
                  pl.BlockSpec((1, 2 * width, CMP_HIDDEN), lambda i: (kv_of(i), 0, 0)),
                  pl.BlockSpec((1, CMP_HIDDEN, LANES), lambda i: (kv_of(i), 0, 0))],
        out_specs=pl.BlockSpec((1, rows, LANES), lambda i: (i, 0, 0)),
        out_shape=jax.ShapeDtypeStruct((n, rows, LANES), BF16),
        compiler_params=_cparams(("parallel",)),
        name="compress",
    )(u, pe, pe, w1, w2)


def _nsa_kernel(qn_ref, qr_ref, kc_ref, vct_ref, ks_ref, vst_ref, kw_ref, vwt_ref, gate_ref, ovt_ref, oh_ref,
                eye_ref, wlo_ref, whi_ref,
                o_ref, gt_ref, qa_ref, sa_ref, sb_ref, pa_ref, pb_ref, sc_ref, *, topk):
    gp = pl.program_id(1)
    qb = pl.program_id(2)
    groups = range(NSA_GROUPS)
    nq = ATT_HPG * Q_BLOCK
    ncmp = kc_ref.shape[1]
    nsel = ovt_ref.shape[0]
    q0 = qb * Q_BLOCK
    tq = q0 + lax.broadcasted_iota(jnp.int32, (1, nq), 1) % Q_BLOCK

    def heads_on_lanes(ref, gi):
        base = gi * ATT_HPG * HEAD_DIM
        return jnp.concatenate([ref[0, base + r * HEAD_DIM:base + (r + 1) * HEAD_DIM, :]
                                for r in range(ATT_HPG)], axis=1)

    qn_t = [heads_on_lanes(qn_ref, gi) for gi in groups]
    qr_t = [heads_on_lanes(qr_ref, gi) for gi in groups]

    def q_operand(q_t, flag, table):
        flag_rows = jnp.concatenate([flag, jnp.zeros((LANES - HEAD_DIM - 1, nq), F32)], axis=0).astype(BF16)
        return jnp.concatenate([q_t, flag_rows] + ([table] if table is not None else []), axis=0)

    no_flag = jnp.zeros((1, nq), F32)

    n_chunks = ncmp // LANES
    chunk_span = LANES * CMP_STRIDE
    for c in range(n_chunks):
        none_valid = c * chunk_span + CMP_LEN - 1 > q0 + Q_BLOCK - 1
        for gi in groups:
            sc_ref[gi, c * LANES:(c + 1) * LANES, :] = jnp.dot(
                kc_ref[gi, c * LANES:(c + 1) * LANES, :],
                q_operand(qn_t[gi], jnp.where(none_valid, NEG, no_flag), None), preferred_element_type=F32)
    c_hi = jnp.minimum((q0 + Q_BLOCK - CMP_LEN) // chunk_span, n_chunks - 1)
    for cc in (c_hi, jnp.maximum(c_hi - 1, 0)):
        rows = pl.ds(pl.multiple_of(cc * LANES, LANES), LANES)
        ends = cc * chunk_span + CMP_LEN - 1 + lax.broadcasted_iota(jnp.int32, (LANES, nq), 0) * CMP_STRIDE
        for gi in groups:
            sc_ref[gi, rows, :] = jnp.where(ends <= tq, sc_ref[gi, rows, :], NEG)
    p_c, acc_c, inv_c = [], [], []
    for gi in groups:
        s_c = sc_ref[gi]
        m_c = jnp.max(s_c, axis=0, keepdims=True)
        p_c.append(jnp.exp2(s_c - m_c))
        acc_c.append(jnp.dot(vct_ref[gi], p_c[gi].astype(BF16), preferred_element_type=F32))
        inv_c.append(jnp.where(m_c > 0.5 * NEG, 1.0 / acc_c[gi][HEAD_DIM:HEAD_DIM + 1], 0.0))

    n_wt = WINDOW // Q_BLOCK + 1
    eye = eye_ref[...]
    acc_w = []
    for gi in groups:
        s_w, v_w = [], []
        for i in range(n_wt):
            kt = qb - (n_wt - 1) + i
            k0 = pl.multiple_of(jnp.maximum(kt, 0) * Q_BLOCK, Q_BLOCK)
            keys = kw_ref[gi, pl.ds(k0, Q_BLOCK), :]
            flag = jnp.where(kt >= 0, no_flag, NEG)
            if i == 0:
                s = jnp.dot(jnp.concatenate([keys, eye], axis=1), q_operand(qr_t[gi], flag, wlo_ref[...]),
                            preferred_element_type=F32)
            elif i == n_wt - 1:
                s = jnp.dot(jnp.concatenate([keys, eye], axis=1), q_operand(qr_t[gi], flag, whi_ref[...]),
                            preferred_element_type=F32)
            else:
                s = jnp.dot(keys, q_operand(qr_t[gi], flag, None), preferred_element_type=F32)
            s_w.append(s)
            v_w.append(vwt_ref[gi, jnp.maximum(kt, 0)])
        s_w = jnp.concatenate(s_w, axis=0)
        p_w = jnp.exp2(s_w - jnp.max(s_w, axis=0, keepdims=True))
        acc_w.append(jnp.dot(jnp.concatenate(v_w, axis=1), p_w.astype(BF16), preferred_element_type=F32))

    imp = []
    for gi in groups:
        p_sum = p_c[gi][:, 0:Q_BLOCK] * inv_c[gi][:, 0:Q_BLOCK]
        for r in range(1, ATT_HPG):
            p_sum = p_sum + p_c[gi][:, r * Q_BLOCK:(r + 1) * Q_BLOCK] * inv_c[gi][:, r * Q_BLOCK:(r + 1) * Q_BLOCK]
        imp.append(jnp.dot(ovt_ref[...], jnp.concatenate(_split3(p_sum), axis=0), preferred_element_type=F32))

    blk = lax.broadcasted_iota(jnp.int32, (nsel, Q_BLOCK), 0)
    cur = (q0 + lax.broadcasted_iota(jnp.int32, (nsel, Q_BLOCK), 1)) // SEL_BLOCK
    valid = blk <= cur
    forced = valid & ((blk == 0) | (blk == cur) | (blk == cur - 1))
    val = [jnp.where(forced, REMOVED, jnp.where(valid, imp[gi], NEG)) for gi in groups]
    chosen = [jnp.where(forced, 1.0, 0.0) for gi in groups]
    blk_f = blk.astype(F32)
    for _ in range(topk - 3):
        for gi in groups:
            mx = jnp.max(val[gi], axis=0, keepdims=True)
            first = jnp.min(jnp.where(val[gi] == mx, blk_f, float(nsel)), axis=0, keepdims=True)
            hit = blk_f == first
            chosen[gi] = jnp.where(hit, 1.0, chosen[gi])
            val[gi] = jnp.where(hit, REMOVED, val[gi])

    pad_rows = [jnp.zeros((LANES - nsel, nq), F32)] if nsel < LANES else []
    for gi in groups:
        bias = jnp.concatenate([(chosen[gi] - 1.0) * BIG] * ATT_HPG, axis=1)
        qa_ref[gi] = q_operand(qr_t[gi], no_flag, jnp.concatenate([bias] + pad_rows, axis=0).astype(BF16))

    def sel_scores(gi, i):
        k0 = pl.multiple_of(i * KEY_STEP, KEY_STEP)
        lhs = jnp.concatenate([ks_ref[gi, pl.ds(k0, KEY_STEP), :], oh_ref[pl.ds(k0, KEY_STEP), :]], axis=1)
        return jnp.dot(lhs, qa_ref[gi], preferred_element_type=F32)

    row_k = lax.broadcasted_iota(jnp.int32, (KEY_STEP, nq), 0)

    def softmax_step(s_ref, p_ref, gi, m, first_key):
        s = s_ref[gi]
        if first_key is not None:
            s = jnp.where(row_k <= tq - first_key, s, NEG)
        m_new = jnp.maximum(m, jnp.max(s, axis=0, keepdims=True))
        p_ref[gi] = jnp.exp2(s - m_new).astype(BF16)
        return m_new, jnp.exp2(m - m_new)

    def pv(gi, acc, alpha, p_ref, i):
        return acc * alpha + jnp.dot(vst_ref[gi, i], p_ref[gi], preferred_element_type=F32)

    def sel_pair(j, carries, final):
        a = 2 * j
        m, acc, alpha_b = ([c[k] for c in carries] for k in range(3))
        alpha_a = [None] * NSA_GROUPS
        for gi in groups:
            sb_ref[gi] = sel_scores(gi, a + 1)
            acc[gi] = pv(gi, acc[gi], alpha_b[gi], pb_ref, jnp.maximum(a - 1, 0))
            m[gi], alpha_a[gi] = softmax_step(sa_ref, pa_ref, gi, m[gi], a * KEY_STEP if final else None)
            if not final:
                sa_ref[gi] = sel_scores(gi, a + 2)
            acc[gi] = pv(gi, acc[gi], alpha_a[gi], pa_ref, a)
            m[gi], alpha_b[gi] = softmax_step(sb_ref, pb_ref, gi, m[gi], (a + 1) * KEY_STEP if final else None)
        return tuple((m[gi], acc[gi], alpha_b[gi]) for gi in groups)

    n_pairs = (q0 + Q_BLOCK - 1) // (2 * KEY_STEP) + 1
    for gi in groups:
        sa_ref[gi] = sel_scores(gi, 0)
    pb_ref[...] = jnp.zeros_like(pb_ref)
    init = tuple((jnp.full((1, nq), NEG, F32), jnp.zeros((V_ROWS, nq), F32), jnp.ones((1, nq), F32))
                 for gi in groups)
    carries = lax.fori_loop(0, n_pairs - 1, lambda j, c: sel_pair(j, c, False), init)
    carries = sel_pair(n_pairs - 1, carries, True)
    acc_s = [pv(gi, carries[gi][1], carries[gi][2], pb_ref, 2 * n_pairs - 1) for gi in groups]

    gt_ref[...] = _sigmoid(gate_ref[...]).T
    hd = HEAD_DIM
    for gi in groups:
        def gate_row(branch, gi=gi):
            base = MISC_GATE0 + branch * ATT_HEADS + (gp * NSA_GROUPS + gi) * ATT_HPG
            return jnp.concatenate([gt_ref[pl.ds(base + r, 1), :] for r in range(ATT_HPG)], axis=1)

        o = ((gate_row(0) * inv_c[gi]) * acc_c[gi][0:hd] + (gate_row(1) / acc_s[gi][hd:hd + 1]) * acc_s[gi][0:hd]
             + (gate_row(2) / acc_w[gi][hd:hd + 1]) * acc_w[gi][0:hd])
        for r in range(ATT_HPG):
            row0 = (gi * ATT_HPG + r) * HEAD_DIM
            o_ref[0, row0:row0 + HEAD_DIM, :] = o[:, r * Q_BLOCK:(r + 1) * Q_BLOCK].astype(o_ref.dtype)


def _nsa(qn_t, qr_t, kc, vc_t, ks, vs_t, kw, vw_t, raw_gate, bsz, s):
    nqb = s // Q_BLOCK
    ncmp = s // CMP_STRIDE
    nsel = s // SEL_BLOCK
    topk = min(SEL_TOPK, nsel)
    gw = ATT_HPG * HEAD_DIM
    ci = jnp.arange(ncmp)[None, :]
    sj = jnp.arange(nsel)[:, None]
    ov_t = (ci * CMP_STRIDE < (sj + 1) * SEL_BLOCK) & (ci * CMP_STRIDE + CMP_LEN > sj * SEL_BLOCK) \
        & (ci < ncmp - 1)
    ov_t3 = jnp.concatenate([ov_t.astype(BF16)] * 3, axis=1)
    assert nsel <= LANES
    onehot = (jnp.arange(s)[:, None] // SEL_BLOCK == jnp.arange(LANES)[None, :]).astype(BF16)
    eye = jnp.eye(Q_BLOCK, dtype=BF16)
    key_row = jnp.arange(Q_BLOCK)[:, None]
    q_lane = jnp.arange(ATT_HPG * Q_BLOCK)[None, :] % Q_BLOCK
    win_lo = jnp.where(key_row > q_lane, 0.0, NEG).astype(BF16)
    win_hi = jnp.where(key_row <= q_lane, 0.0, NEG).astype(BF16)
    once = pl.Buffered(1)
    const2 = lambda shape: pl.BlockSpec(shape, lambda b, g, i: (0, 0), pipeline_mode=once)
    ng = NSA_GROUPS
    gpb = KV_GROUPS // ng
    q_spec = pl.BlockSpec((1, ng * gw, Q_BLOCK), lambda b, g, i: (b, g, i))
    per_bg = lambda shape: pl.BlockSpec((ng,) + shape, lambda b, g, i: (b * gpb + g,) + (0,) * len(shape),
                                        pipeline_mode=once)
    nql = ATT_HPG * Q_BLOCK
    return pl.pallas_call(
        functools.partial(_nsa_kernel, topk=topk),
        grid=(bsz, gpb, nqb),
        in_specs=[q_spec, q_spec,
                  per_bg((ncmp, LANES)), per_bg((V_ROWS, ncmp)),
                  per_bg((s, LANES)), per_bg((s // KEY_STEP, V_ROWS, KEY_STEP)),
                  per_bg((s, LANES)), per_bg((s // Q_BLOCK, V_ROWS, Q_BLOCK)),
                  pl.BlockSpec((Q_BLOCK, LANES), lambda b, g, i: (b * nqb + i, GATE_MISC // LANES)),
                  const2((nsel, 3 * ncmp)), const2((s, LANES)),
                  const2((Q_BLOCK, Q_BLOCK)), const2((Q_BLOCK, ATT_HPG * Q_BLOCK)),
                  const2((Q_BLOCK, ATT_HPG * Q_BLOCK))],
        out_specs=pl.BlockSpec((1, ng * gw, Q_BLOCK), lambda b, g, i: (b, g, i)),
        out_shape=jax.ShapeDtypeStruct((bsz, ATT_HEADS * HEAD_DIM, s), BF16),
        scratch_shapes=[pltpu.VMEM((LANES, Q_BLOCK), F32),
                        pltpu.VMEM((ng, 2 * LANES, nql), BF16),
                        pltpu.VMEM((ng, KEY_STEP, nql), F32),
                        pltpu.VMEM((ng, KEY_STEP, nql), F32),
                        pltpu.VMEM((ng, KEY_STEP, nql), BF16),
                        pltpu.VMEM((ng, KEY_STEP, nql), BF16),
                        pltpu.VMEM((ng, ncmp, nql), F32)],
        compiler_params=_cparams(("parallel", "parallel", "arbitrary")),
        name="nsa_sweep",
    )(qn_t, qr_t, kc, vc_t, ks, vs_t, kw, vw_t, raw_gate, ov_t3, onehot, eye, win_lo, win_hi)


def _merge_mlp_kernel(ot_ref, yssm_ref, mix_ref, x_ref, wab_ref, wo_ref, g1_ref, wup_ref, wdn_ref, g2_ref, o_ref):
    o = ot_ref[0].astype(F32).T.astype(BF16)
    y_att = jnp.dot(o, wab_ref[...], preferred_element_type=F32)
    gates = _sigmoid(mix_ref[...])
    mixed = gates[:, 0:D_MODEL] * yssm_ref[...] + gates[:, D_MODEL:2 * D_MODEL] * y_att
    x = x_ref[...] + jnp.dot(mixed.astype(BF16), wo_ref[...], preferred_element_type=F32)
    ms = jnp.mean(x * x, axis=-1, keepdims=True)
    h = (x * lax.rsqrt(ms + NORM_EPS) * g1_ref[...]).astype(BF16)
    up = jnp.maximum(jnp.dot(h, wup_ref[...], preferred_element_type=F32), 0.0)
    y = x + jnp.dot((up * up).astype(BF16), wdn_ref[...], preferred_element_type=F32)
    ms2 = jnp.mean(y * y, axis=-1, keepdims=True)
    o_ref[...] = y * lax.rsqrt(ms2 + NORM_EPS) * g2_ref[...]


def _merge_mlp(o_t, y_ssm, raw_gate, x2d, w_ab, w_o, g1, w_up, w_down, g2, s):
    t = x2d.shape[0]
    tm = ROW_TILE // 2
    spb = s // tm
    row_spec = pl.BlockSpec((tm, D_MODEL), lambda i: (i, 0))
    const = lambda shape: pl.BlockSpec(shape, lambda i: (0, 0), pipeline_mode=pl.Buffered(1))
    return pl.pallas_call(
        _merge_mlp_kernel,
        grid=(t // tm,),
        in_specs=[pl.BlockSpec((1, D_MODEL, tm), lambda i: (i // spb, 0, i % spb)),
                  row_spec,
                  pl.BlockSpec((tm, 2 * D_MODEL), lambda i: (i, GATE_MIX // (2 * D_MODEL))),
                  row_spec, const((D_MODEL, D_MODEL)), const((D_MODEL, D_MODEL)),
                  const((1, D_MODEL)), const((D_MODEL, MLP_HIDDEN)), const((MLP_HIDDEN, D_MODEL)),
                  const((1, D_MODEL))],
        out_specs=row_spec,
        out_shape=jax.ShapeDtypeStruct((t, D_MODEL), F32),
        compiler_params=_cparams(("parallel",)),
        name="merge_mlp",
    )(o_t, y_ssm, raw_gate, x2d, w_ab, w_o, g1.reshape(1, D_MODEL), w_up, w_down, g2.reshape(1, D_MODEL))


def _rope_tables(positions):
    inv_freq = ROPE_THETA ** (-jnp.arange(0, ROPE_DIM, 2, dtype=F32) / ROPE_DIM)
    ang = positions.astype(F32).reshape(1, -1) * inv_freq[:, None]
    return jnp.cos(ang), jnp.sin(ang)


def kernel(x, positions, g_norm_mix, w_in, conv_w, conv_b, dt_bias, a_log, d_skip, g_ssm_norm, w_ssm_branch, cmp_pe_k, cmp_pe_v, w_cmp_k1, w_cmp_k2, w_cmp_v1, w_cmp_v2, w_attn_branch, w_o, g_norm_mlp, w_up, w_down, g_norm_final):
    bsz, s, d = x.shape
    assert d == D_MODEL and s % ROW_TILE == 0 and w_in.shape[0] == 1
    t = bsz * s
    x2d = x.reshape(t, d)
    tabs = _rope_tables(positions)

    w = w_in[0].astype(BF16)
    o_dt, o_q, o_kv, o_ag, o_mg = 6144, 6176, 7200, 8736, 8784
    misc_pad = jnp.zeros((d, GATE_COLS - GATE_MISC - SSM_HEADS - 3 * ATT_HEADS), BF16)
    w_gate = jnp.concatenate([w[:, o_mg:], w[:, o_dt:o_q], w[:, o_ag:o_mg], misc_pad], axis=1)
    w_q = w[:, o_q:o_kv]
    w_kv = w[:, o_kv:o_ag]

    h = _rmsnorm(x2d, g_norm_mix[0])
    raw = _proj_raw(h, w, SSM_COLS, SSM_TN, "proj_ssm")
    raw_gate = _proj_raw(h, w_gate, GATE_COLS, GATE_COLS, "proj_gate")
    qn_t, qr_t = _proj_q(h, w_q, tabs, bsz, s)
    kvc, ks, kw, vs_t, vw_t = _proj_kv(h, w_kv, tabs, bsz, s)

    y_ssm = _ssd(raw, raw_gate, conv_w[0], conv_b[0], dt_bias[0], a_log[0], d_skip[0], g_ssm_norm[0],
                 w_ssm_branch[0], bsz, s)

    rows = s // CMP_STRIDE
    u = kvc.reshape(bsz * 2 * KV_GROUPS, rows, CMP_STRIDE * HEAD_DIM)
    pe = jnp.stack([cmp_pe_k[0], cmp_pe_v[0]]).astype(F32).reshape(2, 2, 1, CMP_STRIDE * HEAD_DIM)
    w1 = jnp.stack([w_cmp_k1[0], w_cmp_v1[0]]).astype(BF16)
    w2 = jnp.pad(jnp.stack([w_cmp_k2[0], w_cmp_v2[0]]).astype(BF16), ((0, 0), (0, 0), (0, LANES - HEAD_DIM)))
    cmp = _compress(u, pe, w1, w2).reshape(bsz, 2, KV_GROUPS, rows, LANES)
    kc = cmp[:, 0].reshape(bsz * KV_GROUPS, rows, LANES)
    vc_t = jnp.swapaxes(cmp[:, 1], -1, -2)[:, :, :V_ROWS].reshape(bsz * KV_GROUPS, V_ROWS, rows)

    flat = lambda a: a.reshape((bsz * KV_GROUPS,) + a.shape[2:])
    o_t = _nsa(qn_t, qr_t, kc, vc_t, flat(ks), flat(vs_t), flat(kw), flat(vw_t), raw_gate, bsz, s)

    out = _merge_mlp(o_t, y_ssm, raw_gate, x2d, w_attn_branch[0].astype(BF16), w_o[0].astype(BF16),
                     g_norm_mlp[0], w_up[0].astype(BF16), w_down[0].astype(BF16), g_norm_final, s)
    return out.reshape(bsz, s, d)
```

```python
import functools

import jax
import jax.numpy as jnp
from jax import lax
from jax.experimental import pallas as pl
from jax.experimental.pallas import tpu as pltpu

F32 = jnp.float32
BF16 = jnp.bfloat16

D_MODEL = 1024
D_INNER = 2048
SSM_HEADS = 32
SSM_GROUPS = 8
SSM_HPG = 4
SSM_HEAD_DIM = 64
SSM_STATE = 128
SSM_CONV = 4
CHUNK = 128
SSM_XBC = 4096
HEAD_DIM = 64
ATT_HEADS = 16
KV_GROUPS = 4
ATT_HPG = 4
CMP_LEN = 32
CMP_STRIDE = 16
CMP_HIDDEN = 256
SEL_BLOCK = 64
SEL_TOPK = 16
WINDOW = 512
Q_BLOCK = 128
ROPE_THETA = 500000.0
ROPE_DIM = 16
MLP_HIDDEN = 4096
NORM_EPS = 1e-6
NEG = -1e30
BIG = 1e30
REMOVED = -3e38
LOG2E = 1.4426950408889634
SCALE = HEAD_DIM ** -0.5 * LOG2E
KEY_STEP = 256
NSA_GROUPS = 4

LANES = 128
ROW_TILE = 512
VMEM_LIMIT = 56 * 1024 * 1024

RAW_Z, RAW_XS, RAW_BC = 0, 2048, 4096
SSM_COLS, SSM_TN = 6144, 1536
GATE_MIX, GATE_MISC, GATE_COLS = 0, 2048, 2176
MISC_GATE0 = SSM_HEADS
V_ROWS = HEAD_DIM + 16


def _cparams(sem):
    return pltpu.CompilerParams(dimension_semantics=sem, vmem_limit_bytes=VMEM_LIMIT)


def _sigmoid(x):
    return 0.5 * jnp.tanh(0.5 * x) + 0.5


def _silu(x):
    h = 0.5 * x
    return h + h * jnp.tanh(h)


def _split3(x):
    hi = x.astype(BF16)
    r1 = x - hi.astype(F32)
    mid = r1.astype(BF16)
    lo = (r1 - mid.astype(F32)).astype(BF16)
    return hi, mid, lo


def _rmsnorm_kernel(x_ref, g_ref, o_ref):
    x = x_ref[...]
    ms = jnp.mean(x * x, axis=-1, keepdims=True)
    o_ref[...] = (x * lax.rsqrt(ms + NORM_EPS) * g_ref[...]).astype(o_ref.dtype)


def _rmsnorm(x2d, g):
    t, d = x2d.shape
    return pl.pallas_call(
        _rmsnorm_kernel,
        grid=(t // ROW_TILE,),
        in_specs=[pl.BlockSpec((ROW_TILE, d), lambda i: (i, 0)),
                  pl.BlockSpec((1, d), lambda i: (0, 0))],
        out_specs=pl.BlockSpec((ROW_TILE, d), lambda i: (i, 0)),
        out_shape=jax.ShapeDtypeStruct((t, d), BF16),
        compiler_params=_cparams(("parallel",)),
        name="rmsnorm",
    )(x2d, g.reshape(1, d))


def _proj_raw_kernel(a_ref, w_ref, o_ref):
    o_ref[...] = jnp.dot(a_ref[...], w_ref[...], preferred_element_type=F32)


def _proj_raw(h, w, n, tn, tm, name):
    t, k = h.shape
    return pl.pallas_call(
        _proj_raw_kernel,
        grid=(n // tn, t // tm),
        in_specs=[pl.BlockSpec((tm, k), lambda j, i: (i, 0)),
                  pl.BlockSpec((k, tn), lambda j, i: (0, j))],
        out_specs=pl.BlockSpec((tm, tn), lambda j, i: (i, j)),
        out_shape=jax.ShapeDtypeStruct((t, n), F32),
        compiler_params=_cparams(("parallel", "parallel")),
        name=name,
    )(h, w)


def _rope_t(t_t, cos_t, sin_t):
    half = ROPE_DIM // 2
    rows = []
    for base in range(0, t_t.shape[0], HEAD_DIM):
        t1, t2 = t_t[base:base + half], t_t[base + half:base + ROPE_DIM]
        rows += [t1 * cos_t - t2 * sin_t, t2 * cos_t + t1 * sin_t, t_t[base + ROPE_DIM:base + HEAD_DIM]]
    return jnp.concatenate(rows, axis=0)


def _proj_q_kernel(a_ref, w_ref, cos_ref, sin_ref, qn_ref, qr_ref):
    acc = jnp.dot(a_ref[...], w_ref[...], preferred_element_type=F32)
    cos_t, sin_t = cos_ref[...], sin_ref[...]
    for c in range(acc.shape[1] // LANES):
        t_t = acc[:, c * LANES:(c + 1) * LANES].T
        qn_ref[0, c * LANES:(c + 1) * LANES, :] = (t_t * SCALE).astype(qn_ref.dtype)
        qr_ref[0, c * LANES:(c + 1) * LANES, :] = (_rope_t(t_t, cos_t, sin_t) * SCALE).astype(qr_ref.dtype)


def _proj_q(h, w, tabs, bsz, s):
    t, k = h.shape
    n = w.shape[1]
    spb = s // ROW_TILE
    tab_spec = pl.BlockSpec((ROPE_DIM // 2, ROW_TILE), lambda i: (0, i))
    out_spec = pl.BlockSpec((1, n, ROW_TILE), lambda i: (i // spb, 0, i % spb))
    return pl.pallas_call(
        _proj_q_kernel,
        grid=(t // ROW_TILE,),
        in_specs=[pl.BlockSpec((ROW_TILE, k), lambda i: (i, 0)),
                  pl.BlockSpec((k, n), lambda i: (0, 0)),
                  tab_spec, tab_spec],
        out_specs=[out_spec, out_spec],
        out_shape=[jax.ShapeDtypeStruct((bsz, n, s), BF16)] * 2,
        compiler_params=_cparams(("parallel",)),
        name="proj_q",
    )(h, w, *tabs)


def _proj_kv_kernel(a_ref, w_ref, cos_ref, sin_ref, kvc_ref, ks_ref, kw_ref, vst_ref, vwt_ref, kvc_scr):
    acc = jnp.dot(a_ref[...], w_ref[...], preferred_element_type=F32)
    cos_t, sin_t = cos_ref[...], sin_ref[...]
    tm = acc.shape[0]
    gw = KV_GROUPS * HEAD_DIM
    rows = tm // CMP_STRIDE
    low = lax.broadcasted_iota(jnp.int32, (rows, LANES), 1) < HEAD_DIM
    for c in range(2 * gw // LANES):
        kvc_scr[c] = acc[:, c * LANES:(c + 1) * LANES]
        for j in range(CMP_STRIDE // 2):
            e = kvc_scr[c, pl.ds(2 * j, rows, stride=CMP_STRIDE), :]
            o = kvc_scr[c, pl.ds(2 * j + 1, rows, stride=CMP_STRIDE), :]
            kvc_ref[0, 2 * c, :, j * LANES:(j + 1) * LANES] = jnp.where(low, e, pltpu.roll(o, HEAD_DIM, 1))
            kvc_ref[0, 2 * c + 1, :, j * LANES:(j + 1) * LANES] = jnp.where(low, pltpu.roll(e, HEAD_DIM, 1), o)
    lane = lax.broadcasted_iota(jnp.int32, (tm, LANES), 1)
    key_pad = jnp.where(lane == HEAD_DIM, 1.0, 0.0)
    ones_rows = jnp.where(lax.broadcasted_iota(jnp.int32, (V_ROWS - HEAD_DIM, tm), 0) == 0, 1.0, 0.0)
    for base, k_ref, vt_ref in ((2 * gw, ks_ref, vst_ref), (4 * gw, kw_ref, vwt_ref)):
        vtile = vt_ref.shape[-1]
        for c in range(gw // LANES):
            kk = _rope_t(acc[:, base + c * LANES: base + (c + 1) * LANES].T, cos_t, sin_t).T
            vv = acc[:, base + gw + c * LANES: base + gw + (c + 1) * LANES].T
            for half in range(2):
                g = 2 * c + half
                k_lo = kk if half == 0 else pltpu.roll(kk, HEAD_DIM, 1)
                k_ref[0, g] = jnp.where(lane < HEAD_DIM, k_lo, key_pad).astype(k_ref.dtype)
                v_aug = jnp.concatenate([vv[half * HEAD_DIM:(half + 1) * HEAD_DIM], ones_rows], axis=0)
                for j in range(ROW_TILE // vtile):
                    vt_ref[0, g, j] = v_aug[:, j * vtile:(j + 1) * vtile].astype(vt_ref.dtype)


def _proj_kv(h, w, tabs, bsz, s):
    t, k = h.shape
    n = w.shape[1]
    spb = s // ROW_TILE
    tab_spec = pl.BlockSpec((ROPE_DIM // 2, ROW_TILE), lambda i: (0, i))
    k_spec = pl.BlockSpec((1, KV_GROUPS, ROW_TILE, LANES), lambda i: (i // spb, 0, i % spb, 0))
    k_shape = jax.ShapeDtypeStruct((bsz, KV_GROUPS, s, LANES), BF16)

    def vt(tile):
        return (pl.BlockSpec((1, KV_GROUPS, ROW_TILE // tile, V_ROWS, tile), lambda i: (i // spb, 0, i % spb, 0, 0)),
                jax.ShapeDtypeStruct((bsz, KV_GROUPS, s // tile, V_ROWS, tile), BF16))

    (vs_spec, vs_shape), (vw_spec, vw_shape) = vt(KEY_STEP), vt(Q_BLOCK)
    return pl.pallas_call(
        _proj_kv_kernel,
        grid=(t // ROW_TILE,),
        in_specs=[pl.BlockSpec((ROW_TILE, k), lambda i: (i, 0)),
                  pl.BlockSpec((k, n), lambda i: (0, 0)),
                  tab_spec, tab_spec],
        out_specs=[pl.BlockSpec((1, 2 * KV_GROUPS, ROW_TILE // CMP_STRIDE, CMP_STRIDE * HEAD_DIM),
                                lambda i: (i // spb, 0, i % spb, 0)),
                   k_spec, k_spec, vs_spec, vw_spec],
        out_shape=[jax.ShapeDtypeStruct((bsz, 2 * KV_GROUPS, s // CMP_STRIDE, CMP_STRIDE * HEAD_DIM), F32),
                   k_shape, k_shape, vs_shape, vw_shape],
        scratch_shapes=[pltpu.VMEM((2 * KV_GROUPS * HEAD_DIM // LANES, ROW_TILE, LANES), F32)],
        compiler_params=_cparams(("parallel",)),
        name="proj_kv",
    )(h, w, *tabs)


def _ssd_kernel(z_ref, xs_ref, bc_ref, misc_ref, convw_ref, convb_ref, dtb_ref, alog_ref,
                dskip_ref, gnorm_ref, selb_ref, sele_ref, wout_ref, o_ref,
                tail_ref, act_ref, state_ref):
    c = pl.program_id(1)
    L = CHUNK
    gw = SSM_HPG * SSM_HEAD_DIM

    @pl.when(c == 0)
    def _():
        state_ref[...] = jnp.zeros_like(state_ref)
        tail_ref[...] = jnp.zeros_like(tail_ref)

    cw = 512
    first_row = lax.broadcasted_iota(jnp.int32, (8, cw), 0) == 0
    for cc in range(SSM_XBC // cw):
        cols = slice(cc * cw, (cc + 1) * cw)
        u = xs_ref[:, cols] if cc < D_INNER // cw else bc_ref[:, cc * cw - D_INNER:(cc + 1) * cw - D_INNER]
        a = convw_ref[0:1, cols] * u
        for k in range(1, SSM_CONV):
            shifted = pltpu.roll(a, 1, 0)
            shifted = jnp.concatenate([jnp.where(first_row, tail_ref[k - 1:k, cols], shifted[0:8]), shifted[8:]], axis=0)
            tail_ref[k - 1:k, cols] = a[L - 1:L]
            a = convw_ref[k:k + 1, cols] * u + shifted
        act_ref[:, cols] = _silu(a + convb_ref[:, cols])

    lane = lax.broadcasted_iota(jnp.int32, (L, LANES), 1)
    row = lax.broadcasted_iota(jnp.int32, (L, LANES), 0)
    head_lane = lane < SSM_HEADS
    raw = misc_ref[...] + dtb_ref[...]
    dt = jnp.where(head_lane, jnp.maximum(raw, 0.0) + jnp.log1p(jnp.exp(-jnp.abs(raw))), 0.0)
    a_row = jnp.where(head_lane[0:1], -jnp.exp(alog_ref[...]), 0.0)
    cs = dt * a_row
    sh = 1
    while sh < L:
        cs = cs + jnp.where(row >= sh, pltpu.roll(cs, sh, 0), 0.0)
        sh *= 2
    tot = cs[L - 1:L, :]
    w_state = dt * jnp.exp(tot - cs)
    cs_t, dt_t, w_t = cs.T, dt.T, w_state.T

    hi, mid, lo = _split3(cs)
    packed = (hi.astype(F32) + pltpu.roll(mid.astype(F32), SSM_HEADS, 1)
              + pltpu.roll(lo.astype(F32), 2 * SSM_HEADS, 1)).astype(BF16)

    tri = lax.broadcasted_iota(jnp.int32, (L, L), 0) >= lax.broadcasted_iota(jnp.int32, (L, L), 1)
    brow = lax.broadcasted_iota(jnp.int32, (SSM_HPG * L, gw), 0) // L
    bcol = lax.broadcasted_iota(jnp.int32, (SSM_HPG * L, gw), 1) // SSM_HEAD_DIM
    blockmask = brow == bcol

    for g in range(SSM_GROUPS):
        col_b = jnp.dot(packed, selb_ref[:, g * SSM_HPG * L:(g + 1) * SSM_HPG * L], preferred_element_type=F32)
        exp_e = jnp.exp(jnp.dot(packed, sele_ref[:, g * gw:(g + 1) * gw], preferred_element_type=F32))
        etot_e = exp_e[L - 1:L, :]
        xs_g = act_ref[:, g * gw:(g + 1) * gw]
        b_g = act_ref[:, D_INNER + g * SSM_STATE: D_INNER + (g + 1) * SSM_STATE]
        c_g = act_ref[:, D_INNER + SSM_GROUPS * SSM_STATE + g * SSM_STATE:
                      D_INNER + SSM_GROUPS * SSM_STATE + (g + 1) * SSM_STATE]
        c_bf = c_g.astype(BF16)
        cb = lax.dot_general(c_bf, b_g.astype(BF16), (((1,), (1,)), ((), ())),
                             preferred_element_type=F32)
        b_t = b_g.T
        tops, bots = [], []
        for j in range(SSM_HPG):
            h = g * SSM_HPG + j
            decay = jnp.exp(jnp.where(tri, col_b[:, j * L:(j + 1) * L] - cs_t[h:h + 1, :], NEG))
            tops.append((cb * decay * dt_t[h:h + 1, :]).astype(BF16))
            bots.append((b_t * w_t[h:h + 1, :]).astype(BF16))
        lhs = jnp.concatenate([jnp.concatenate(tops, axis=1), jnp.concatenate(bots, axis=1)], axis=0)
        xs_bf = xs_g.astype(BF16)
        xs_bd = jnp.where(blockmask, jnp.concatenate([xs_bf] * SSM_HPG, axis=0), jnp.zeros((), BF16))
        res = jnp.dot(lhs, xs_bd, preferred_element_type=F32)
        st = state_ref[g]
        y_off = jnp.dot(c_bf, st.astype(BF16), preferred_element_type=F32) * exp_e
        state_ref[g] = st * etot_e + res[L:2 * L]
        y = res[0:L] + y_off + dskip_ref[:, g * gw:(g + 1) * gw] * xs_g
        zg = z_ref[:, g * gw:(g + 1) * gw]
        y = y * _silu(zg)
        ms = jnp.mean(y * y, axis=-1, keepdims=True)
        act_ref[:, g * gw:(g + 1) * gw] = y * lax.rsqrt(ms + NORM_EPS) * gnorm_ref[:, g * gw:(g + 1) * gw]

    o_ref[...] = jnp.dot(act_ref[:, 0:D_INNER].astype(BF16), wout_ref[...], preferred_element_type=F32)


def _ssd(raw, raw_gate, conv_w, conv_b, dt_bias, a_log, d_skip, g_ssm_norm, w_ssm, bsz, s):
    t = raw.shape[0]
    nc = s // CHUNK
    blk = lambda cb: pl.BlockSpec((CHUNK, 2048), lambda b, c, cb=cb: (b * nc + c, cb))
    const = lambda shape: pl.BlockSpec(shape, lambda b, c: (0,) * len(shape))
    k = jnp.arange(LANES)
    selb = ((k[:, None] < 3 * SSM_HEADS) & ((k[:, None] % SSM_HEADS) == (jnp.arange(SSM_HEADS * LANES)[None, :] // LANES)))
    sele = ((k[:, None] < 3 * SSM_HEADS) & ((k[:, None] % SSM_HEADS) == (jnp.arange(D_INNER)[None, :] // SSM_HEAD_DIM)))
    pad = lambda v: jnp.pad(v.astype(F32), (0, LANES - SSM_HEADS)).reshape(1, LANES)
    return pl.pallas_call(
        _ssd_kernel,
        grid=(bsz, nc),
        in_specs=[blk(RAW_Z // 2048), blk(RAW_XS // 2048), blk(RAW_BC // 2048),
                  pl.BlockSpec((CHUNK, LANES), lambda b, c: (b * nc + c, GATE_MISC // LANES)),
                  const((SSM_CONV, SSM_XBC)), const((1, SSM_XBC)), const((1, LANES)), const((1, LANES)),
                  const((1, D_INNER)), const((1, D_INNER)),
                  const((LANES, SSM_HEADS * LANES)), const((LANES, D_INNER)),
                  const((D_INNER, D_MODEL))],
        out_specs=pl.BlockSpec((CHUNK, D_MODEL), lambda b, c: (b * nc + c, 0)),
        out_shape=jax.ShapeDtypeStruct((t, D_MODEL), F32),
        scratch_shapes=[pltpu.VMEM((8, SSM_XBC), F32),
                        pltpu.VMEM((CHUNK, SSM_XBC), F32),
                        pltpu.VMEM((SSM_GROUPS, SSM_STATE, SSM_HPG * SSM_HEAD_DIM), F32)],
        compiler_params=_cparams(("parallel", "arbitrary")),
        name="ssd_mixer",
    )(raw, raw, raw, raw_gate, conv_w.astype(F32), conv_b.reshape(1, SSM_XBC).astype(F32),
      pad(dt_bias), pad(a_log), jnp.repeat(d_skip.astype(F32), SSM_HEAD_DIM).reshape(1, D_INNER),
      g_ssm_norm.reshape(1, D_INNER).astype(F32), selb.astype(BF16), sele.astype(BF16), w_ssm.astype(BF16))


def _compress_kernel(u_ref, pelo_ref, pehi_ref, w1_ref, w2_ref, o_ref):
    u = u_ref[0]
    half = CMP_STRIDE * HEAD_DIM
    a = jnp.dot((u + pelo_ref[0, 0]).astype(BF16), w1_ref[0, 0:half, :], preferred_element_type=F32)
    b = jnp.dot((u + pehi_ref[0, 0]).astype(BF16), w1_ref[0, half:2 * half, :], preferred_element_type=F32)
    pre = a + pltpu.roll(b, u.shape[0] - 1, 0)
    hidden = _silu(pre)
    out = jnp.dot(hidden.astype(BF16), w2_ref[0], preferred_element_type=F32)
    lane = lax.broadcasted_iota(jnp.int32, out.shape, 1)
    o_ref[0] = jnp.where(lane == HEAD_DIM, 1.0, out).astype(o_ref.dtype)


def _compress(u, pe, w1, w2):
    n, rows, width = u.shape
    kv_of = lambda i: (i // KV_GROUPS) % 2
    return pl.pallas_call(
        _compress_kernel,
        grid=(n,),
        in_specs=[pl.BlockSpec((1, rows, width), lambda i: (i, 0, 0)),
                  pl.BlockSpec((1, 1, 1, width), lambda i: (kv_of(i), 0, 0, 0)),
                  pl.BlockSpec((1, 1, 1, width), lambda i: (kv_of(i), 1, 0, 0)),
                  pl.BlockSpec((1, 2 * width, CMP_HIDDEN), lambda i: (kv_of(i), 0, 0)),
                  pl.BlockSpec((1, CMP_HIDDEN, LANES), lambda i: (kv_of(i), 0, 0))],
        out_specs=pl.BlockSpec((1, rows, LANES), lambda i: (i, 0, 0)),
        out_shape=jax.ShapeDtypeStruct((n, rows, LANES), BF16),
        compiler_params=_cparams(("parallel",)),
        name="compress",
    )(u, pe, pe, w1, w2)


def _nsa_kernel(qn_ref, qr_ref, kc_ref, vct_ref, ks_ref, vst_ref, kw_ref, vwt_ref, gate_ref, ovt_ref, oh_ref,
                eye_ref, wlo_ref, whi_ref,
                o_ref, gt_ref, qa_ref, sa_ref, sb_ref, pa_ref, pb_ref, sc_ref, accc_ref, invc_ref, imp_ref,
                *, topk):
    gp = pl.program_id(1)
    qb = pl.program_id(2)
    groups = range(NSA_GROUPS)
    nq = ATT_HPG * Q_BLOCK
    ncmp = kc_ref.shape[1]
    nsel = ovt_ref.shape[0]
    q0 = qb * Q_BLOCK
    tq = q0 + lax.broadcasted_iota(jnp.int32, (1, nq), 1) % Q_BLOCK

    def heads_on_lanes(ref, gi):
        base = gi * ATT_HPG * HEAD_DIM
        return jnp.concatenate([ref[0, base + r * HEAD_DIM:base + (r + 1) * HEAD_DIM, :]
                                for r in range(ATT_HPG)], axis=1)

    qn_t = [heads_on_lanes(qn_ref, gi) for gi in groups]
    qr_t = [heads_on_lanes(qr_ref, gi) for gi in groups]

    def q_operand(q_t, flag, table):
        flag_rows = jnp.concatenate([flag, jnp.zeros((LANES - HEAD_DIM - 1, nq), F32)], axis=0).astype(BF16)
        return jnp.concatenate([q_t, flag_rows] + ([table] if table is not None else []), axis=0)

    no_flag = jnp.zeros((1, nq), F32)

    n_chunks = ncmp // LANES
    chunk_span = LANES * CMP_STRIDE
    c_hi = jnp.minimum((q0 + Q_BLOCK - CMP_LEN) // chunk_span, n_chunks - 1)

    def compressed_branch(n_live):
        live = n_live * LANES
        ov_live = jnp.concatenate([ovt_ref[:, piece * ncmp:piece * ncmp + live] for piece in range(3)], axis=1)
        for c in range(n_live):
            for gi in groups:
                s = jnp.dot(kc_ref[gi, c * LANES:(c + 1) * LANES, :], q_operand(qn_t[gi], no_flag, None),
                            preferred_element_type=F32)
                if c >= n_live - 2:
                    ends = (c * chunk_span + CMP_LEN - 1
                            + lax.broadcasted_iota(jnp.int32, (LANES, nq), 0) * CMP_STRIDE)
                    s = jnp.where(ends <= tq, s, NEG)
                sc_ref[gi, c * LANES:(c + 1) * LANES, :] = s
        for gi in groups:
            s_c = sc_ref[gi, 0:live, :]
            m_c = jnp.max(s_c, axis=0, keepdims=True)
            p_c = jnp.exp2(s_c - m_c)
            acc = jnp.dot(vct_ref[gi, :, 0:live], p_c.astype(BF16), preferred_element_type=F32)
            inv = jnp.where(m_c > 0.5 * NEG, 1.0 / acc[HEAD_DIM:HEAD_DIM + 1], 0.0)
            accc_ref[gi] = acc
            invc_ref[gi] = jnp.broadcast_to(inv, (8, nq))
            p_sum = p_c[:, 0:Q_BLOCK] * inv[:, 0:Q_BLOCK]
            for r in range(1, ATT_HPG):
                p_sum = p_sum + p_c[:, r * Q_BLOCK:(r + 1) * Q_BLOCK] * inv[:, r * Q_BLOCK:(r + 1) * Q_BLOCK]
            imp_ref[gi] = jnp.dot(ov_live, jnp.concatenate(_split3(p_sum), axis=0), preferred_element_type=F32)

    for k in range(n_chunks):
        pl.when(c_hi == k)(functools.partial(compressed_branch, k + 1))
    acc_c = [accc_ref[gi] for gi in groups]
    inv_c = [invc_ref[gi, 0:1, :] for gi in groups]
    imp = [imp_ref[gi] for gi in groups]

    n_wt = WINDOW // Q_BLOCK + 1
    eye = eye_ref[...]
    acc_w = []
    for gi in groups:
        s_w, v_w = [], []
        for i in range(n_wt):
            kt = qb - (n_wt - 1) + i
            k0 = pl.multiple_of(jnp.maximum(kt, 0) * Q_BLOCK, Q_BLOCK)
            keys = kw_ref[gi, pl.ds(k0, Q_BLOCK), :]
            flag = jnp.where(kt >= 0, no_flag, NEG)
            if i == 0:
                s = jnp.dot(jnp.concatenate([keys, eye], axis=1), q_operand(qr_t[gi], flag, wlo_ref[...]),
                            preferred_element_type=F32)
            elif i == n_wt - 1:
                s = jnp.dot(jnp.concatenate([keys, eye], axis=1), q_operand(qr_t[gi], flag, whi_ref[...]),
                            preferred_element_type=F32)
            else:
                s = jnp.dot(keys, q_operand(qr_t[gi], flag, None), preferred_element_type=F32)
            s_w.append(s)
            v_w.append(vwt_ref[gi, jnp.maximum(kt, 0)])
        s_w = jnp.concatenate(s_w, axis=0)
        p_w = jnp.exp2(s_w - jnp.max(s_w, axis=0, keepdims=True))
        acc_w.append(jnp.dot(jnp.concatenate(v_w, axis=1), p_w.astype(BF16), preferred_element_type=F32))

    blk = lax.broadcasted_iota(jnp.int32, (nsel, Q_BLOCK), 0)
    cur = (q0 + lax.broadcasted_iota(jnp.int32, (nsel, Q_BLOCK), 1)) // SEL_BLOCK
    valid = blk <= cur
    forced = valid & ((blk == 0) | (blk == cur) | (blk == cur - 1))
    val = [jnp.where(forced, REMOVED, jnp.where(valid, imp[gi], NEG)) for gi in groups]
    chosen = [jnp.where(forced, 1.0, 0.0) for gi in groups]
    blk_f = blk.astype(F32)
    for _ in range(topk - 3):
        for gi in groups:
            mx = jnp.max(val[gi], axis=0, keepdims=True)
            first = jnp.min(jnp.where(val[gi] == mx, blk_f, float(nsel)), axis=0, keepdims=True)
            hit = blk_f == first
            chosen[gi] = jnp.where(hit, 1.0, chosen[gi])
            val[gi] = jnp.where(hit, REMOVED, val[gi])

    pad_rows = [jnp.zeros((LANES - nsel, nq), F32)] if nsel < LANES else []
    for gi in groups:
        bias = jnp.concatenate([(chosen[gi] - 1.0) * BIG] * ATT_HPG, axis=1)
        qa_ref[gi] = q_operand(qr_t[gi], no_flag, jnp.concatenate([bias] + pad_rows, axis=0).astype(BF16))

    def sel_scores(gi, i):
        k0 = pl.multiple_of(i * KEY_STEP, KEY_STEP)
        lhs = jnp.concatenate([ks_ref[gi, pl.ds(k0, KEY_STEP), :], oh_ref[pl.ds(k0, KEY_STEP), :]], axis=1)
        return jnp.dot(lhs, qa_ref[gi], preferred_element_type=F32)

    row_k = lax.broadcasted_iota(jnp.int32, (KEY_STEP, nq), 0)

    def softmax_step(s_ref, p_ref, gi, m, first_key):
        s = s_ref[gi]
        if first_key is not None:
            s = jnp.where(row_k <= tq - first_key, s, NEG)
        m_new = jnp.maximum(m, jnp.max(s, axis=0, keepdims=True))
        p_ref[gi] = jnp.exp2(s - m_new).astype(BF16)
        return m_new, jnp.exp2(m - m_new)

    def pv(gi, acc, alpha, p_ref, i):
        return acc * alpha + jnp.dot(vst_ref[gi, i], p_ref[gi], preferred_element_type=F32)

    def sel_pair(j, carries, final):
        a = 2 * j
        m, acc, alpha_b = ([c[k] for c in carries] for k in range(3))
        alpha_a = [None] * NSA_GROUPS
        for gi in groups:
            sb_ref[gi] = sel_scores(gi, a + 1)
            acc[gi] = pv(gi, acc[gi], alpha_b[gi], pb_ref, jnp.maximum(a - 1, 0))
            m[gi], alpha_a[gi] = softmax_step(sa_ref, pa_ref, gi, m[gi], a * KEY_STEP if final else None)
            if not final:
                sa_ref[gi] = sel_scores(gi, a + 2)
            acc[gi] = pv(gi, acc[gi], alpha_a[gi], pa_ref, a)
            m[gi], alpha_b[gi] = softmax_step(sb_ref, pb_ref, gi, m[gi], (a + 1) * KEY_STEP if final else None)
        return tuple((m[gi], acc[gi], alpha_b[gi]) for gi in groups)

    n_pairs = (q0 + Q_BLOCK - 1) // (2 * KEY_STEP) + 1
    for gi in groups:
        sa_ref[gi] = sel_scores(gi, 0)
    pb_ref[...] = jnp.zeros_like(pb_ref)
    init = tuple((jnp.full((1, nq), NEG, F32), jnp.zeros((V_ROWS, nq), F32), jnp.ones((1, nq), F32))
                 for gi in groups)
    carries = lax.fori_loop(0, n_pairs - 1, lambda j, c: sel_pair(j, c, False), init)
    carries = sel_pair(n_pairs - 1, carries, True)
    acc_s = [pv(gi, carries[gi][1], carries[gi][2], pb_ref, 2 * n_pairs - 1) for gi in groups]

    gt_ref[...] = _sigmoid(gate_ref[...]).T
    hd = HEAD_DIM
    for gi in groups:
        def gate_row(branch, gi=gi):
            base = MISC_GATE0 + branch * ATT_HEADS + (gp * NSA_GROUPS + gi) * ATT_HPG
            return jnp.concatenate([gt_ref[pl.ds(base + r, 1), :] for r in range(ATT_HPG)], axis=1)

        o = ((gate_row(0) * inv_c[gi]) * acc_c[gi][0:hd] + (gate_row(1) / acc_s[gi][hd:hd + 1]) * acc_s[gi][0:hd]
             + (gate_row(2) / acc_w[gi][hd:hd + 1]) * acc_w[gi][0:hd])
        for r in range(ATT_HPG):
            row0 = (gi * ATT_HPG + r) * HEAD_DIM
            o_ref[0, row0:row0 + HEAD_DIM, :] = o[:, r * Q_BLOCK:(r + 1) * Q_BLOCK].astype(o_ref.dtype)


def _nsa(qn_t, qr_t, kc, vc_t, ks, vs_t, kw, vw_t, raw_gate, bsz, s):
    nqb = s // Q_BLOCK
    ncmp = s // CMP_STRIDE
    nsel = s // SEL_BLOCK
    topk = min(SEL_TOPK, nsel)
    gw = ATT_HPG * HEAD_DIM
    ci = jnp.arange(ncmp)[None, :]
    sj = jnp.arange(nsel)[:, None]
    ov_t = (ci * CMP_STRIDE < (sj + 1) * SEL_BLOCK) & (ci * CMP_STRIDE + CMP_LEN > sj * SEL_BLOCK) \
        & (ci < ncmp - 1)
    ov_t3 = jnp.concatenate([ov_t.astype(BF16)] * 3, axis=1)
    assert nsel <= LANES
    onehot = (jnp.arange(s)[:, None] // SEL_BLOCK == jnp.arange(LANES)[None, :]).astype(BF16)
    eye = jnp.eye(Q_BLOCK, dtype=BF16)
    key_row = jnp.arange(Q_BLOCK)[:, None]
    q_lane = jnp.arange(ATT_HPG * Q_BLOCK)[None, :] % Q_BLOCK
    win_lo = jnp.where(key_row > q_lane, 0.0, NEG).astype(BF16)
    win_hi = jnp.where(key_row <= q_lane, 0.0, NEG).astype(BF16)
    once = pl.Buffered(1)
    const2 = lambda shape: pl.BlockSpec(shape, lambda b, g, i: (0, 0), pipeline_mode=once)
    ng = NSA_GROUPS
    gpb = KV_GROUPS // ng
    q_spec = pl.BlockSpec((1, ng * gw, Q_BLOCK), lambda b, g, i: (b, g, i))
    per_bg = lambda shape: pl.BlockSpec((ng,) + shape, lambda b, g, i: (b * gpb + g,) + (0,) * len(shape),
                                        pipeline_mode=once)
    nql = ATT_HPG * Q_BLOCK
    return pl.pallas_call(
        functools.partial(_nsa_kernel, topk=topk),
        grid=(bsz, gpb, nqb),
        in_specs=[q_spec, q_spec,
                  per_bg((ncmp, LANES)), per_bg((V_ROWS, ncmp)),
                  per_bg((s, LANES)), per_bg((s // KEY_STEP, V_ROWS, KEY_STEP)),
                  per_bg((s, LANES)), per_bg((s // Q_BLOCK, V_ROWS, Q_BLOCK)),
                  pl.BlockSpec((Q_BLOCK, LANES), lambda b, g, i: (b * nqb + i, GATE_MISC // LANES)),
                  const2((nsel, 3 * ncmp)), const2((s, LANES)),
                  const2((Q_BLOCK, Q_BLOCK)), const2((Q_BLOCK, ATT_HPG * Q_BLOCK)),
                  const2((Q_BLOCK, ATT_HPG * Q_BLOCK))],
        out_specs=pl.BlockSpec((1, ng * gw, Q_BLOCK), lambda b, g, i: (b, g, i)),
        out_shape=jax.ShapeDtypeStruct((bsz, ATT_HEADS * HEAD_DIM, s), BF16),
        scratch_shapes=[pltpu.VMEM((LANES, Q_BLOCK), F32),
                        pltpu.VMEM((ng, 2 * LANES, nql), BF16),
                        pltpu.VMEM((ng, KEY_STEP, nql), F32),
                        pltpu.VMEM((ng, KEY_STEP, nql), F32),
                        pltpu.VMEM((ng, KEY_STEP, nql), BF16),
                        pltpu.VMEM((ng, KEY_STEP, nql), BF16),
                        pltpu.VMEM((ng, ncmp, nql), F32),
                        pltpu.VMEM((ng, V_ROWS, nql), F32),
                        pltpu.VMEM((ng, 8, nql), F32),
                        pltpu.VMEM((ng, nsel, Q_BLOCK), F32)],
        compiler_params=_cparams(("parallel", "parallel", "arbitrary")),
        name="nsa_sweep",
    )(qn_t, qr_t, kc, vc_t, ks, vs_t, kw, vw_t, raw_gate, ov_t3, onehot, eye, win_lo, win_hi)


def _merge_mlp_kernel(ot_ref, yssm_ref, mix_ref, x_ref, wab_ref, wo_ref, g1_ref, wup_ref, wdn_ref, g2_ref, o_ref):
    o = ot_ref[0].astype(F32).T.astype(BF16)
    y_att = jnp.dot(o, wab_ref[...], preferred_element_type=F32)
    gates = _sigmoid(mix_ref[...])
    mixed = gates[:, 0:D_MODEL] * yssm_ref[...] + gates[:, D_MODEL:2 * D_MODEL] * y_att
    x = x_ref[...] + jnp.dot(mixed.astype(BF16), wo_ref[...], preferred_element_type=F32)
    ms = jnp.mean(x * x, axis=-1, keepdims=True)
    h = (x * lax.rsqrt(ms + NORM_EPS) * g1_ref[...]).astype(BF16)
    up = jnp.maximum(jnp.dot(h, wup_ref[...], preferred_element_type=F32), 0.0)
    y = x + jnp.dot((up * up).astype(BF16), wdn_ref[...], preferred_element_type=F32)
    ms2 = jnp.mean(y * y, axis=-1, keepdims=True)
    o_ref[...] = y * lax.rsqrt(ms2 + NORM_EPS) * g2_ref[...]


def _merge_mlp(o_t, y_ssm, raw_gate, x2d, w_ab, w_o, g1, w_up, w_down, g2, s):
    t = x2d.shape[0]
    tm = ROW_TILE // 2
    spb = s // tm
    row_spec = pl.BlockSpec((tm, D_MODEL), lambda i: (i, 0))
    const = lambda shape: pl.BlockSpec(shape, lambda i: (0, 0), pipeline_mode=pl.Buffered(1))
    return pl.pallas_call(
        _merge_mlp_kernel,
        grid=(t // tm,),
        in_specs=[pl.BlockSpec((1, D_MODEL, tm), lambda i: (i // spb, 0, i % spb)),
                  row_spec,
                  pl.BlockSpec((tm, 2 * D_MODEL), lambda i: (i, GATE_MIX // (2 * D_MODEL))),
                  row_spec, const((D_MODEL, D_MODEL)), const((D_MODEL, D_MODEL)),
                  const((1, D_MODEL)), const((D_MODEL, MLP_HIDDEN)), const((MLP_HIDDEN, D_MODEL)),
                  const((1, D_MODEL))],
        out_specs=row_spec,
        out_shape=jax.ShapeDtypeStruct((t, D_MODEL), F32),
        compiler_params=_cparams(("parallel",)),
        name="merge_mlp",
    )(o_t, y_ssm, raw_gate, x2d, w_ab, w_o, g1.reshape(1, D_MODEL), w_up, w_down, g2.reshape(1, D_MODEL))


def _rope_tables(positions):
    inv_freq = ROPE_THETA ** (-jnp.arange(0, ROPE_DIM, 2, dtype=F32) / ROPE_DIM)
    ang = positions.astype(F32).reshape(1, -1) * inv_freq[:, None]
    return jnp.cos(ang), jnp.sin(ang)


def kernel(x, positions, g_norm_mix, w_in, conv_w, conv_b, dt_bias, a_log, d_skip, g_ssm_norm, w_ssm_branch, cmp_pe_k, cmp_pe_v, w_cmp_k1, w_cmp_k2, w_cmp_v1, w_cmp_v2, w_attn_branch, w_o, g_norm_mlp, w_up, w_down, g_norm_final):
    bsz, s, d = x.shape
    assert d == D_MODEL and s % ROW_TILE == 0 and w_in.shape[0] == 1
    t = bsz * s
    x2d = x.reshape(t, d)
    tabs = _rope_tables(positions)

    w = w_in[0].astype(BF16)
    o_dt, o_q, o_kv, o_ag, o_mg = 6144, 6176, 7200, 8736, 8784
    misc_pad = jnp.zeros((d, GATE_COLS - GATE_MISC - SSM_HEADS - 3 * ATT_HEADS), BF16)
    w_gate = jnp.concatenate([w[:, o_mg:], w[:, o_dt:o_q], w[:, o_ag:o_mg], misc_pad], axis=1)
    w_q = w[:, o_q:o_kv]
    w_kv = w[:, o_kv:o_ag]

    h = _rmsnorm(x2d, g_norm_mix[0])
    raw = _proj_raw(h, w, SSM_COLS, SSM_TN, 2 * ROW_TILE, "proj_ssm")
    raw_gate = _proj_raw(h, w_gate, GATE_COLS, GATE_COLS, ROW_TILE, "proj_gate")
    qn_t, qr_t = _proj_q(h, w_q, tabs, bsz, s)
    kvc, ks, kw, vs_t, vw_t = _proj_kv(h, w_kv, tabs, bsz, s)

    y_ssm = _ssd(raw, raw_gate, conv_w[0], conv_b[0], dt_bias[0], a_log[0], d_skip[0], g_ssm_norm[0],
                 w_ssm_branch[0], bsz, s)

    rows = s // CMP_STRIDE
    u = kvc.reshape(bsz * 2 * KV_GROUPS, rows, CMP_STRIDE * HEAD_DIM)
    pe = jnp.stack([cmp_pe_k[0], cmp_pe_v[0]]).astype(F32).reshape(2, 2, 1, CMP_STRIDE * HEAD_DIM)
    w1 = jnp.stack([w_cmp_k1[0], w_cmp_v1[0]]).astype(BF16)
    w2 = jnp.pad(jnp.stack([w_cmp_k2[0], w_cmp_v2[0]]).astype(BF16), ((0, 0), (0, 0), (0, LANES - HEAD_DIM)))
    cmp = _compress(u, pe, w1, w2).reshape(bsz, 2, KV_GROUPS, rows, LANES)
    kc = cmp[:, 0].reshape(bsz * KV_GROUPS, rows, LANES)
    vc_t = jnp.swapaxes(cmp[:, 1], -1, -2)[:, :, :V_ROWS].reshape(bsz * KV_GROUPS, V_ROWS, rows)

    flat = lambda a: a.reshape((bsz * KV_GROUPS,) + a.shape[2:])
    o_t = _nsa(qn_t, qr_t, kc, vc_t, flat(ks), flat(vs_t), flat(kw), flat(vw_t), raw_gate, bsz, s)

    out = _merge_mlp(o_t, y_ssm, raw_gate, x2d, w_attn_branch[0].astype(BF16), w_o[0].astype(BF16),
                     g_norm_mlp[0], w_up[0].astype(BF16), w_down[0].astype(BF16), g_norm_final, s)
    return out.reshape(bsz, s, d)
```

```python
import functools

import jax
import jax.numpy as jnp
from jax import lax
from jax.experimental import pallas as pl
from jax.experimental.pallas import tpu as pltpu

F32 = jnp.float32
BF16 = jnp.bfloat16

D_MODEL = 1024
D_INNER = 2048
SSM_HEADS = 32
SSM_GROUPS = 8
SSM_HPG = 4
SSM_HEAD_DIM = 64
SSM_STATE = 128
SSM_CONV = 4
CHUNK = 128
SSM_XBC = 4096
HEAD_DIM = 64
ATT_HEADS = 16
KV_GROUPS = 4
ATT_HPG = 4
CMP_LEN = 32
CMP_STRIDE = 16
CMP_HIDDEN = 256
SEL_BLOCK = 64
SEL_TOPK = 16
WINDOW = 512
Q_BLOCK = 128
ROPE_THETA = 500000.0
ROPE_DIM = 16
MLP_HIDDEN = 4096
NORM_EPS = 1e-6
NEG = -1e30
BIG = 1e30
REMOVED = -3e38
LOG2E = 1.4426950408889634
SCALE = HEAD_DIM ** -0.5 * LOG2E
KEY_STEP = 256
NSA_GROUPS = 4

LANES = 128
ROW_TILE = 512
VMEM_LIMIT = 56 * 1024 * 1024

RAW_Z, RAW_XS, RAW_BC = 0, 2048, 4096
SSM_COLS, SSM_TN = 6144, 1536
GATE_MIX, GATE_MISC, GATE_COLS = 0, 2048, 2176
MISC_GATE0 = SSM_HEADS
V_ROWS = HEAD_DIM + 16


def _cparams(sem):
    return pltpu.CompilerParams(dimension_semantics=sem, vmem_limit_bytes=VMEM_LIMIT)


def _sigmoid(x):
    return 0.5 * jnp.tanh(0.5 * x) + 0.5


def _silu(x):
    h = 0.5 * x
    return h + h * jnp.tanh(h)


def _split3(x):
    hi = x.astype(BF16)
    r1 = x - hi.astype(F32)
    mid = r1.astype(BF16)
    lo = (r1 - mid.astype(F32)).astype(BF16)
    return hi, mid, lo


def _rmsnorm_kernel(x_ref, g_ref, o_ref):
    x = x_ref[...]
    ms = jnp.mean(x * x, axis=-1, keepdims=True)
    o_ref[...] = (x * lax.rsqrt(ms + NORM_EPS) * g_ref[...]).astype(o_ref.dtype)


def _rmsnorm(x2d, g):
    t, d = x2d.shape
    return pl.pallas_call(
        _rmsnorm_kernel,
        grid=(t // ROW_TILE,),
        in_specs=[pl.BlockSpec((ROW_TILE, d), lambda i: (i, 0)),
                  pl.BlockSpec((1, d), lambda i: (0, 0))],
        out_specs=pl.BlockSpec((ROW_TILE, d), lambda i: (i, 0)),
        out_shape=jax.ShapeDtypeStruct((t, d), BF16),
        compiler_params=_cparams(("parallel",)),
        name="rmsnorm",
    )(x2d, g.reshape(1, d))


def _proj_raw_kernel(a_ref, w_ref, o_ref):
    o_ref[...] = jnp.dot(a_ref[...], w_ref[...], preferred_element_type=F32)


def _proj_raw(h, w, n, tn, tm, name):
    t, k = h.shape
    return pl.pallas_call(
        _proj_raw_kernel,
        grid=(n // tn, t // tm),
        in_specs=[pl.BlockSpec((tm, k), lambda j, i: (i, 0)),
                  pl.BlockSpec((k, tn), lambda j, i: (0, j))],
        out_specs=pl.BlockSpec((tm, tn), lambda j, i: (i, j)),
        out_shape=jax.ShapeDtypeStruct((t, n), F32),
        compiler_params=_cparams(("parallel", "parallel")),
        name=name,
    )(h, w)


def _rope_t(t_t, cos_t, sin_t):
    half = ROPE_DIM // 2
    rows = []
    for base in range(0, t_t.shape[0], HEAD_DIM):
        t1, t2 = t_t[base:base + half], t_t[base + half:base + ROPE_DIM]
        rows += [t1 * cos_t - t2 * sin_t, t2 * cos_t + t1 * sin_t, t_t[base + ROPE_DIM:base + HEAD_DIM]]
    return jnp.concatenate(rows, axis=0)


def _proj_q_kernel(a_ref, w_ref, cos_ref, sin_ref, qn_ref, qr_ref):
    acc = jnp.dot(a_ref[...], w_ref[...], preferred_element_type=F32)
    cos_t, sin_t = cos_ref[...], sin_ref[...]
    for c in range(acc.shape[1] // LANES):
        t_t = acc[:, c * LANES:(c + 1) * LANES].T
        qn_ref[0, c * LANES:(c + 1) * LANES, :] = (t_t * SCALE).astype(qn_ref.dtype)
        qr_ref[0, c * LANES:(c + 1) * LANES, :] = (_rope_t(t_t, cos_t, sin_t) * SCALE).astype(qr_ref.dtype)


def _proj_q(h, w, tabs, bsz, s):
    t, k = h.shape
    n = w.shape[1]
    spb = s // ROW_TILE
    tab_spec = pl.BlockSpec((ROPE_DIM // 2, ROW_TILE), lambda i: (0, i))
    out_spec = pl.BlockSpec((1, n, ROW_TILE), lambda i: (i // spb, 0, i % spb))
    return pl.pallas_call(
        _proj_q_kernel,
        grid=(t // ROW_TILE,),
        in_specs=[pl.BlockSpec((ROW_TILE, k), lambda i: (i, 0)),
                  pl.BlockSpec((k, n), lambda i: (0, 0)),
                  tab_spec, tab_spec],
        out_specs=[out_spec, out_spec],
        out_shape=[jax.ShapeDtypeStruct((bsz, n, s), BF16)] * 2,
        compiler_params=_cparams(("parallel",)),
        name="proj_q",
    )(h, w, *tabs)


def _proj_kv_kernel(a_ref, w_ref, cos_ref, sin_ref, kvc_ref, ks_ref, kw_ref, vst_ref, vwt_ref, kvc_scr):
    acc = jnp.dot(a_ref[...], w_ref[...], preferred_element_type=F32)
    cos_t, sin_t = cos_ref[...], sin_ref[...]
    tm = acc.shape[0]
    gw = KV_GROUPS * HEAD_DIM
    rows = tm // CMP_STRIDE
    low = lax.broadcasted_iota(jnp.int32, (rows, LANES), 1) < HEAD_DIM
    for c in range(2 * gw // LANES):
        kvc_scr[c] = acc[:, c * LANES:(c + 1) * LANES]
        for j in range(CMP_STRIDE // 2):
            e = kvc_scr[c, pl.ds(2 * j, rows, stride=CMP_STRIDE), :]
            o = kvc_scr[c, pl.ds(2 * j + 1, rows, stride=CMP_STRIDE), :]
            kvc_ref[0, 2 * c, :, j * LANES:(j + 1) * LANES] = jnp.where(low, e, pltpu.roll(o, HEAD_DIM, 1))
            kvc_ref[0, 2 * c + 1, :, j * LANES:(j + 1) * LANES] = jnp.where(low, pltpu.roll(e, HEAD_DIM, 1), o)
    lane = lax.broadcasted_iota(jnp.int32, (tm, LANES), 1)
    key_pad = jnp.where(lane == HEAD_DIM, 1.0, 0.0)
    ones_rows = jnp.where(lax.broadcasted_iota(jnp.int32, (V_ROWS - HEAD_DIM, tm), 0) == 0, 1.0, 0.0)
    for base, k_ref, vt_ref in ((2 * gw, ks_ref, vst_ref), (4 * gw, kw_ref, vwt_ref)):
        vtile = vt_ref.shape[-1]
        for c in range(gw // LANES):
            kk = _rope_t(acc[:, base + c * LANES: base + (c + 1) * LANES].T, cos_t, sin_t).T
            vv = acc[:, base + gw + c * LANES: base + gw + (c + 1) * LANES].T
            for half in range(2):
                g = 2 * c + half
                k_lo = kk if half == 0 else pltpu.roll(kk, HEAD_DIM, 1)
                k_ref[0, g] = jnp.where(lane < HEAD_DIM, k_lo, key_pad).astype(k_ref.dtype)
                v_aug = jnp.concatenate([vv[half * HEAD_DIM:(half + 1) * HEAD_DIM], ones_rows], axis=0)
                for j in range(ROW_TILE // vtile):
                    vt_ref[0, g, j] = v_aug[:, j * vtile:(j + 1) * vtile].astype(vt_ref.dtype)


def _proj_kv(h, w, tabs, bsz, s):
    t, k = h.shape
    n = w.shape[1]
    spb = s // ROW_TILE
    tab_spec = pl.BlockSpec((ROPE_DIM // 2, ROW_TILE), lambda i: (0, i))
    k_spec = pl.BlockSpec((1, KV_GROUPS, ROW_TILE, LANES), lambda i: (i // spb, 0, i % spb, 0))
    k_shape = jax.ShapeDtypeStruct((bsz, KV_GROUPS, s, LANES), BF16)

    def vt(tile):
        return (pl.BlockSpec((1, KV_GROUPS, ROW_TILE // tile, V_ROWS, tile), lambda i: (i // spb, 0, i % spb, 0, 0)),
                jax.ShapeDtypeStruct((bsz, KV_GROUPS, s // tile, V_ROWS, tile), BF16))

    (vs_spec, vs_shape), (vw_spec, vw_shape) = vt(KEY_STEP), vt(Q_BLOCK)
    return pl.pallas_call(
        _proj_kv_kernel,
        grid=(t // ROW_TILE,),
        in_specs=[pl.BlockSpec((ROW_TILE, k), lambda i: (i, 0)),
                  pl.BlockSpec((k, n), lambda i: (0, 0)),
                  tab_spec, tab_spec],
        out_specs=[pl.BlockSpec((1, 2 * KV_GROUPS, ROW_TILE // CMP_STRIDE, CMP_STRIDE * HEAD_DIM),
                                lambda i: (i // spb, 0, i % spb, 0)),
                   k_spec, k_spec, vs_spec, vw_spec],
        out_shape=[jax.ShapeDtypeStruct((bsz, 2 * KV_GROUPS, s // CMP_STRIDE, CMP_STRIDE * HEAD_DIM), F32),
                   k_shape, k_shape, vs_shape, vw_shape],
        scratch_shapes=[pltpu.VMEM((2 * KV_GROUPS * HEAD_DIM // LANES, ROW_TILE, LANES), F32)],
        compiler_params=_cparams(("parallel",)),
        name="proj_kv",
    )(h, w, *tabs)


def _ssd_kernel(z_ref, xs_ref, bc_ref, misc_ref, convw_ref, convb_ref, dtb_ref, alog_ref,
                dskip_ref, gnorm_ref, selb_ref, sele_ref, wout_ref, o_ref,
                tail_ref, act_ref, state_ref):
    c = pl.program_id(1)
    L = CHUNK
    gw = SSM_HPG * SSM_HEAD_DIM

    @pl.when(c == 0)
    def _():
        state_ref[...] = jnp.zeros_like(state_ref)
        tail_ref[...] = jnp.zeros_like(tail_ref)

    cw = 512
    first_row = lax.broadcasted_iota(jnp.int32, (8, cw), 0) == 0
    for cc in range(SSM_XBC // cw):
        cols = slice(cc * cw, (cc + 1) * cw)
        u = xs_ref[:, cols] if cc < D_INNER // cw else bc_ref[:, cc * cw - D_INNER:(cc + 1) * cw - D_INNER]
        a = convw_ref[0:1, cols] * u
        for k in range(1, SSM_CONV):
            shifted = pltpu.roll(a, 1, 0)
            shifted = jnp.concatenate([jnp.where(first_row, tail_ref[k - 1:k, cols], shifted[0:8]), shifted[8:]], axis=0)
            tail_ref[k - 1:k, cols] = a[L - 1:L]
            a = convw_ref[k:k + 1, cols] * u + shifted
        act_ref[:, cols] = _silu(a + convb_ref[:, cols])

    lane = lax.broadcasted_iota(jnp.int32, (L, LANES), 1)
    row = lax.broadcasted_iota(jnp.int32, (L, LANES), 0)
    head_lane = lane < SSM_HEADS
    raw = misc_ref[...] + dtb_ref[...]
    dt = jnp.where(head_lane, jnp.maximum(raw, 0.0) + jnp.log1p(jnp.exp(-jnp.abs(raw))), 0.0)
    a_row = jnp.where(head_lane[0:1], -jnp.exp(alog_ref[...]), 0.0)
    cs = dt * a_row
    sh = 1
    while sh < L:
        cs = cs + jnp.where(row >= sh, pltpu.roll(cs, sh, 0), 0.0)
        sh *= 2
    tot = cs[L - 1:L, :]
    w_state = dt * jnp.exp(tot - cs)
    cs_t, dt_t, w_t = cs.T, dt.T, w_state.T

    hi, mid, lo = _split3(cs)
    packed = (hi.astype(F32) + pltpu.roll(mid.astype(F32), SSM_HEADS, 1)
              + pltpu.roll(lo.astype(F32), 2 * SSM_HEADS, 1)).astype(BF16)

    tri = lax.broadcasted_iota(jnp.int32, (L, L), 0) >= lax.broadcasted_iota(jnp.int32, (L, L), 1)
    brow = lax.broadcasted_iota(jnp.int32, (SSM_HPG * L, gw), 0) // L
    bcol = lax.broadcasted_iota(jnp.int32, (SSM_HPG * L, gw), 1) // SSM_HEAD_DIM
    blockmask = brow == bcol

    for g in range(SSM_GROUPS):
        col_b = jnp.dot(packed, selb_ref[:, g * SSM_HPG * L:(g + 1) * SSM_HPG * L], preferred_element_type=F32)
        exp_e = jnp.exp(jnp.dot(packed, sele_ref[:, g * gw:(g + 1) * gw], preferred_element_type=F32))
        etot_e = exp_e[L - 1:L, :]
        xs_g = act_ref[:, g * gw:(g + 1) * gw]
        b_g = act_ref[:, D_INNER + g * SSM_STATE: D_INNER + (g + 1) * SSM_STATE]
        c_g = act_ref[:, D_INNER + SSM_GROUPS * SSM_STATE + g * SSM_STATE:
                      D_INNER + SSM_GROUPS * SSM_STATE + (g + 1) * SSM_STATE]
        c_bf = c_g.astype(BF16)
        cb = lax.dot_general(c_bf, b_g.astype(BF16), (((1,), (1,)), ((), ())),
                             preferred_element_type=F32)
        b_t = b_g.T
        tops, bots = [], []
        for j in range(SSM_HPG):
            h = g * SSM_HPG + j
            decay = jnp.exp(jnp.where(tri, col_b[:, j * L:(j + 1) * L] - cs_t[h:h + 1, :], NEG))
            tops.append((cb * decay * dt_t[h:h + 1, :]).astype(BF16))
            bots.append((b_t * w_t[h:h + 1, :]).astype(BF16))
        lhs = jnp.concatenate([jnp.concatenate(tops, axis=1), jnp.concatenate(bots, axis=1)], axis=0)
        xs_bf = xs_g.astype(BF16)
        xs_bd = jnp.where(blockmask, jnp.concatenate([xs_bf] * SSM_HPG, axis=0), jnp.zeros((), BF16))
        res = jnp.dot(lhs, xs_bd, preferred_element_type=F32)
        st = state_ref[g]
        y_off = jnp.dot(c_bf, st.astype(BF16), preferred_element_type=F32) * exp_e
        state_ref[g] = st * etot_e + res[L:2 * L]
        y = res[0:L] + y_off + dskip_ref[:, g * gw:(g + 1) * gw] * xs_g
        zg = z_ref[:, g * gw:(g + 1) * gw]
        y = y * _silu(zg)
        ms = jnp.mean(y * y, axis=-1, keepdims=True)
        act_ref[:, g * gw:(g + 1) * gw] = y * lax.rsqrt(ms + NORM_EPS) * gnorm_ref[:, g * gw:(g + 1) * gw]

    o_ref[...] = jnp.dot(act_ref[:, 0:D_INNER].astype(BF16), wout_ref[...], preferred_element_type=F32)


def _ssd(raw, raw_gate, conv_w, conv_b, dt_bias, a_log, d_skip, g_ssm_norm, w_ssm, bsz, s):
    t = raw.shape[0]
    nc = s // CHUNK
    blk = lambda cb: pl.BlockSpec((CHUNK, 2048), lambda b, c, cb=cb: (b * nc + c, cb))
    const = lambda shape: pl.BlockSpec(shape, lambda b, c: (0,) * len(shape))
    k = jnp.arange(LANES)
    selb = ((k[:, None] < 3 * SSM_HEADS) & ((k[:, None] % SSM_HEADS) == (jnp.arange(SSM_HEADS * LANES)[None, :] // LANES)))
    sele = ((k[:, None] < 3 * SSM_HEADS) & ((k[:, None] % SSM_HEADS) == (jnp.arange(D_INNER)[None, :] // SSM_HEAD_DIM)))
    pad = lambda v: jnp.pad(v.astype(F32), (0, LANES - SSM_HEADS)).reshape(1, LANES)
    return pl.pallas_call(
        _ssd_kernel,
        grid=(bsz, nc),
        in_specs=[blk(RAW_Z // 2048), blk(RAW_XS // 2048), blk(RAW_BC // 2048),
                  pl.BlockSpec((CHUNK, LANES), lambda b, c: (b * nc + c, GATE_MISC // LANES)),
                  const((SSM_CONV, SSM_XBC)), const((1, SSM_XBC)), const((1, LANES)), const((1, LANES)),
                  const((1, D_INNER)), const((1, D_INNER)),
                  const((LANES, SSM_HEADS * LANES)), const((LANES, D_INNER)),
                  const((D_INNER, D_MODEL))],
        out_specs=pl.BlockSpec((CHUNK, D_MODEL), lambda b, c: (b * nc + c, 0)),
        out_shape=jax.ShapeDtypeStruct((t, D_MODEL), F32),
        scratch_shapes=[pltpu.VMEM((8, SSM_XBC), F32),
                        pltpu.VMEM((CHUNK, SSM_XBC), F32),
                        pltpu.VMEM((SSM_GROUPS, SSM_STATE, SSM_HPG * SSM_HEAD_DIM), F32)],
        compiler_params=_cparams(("parallel", "arbitrary")),
        name="ssd_mixer",
    )(raw, raw, raw, raw_gate, conv_w.astype(F32), conv_b.reshape(1, SSM_XBC).astype(F32),
      pad(dt_bias), pad(a_log), jnp.repeat(d_skip.astype(F32), SSM_HEAD_DIM).reshape(1, D_INNER),
      g_ssm_norm.reshape(1, D_INNER).astype(F32), selb.astype(BF16), sele.astype(BF16), w_ssm.astype(BF16))


def _compress_kernel(u_ref, pelo_ref, pehi_ref, w1_ref, w2_ref, o_ref):
    u = u_ref[0]
    half = CMP_STRIDE * HEAD_DIM
    a = jnp.dot((u + pelo_ref[0, 0]).astype(BF16), w1_ref[0, 0:half, :], preferred_element_type=F32)
    b = jnp.dot((u + pehi_ref[0, 0]).astype(BF16), w1_ref[0, half:2 * half, :], preferred_element_type=F32)
    pre = a + pltpu.roll(b, u.shape[0] - 1, 0)
    hidden = _silu(pre)
    out = jnp.dot(hidden.astype(BF16), w2_ref[0], preferred_element_type=F32)
    lane = lax.broadcasted_iota(jnp.int32, out.shape, 1)
    o_ref[0] = jnp.where(lane == HEAD_DIM, 1.0, out).astype(o_ref.dtype)


def _compress(u, pe, w1, w2):
    n, rows, width = u.shape
    kv_of = lambda i: (i // KV_GROUPS) % 2
    return pl.pallas_call(
        _compress_kernel,
        grid=(n,),
        in_specs=[pl.BlockSpec((1, rows, width), lambda i: (i, 0, 0)),
                  pl.BlockSpec((1, 1, 1, width), lambda i: (kv_of(i), 0, 0, 0)),
                  pl.BlockSpec((1, 1, 1, width), lambda i: (kv_of(i), 1, 0, 0)),
                  pl.BlockSpec((1, 2 * width, CMP_HIDDEN), lambda i: (kv_of(i), 0, 0)),
                  pl.BlockSpec((1, CMP_HIDDEN, LANES), lambda i: (kv_of(i), 0, 0))],
        out_specs=pl.BlockSpec((1, rows, LANES), lambda i: (i, 0, 0)),
        out_shape=jax.ShapeDtypeStruct((n, rows, LANES), BF16),
        compiler_params=_cparams(("parallel",)),
        name="compress",
    )(u, pe, pe, w1, w2)


def _nsa_kernel(qn_ref, qr_ref, kc_ref, vct_ref, ks_ref, vst_ref, kw_ref, vwt_ref, gate_ref, ovt_ref, oh_ref,
                eye_ref, wlo_ref, whi_ref,
                o_ref, gt_ref, qa_ref, sa_ref, sb_ref, pa_ref, pb_ref, sc_ref, accc_ref, invc_ref, imp_ref,
                *, topk):
    gp = pl.program_id(1)
    qb = pl.program_id(2)
    groups = range(NSA_GROUPS)
    nq = ATT_HPG * Q_BLOCK
    ncmp = kc_ref.shape[1]
    nsel = ovt_ref.shape[0]
    q0 = qb * Q_BLOCK
    tq = q0 + lax.broadcasted_iota(jnp.int32, (1, nq), 1) % Q_BLOCK

    def heads_on_lanes(ref, gi):
        base = gi * ATT_HPG * HEAD_DIM
        return jnp.concatenate([ref[0, base + r * HEAD_DIM:base + (r + 1) * HEAD_DIM, :]
                                for r in range(ATT_HPG)], axis=1)

    qn_t = [heads_on_lanes(qn_ref, gi) for gi in groups]
    qr_t = [heads_on_lanes(qr_ref, gi) for gi in groups]

    def q_operand(q_t, flag, table):
        flag_rows = jnp.concatenate([flag, jnp.zeros((LANES - HEAD_DIM - 1, nq), F32)], axis=0).astype(BF16)
        return jnp.concatenate([q_t, flag_rows] + ([table] if table is not None else []), axis=0)

    no_flag = jnp.zeros((1, nq), F32)

    n_chunks = ncmp // LANES
    chunk_span = LANES * CMP_STRIDE
    c_hi = jnp.minimum((q0 + Q_BLOCK - CMP_LEN) // chunk_span, n_chunks - 1)

    def compressed_branch(n_live):
        live = n_live * LANES
        ov_live = jnp.concatenate([ovt_ref[:, piece * ncmp:piece * ncmp + live] for piece in range(3)], axis=1)
        for c in range(n_live):
            for gi in groups:
                s = jnp.dot(kc_ref[gi, c * LANES:(c + 1) * LANES, :], q_operand(qn_t[gi], no_flag, None),
                            preferred_element_type=F32)
                if c >= n_live - 2:
                    ends = (c * chunk_span + CMP_LEN - 1
                            + lax.broadcasted_iota(jnp.int32, (LANES, nq), 0) * CMP_STRIDE)
                    s = jnp.where(ends <= tq, s, NEG)
                sc_ref[gi, c * LANES:(c + 1) * LANES, :] = s
        for gi in groups:
            s_c = sc_ref[gi, 0:live, :]
            m_c = jnp.max(s_c, axis=0, keepdims=True)
            p_c = jnp.exp2(s_c - m_c)
            acc = jnp.dot(vct_ref[gi, :, 0:live], p_c.astype(BF16), preferred_element_type=F32)
            inv = jnp.where(m_c > 0.5 * NEG, 1.0 / acc[HEAD_DIM:HEAD_DIM + 1], 0.0)
            accc_ref[gi] = acc
            invc_ref[gi] = jnp.broadcast_to(inv, (8, nq))
            p_sum = p_c[:, 0:Q_BLOCK] * inv[:, 0:Q_BLOCK]
            for r in range(1, ATT_HPG):
                p_sum = p_sum + p_c[:, r * Q_BLOCK:(r + 1) * Q_BLOCK] * inv[:, r * Q_BLOCK:(r + 1) * Q_BLOCK]
            imp_ref[gi] = jnp.dot(ov_live, jnp.concatenate(_split3(p_sum), axis=0), preferred_element_type=F32)

    for k in range(n_chunks):
        pl.when(c_hi == k)(functools.partial(compressed_branch, k + 1))
    acc_c = [accc_ref[gi] for gi in groups]
    inv_c = [invc_ref[gi, 0:1, :] for gi in groups]
    imp = [imp_ref[gi] for gi in groups]

    n_wt = WINDOW // Q_BLOCK + 1
    eye = eye_ref[...]
    acc_w = []
    for gi in groups:
        s_w, v_w = [], []
        for i in range(n_wt):
            kt = qb - (n_wt - 1) + i
            k0 = pl.multiple_of(jnp.maximum(kt, 0) * Q_BLOCK, Q_BLOCK)
            keys = kw_ref[gi, pl.ds(k0, Q_BLOCK), :]
            flag = jnp.where(kt >= 0, no_flag, NEG)
            if i == 0:
                s = jnp.dot(jnp.concatenate([keys, eye], axis=1), q_operand(qr_t[gi], flag, wlo_ref[...]),
                            preferred_element_type=F32)
            elif i == n_wt - 1:
                s = jnp.dot(jnp.concatenate([keys, eye], axis=1), q_operand(qr_t[gi], flag, whi_ref[...]),
                            preferred_element_type=F32)
            else:
                s = jnp.dot(keys, q_operand(qr_t[gi], flag, None), preferred_element_type=F32)
            s_w.append(s)
            v_w.append(vwt_ref[gi, jnp.maximum(kt, 0)])
        s_w = jnp.concatenate(s_w, axis=0)
        p_w = jnp.exp2(s_w - jnp.max(s_w, axis=0, keepdims=True))
        acc_w.append(jnp.dot(jnp.concatenate(v_w, axis=1), p_w.astype(BF16), preferred_element_type=F32))

    blk = lax.broadcasted_iota(jnp.int32, (nsel, Q_BLOCK), 0)
    cur = (q0 + lax.broadcasted_iota(jnp.int32, (nsel, Q_BLOCK), 1)) // SEL_BLOCK
    valid = blk <= cur
    forced = valid & ((blk == 0) | (blk == cur) | (blk == cur - 1))
    val = [jnp.where(forced, REMOVED, jnp.where(valid, imp[gi], NEG)) for gi in groups]
    blk_f = blk.astype(F32)
    for _ in range(topk - 3):
        for gi in groups:
            mx = jnp.max(val[gi], axis=0, keepdims=True)
            first = jnp.min(jnp.where(val[gi] == mx, blk_f, float(nsel)), axis=0, keepdims=True)
            val[gi] = jnp.where(blk_f == first, REMOVED, val[gi])

    pad_rows = [jnp.zeros((LANES - nsel, nq), F32)] if nsel < LANES else []
    for gi in groups:
        bias = jnp.concatenate([jnp.where(val[gi] == REMOVED, 0.0, NEG)] * ATT_HPG, axis=1)
        qa_ref[gi] = q_operand(qr_t[gi], no_flag, jnp.concatenate([bias] + pad_rows, axis=0).astype(BF16))

    def sel_scores(gi, i):
        k0 = pl.multiple_of(i * KEY_STEP, KEY_STEP)
        lhs = jnp.concatenate([ks_ref[gi, pl.ds(k0, KEY_STEP), :], oh_ref[pl.ds(k0, KEY_STEP), :]], axis=1)
        return jnp.dot(lhs, qa_ref[gi], preferred_element_type=F32)

    row_k = lax.broadcasted_iota(jnp.int32, (KEY_STEP, nq), 0)

    def softmax_step(s_ref, p_ref, gi, m, first_key):
        s = s_ref[gi]
        if first_key is not None:
            s = jnp.where(row_k <= tq - first_key, s, NEG)
        m_new = jnp.maximum(m, jnp.max(s, axis=0, keepdims=True))
        p_ref[gi] = jnp.exp2(s - m_new).astype(BF16)
        return m_new, jnp.exp2(m - m_new)

    def pv(gi, acc, alpha, p_ref, i):
        return acc * alpha + jnp.dot(vst_ref[gi, i], p_ref[gi], preferred_element_type=F32)

    def sel_pair(j, carries, final):
        a = 2 * j
        m, acc, alpha_b = ([c[k] for c in carries] for k in range(3))
        alpha_a = [None] * NSA_GROUPS
        for gi in groups:
            sb_ref[gi] = sel_scores(gi, a + 1)
            acc[gi] = pv(gi, acc[gi], alpha_b[gi], pb_ref, jnp.maximum(a - 1, 0))
            m[gi], alpha_a[gi] = softmax_step(sa_ref, pa_ref, gi, m[gi], a * KEY_STEP if final else None)
            if not final:
                sa_ref[gi] = sel_scores(gi, a + 2)
            acc[gi] = pv(gi, acc[gi], alpha_a[gi], pa_ref, a)
            m[gi], alpha_b[gi] = softmax_step(sb_ref, pb_ref, gi, m[gi], (a + 1) * KEY_STEP if final else None)
        return tuple((m[gi], acc[gi], alpha_b[gi]) for gi in groups)

    n_pairs = (q0 + Q_BLOCK - 1) // (2 * KEY_STEP) + 1
    for gi in groups:
        sa_ref[gi] = sel_scores(gi, 0)
    pb_ref[...] = jnp.zeros_like(pb_ref)
    init = tuple((jnp.full((1, nq), NEG, F32), jnp.zeros((V_ROWS, nq), F32), jnp.ones((1, nq), F32))
                 for gi in groups)
    carries = lax.fori_loop(0, n_pairs - 1, lambda j, c: sel_pair(j, c, False), init)
    carries = sel_pair(n_pairs - 1, carries, True)
    acc_s = [pv(gi, carries[gi][1], carries[gi][2], pb_ref, 2 * n_pairs - 1) for gi in groups]

    gt_ref[...] = _sigmoid(gate_ref[...]).T
    hd = HEAD_DIM
    for gi in groups:
        def gate_row(branch, gi=gi):
            base = MISC_GATE0 + branch * ATT_HEADS + (gp * NSA_GROUPS + gi) * ATT_HPG
            return jnp.concatenate([gt_ref[pl.ds(base + r, 1), :] for r in range(ATT_HPG)], axis=1)

        o = ((gate_row(0) * inv_c[gi]) * acc_c[gi][0:hd] + (gate_row(1) / acc_s[gi][hd:hd + 1]) * acc_s[gi][0:hd]
             + (gate_row(2) / acc_w[gi][hd:hd + 1]) * acc_w[gi][0:hd])
        for r in range(ATT_HPG):
            row0 = (gi * ATT_HPG + r) * HEAD_DIM
            o_ref[0, row0:row0 + HEAD_DIM, :] = o[:, r * Q_BLOCK:(r + 1) * Q_BLOCK].astype(o_ref.dtype)


def _nsa(qn_t, qr_t, kc, vc_t, ks, vs_t, kw, vw_t, raw_gate, bsz, s):
    nqb = s // Q_BLOCK
    ncmp = s // CMP_STRIDE
    nsel = s // SEL_BLOCK
    topk = min(SEL_TOPK, nsel)
    gw = ATT_HPG * HEAD_DIM
    ci = jnp.arange(ncmp)[None, :]
    sj = jnp.arange(nsel)[:, None]
    ov_t = (ci * CMP_STRIDE < (sj + 1) * SEL_BLOCK) & (ci * CMP_STRIDE + CMP_LEN > sj * SEL_BLOCK) \
        & (ci < ncmp - 1)
    ov_t3 = jnp.concatenate([ov_t.astype(BF16)] * 3, axis=1)
    assert nsel <= LANES
    onehot = (jnp.arange(s)[:, None] // SEL_BLOCK == jnp.arange(LANES)[None, :]).astype(BF16)
    eye = jnp.eye(Q_BLOCK, dtype=BF16)
    key_row = jnp.arange(Q_BLOCK)[:, None]
    q_lane = jnp.arange(ATT_HPG * Q_BLOCK)[None, :] % Q_BLOCK
    win_lo = jnp.where(key_row > q_lane, 0.0, NEG).astype(BF16)
    win_hi = jnp.where(key_row <= q_lane, 0.0, NEG).astype(BF16)
    once = pl.Buffered(1)
    const2 = lambda shape: pl.BlockSpec(shape, lambda b, g, i: (0, 0), pipeline_mode=once)
    ng = NSA_GROUPS
    gpb = KV_GROUPS // ng
    q_spec = pl.BlockSpec((1, ng * gw, Q_BLOCK), lambda b, g, i: (b, g, i))
    per_bg = lambda shape: pl.BlockSpec((ng,) + shape, lambda b, g, i: (b * gpb + g,) + (0,) * len(shape),
                                        pipeline_mode=once)
    nql = ATT_HPG * Q_BLOCK
    return pl.pallas_call(
        functools.partial(_nsa_kernel, topk=topk),
        grid=(bsz, gpb, nqb),
        in_specs=[q_spec, q_spec,
                  per_bg((ncmp, LANES)), per_bg((V_ROWS, ncmp)),
                  per_bg((s, LANES)), per_bg((s // KEY_STEP, V_ROWS, KEY_STEP)),
                  per_bg((s, LANES)), per_bg((s // Q_BLOCK, V_ROWS, Q_BLOCK)),
                  pl.BlockSpec((Q_BLOCK, LANES), lambda b, g, i: (b * nqb + i, GATE_MISC // LANES)),
                  const2((nsel, 3 * ncmp)), const2((s, LANES)),
                  const2((Q_BLOCK, Q_BLOCK)), const2((Q_BLOCK, ATT_HPG * Q_BLOCK)),
                  const2((Q_BLOCK, ATT_HPG * Q_BLOCK))],
        out_specs=pl.BlockSpec((1, ng * gw, Q_BLOCK), lambda b, g, i: (b, g, i)),
        out_shape=jax.ShapeDtypeStruct((bsz, ATT_HEADS * HEAD_DIM, s), BF16),
        scratch_shapes=[pltpu.VMEM((LANES, Q_BLOCK), F32),
                        pltpu.VMEM((ng, 2 * LANES, nql), BF16),
                        pltpu.VMEM((ng, KEY_STEP, nql), F32),
                        pltpu.VMEM((ng, KEY_STEP, nql), F32),
                        pltpu.VMEM((ng, KEY_STEP, nql), BF16),
                        pltpu.VMEM((ng, KEY_STEP, nql), BF16),
                        pltpu.VMEM((ng, ncmp, nql), F32),
                        pltpu.VMEM((ng, V_ROWS, nql), F32),
                        pltpu.VMEM((ng, 8, nql), F32),
                        pltpu.VMEM((ng, nsel, Q_BLOCK), F32)],
        compiler_params=_cparams(("parallel", "parallel", "arbitrary")),
        name="nsa_sweep",
    )(qn_t, qr_t, kc, vc_t, ks, vs_t, kw, vw_t, raw_gate, ov_t3, onehot, eye, win_lo, win_hi)


def _merge_mlp_kernel(ot_ref, yssm_ref, mix_ref, x_ref, wab_ref, wo_ref, g1_ref, wup_ref, wdn_ref, g2_ref, o_ref):
    o = ot_ref[0].astype(F32).T.astype(BF16)
    y_att = jnp.dot(o, wab_ref[...], preferred_element_type=F32)
    gates = _sigmoid(mix_ref[...])
    mixed = gates[:, 0:D_MODEL] * yssm_ref[...] + gates[:, D_MODEL:2 * D_MODEL] * y_att
    x = x_ref[...] + jnp.dot(mixed.astype(BF16), wo_ref[...], preferred_element_type=F32)
    ms = jnp.mean(x * x, axis=-1, keepdims=True)
    h = (x * lax.rsqrt(ms + NORM_EPS) * g1_ref[...]).astype(BF16)
    up = jnp.maximum(jnp.dot(h, wup_ref[...], preferred_element_type=F32), 0.0)
    y = x + jnp.dot((up * up).astype(BF16), wdn_ref[...], preferred_element_type=F32)
    ms2 = jnp.mean(y * y, axis=-1, keepdims=True)
    o_ref[...] = y * lax.rsqrt(ms2 + NORM_EPS) * g2_ref[...]


def _merge_mlp(o_t, y_ssm, raw_gate, x2d, w_ab, w_o, g1, w_up, w_down, g2, s):
    t = x2d.shape[0]
    tm = ROW_TILE
    spb = s // tm
    row_spec = pl.BlockSpec((tm, D_MODEL), lambda i: (i, 0))
    const = lambda shape: pl.BlockSpec(shape, lambda i: (0, 0), pipeline_mode=pl.Buffered(1))
    return pl.pallas_call(
        _merge_mlp_kernel,
        grid=(t // tm,),
        in_specs=[pl.BlockSpec((1, D_MODEL, tm), lambda i: (i // spb, 0, i % spb)),
                  row_spec,
                  pl.BlockSpec((tm, 2 * D_MODEL), lambda i: (i, GATE_MIX // (2 * D_MODEL))),
                  row_spec, const((D_MODEL, D_MODEL)), const((D_MODEL, D_MODEL)),
                  const((1, D_MODEL)), const((D_MODEL, MLP_HIDDEN)), const((MLP_HIDDEN, D_MODEL)),
                  const((1, D_MODEL))],
        out_specs=row_spec,
        out_shape=jax.ShapeDtypeStruct((t, D_MODEL), F32),
        compiler_params=_cparams(("parallel",)),
        name="merge_mlp",
    )(o_t, y_ssm, raw_gate, x2d, w_ab, w_o, g1.reshape(1, D_MODEL), w_up, w_down, g2.reshape(1, D_MODEL))


def _rope_tables(positions):
    inv_freq = ROPE_THETA ** (-jnp.arange(0, ROPE_DIM, 2, dtype=F32) / ROPE_DIM)
    ang = positions.astype(F32).reshape(1, -1) * inv_freq[:, None]
    return jnp.cos(ang), jnp.sin(ang)


def kernel(x, positions, g_norm_mix, w_in, conv_w, conv_b, dt_bias, a_log, d_skip, g_ssm_norm, w_ssm_branch, cmp_pe_k, cmp_pe_v, w_cmp_k1, w_cmp_k2, w_cmp_v1, w_cmp_v2, w_attn_branch, w_o, g_norm_mlp, w_up, w_down, g_norm_final):
    bsz, s, d = x.shape
    assert d == D_MODEL and s % (2 * ROW_TILE) == 0 and w_in.shape[0] == 1
    t = bsz * s
    x2d = x.reshape(t, d)
    tabs = _rope_tables(positions)

    w = w_in[0].astype(BF16)
    o_dt, o_q, o_kv, o_ag, o_mg = 6144, 6176, 7200, 8736, 8784
    misc_pad = jnp.zeros((d, GATE_COLS - GATE_MISC - SSM_HEADS - 3 * ATT_HEADS), BF16)
    w_gate = jnp.concatenate([w[:, o_mg:], w[:, o_dt:o_q], w[:, o_ag:o_mg], misc_pad], axis=1)
    w_q = w[:, o_q:o_kv]
    w_kv = w[:, o_kv:o_ag]

    h = _rmsnorm(x2d, g_norm_mix[0])
    raw = _proj_raw(h, w, SSM_COLS, SSM_TN, 2 * ROW_TILE, "proj_ssm")
    raw_gate = _proj_raw(h, w_gate, GATE_COLS, GATE_COLS, 2 * ROW_TILE, "proj_gate")
    qn_t, qr_t = _proj_q(h, w_q, tabs, bsz, s)
    kvc, ks, kw, vs_t, vw_t = _proj_kv(h, w_kv, tabs, bsz, s)

    y_ssm = _ssd(raw, raw_gate, conv_w[0], conv_b[0], dt_bias[0], a_log[0], d_skip[0], g_ssm_norm[0],
                 w_ssm_branch[0], bsz, s)

    rows = s // CMP_STRIDE
    u = kvc.reshape(bsz * 2 * KV_GROUPS, rows, CMP_STRIDE * HEAD_DIM)
    pe = jnp.stack([cmp_pe_k[0], cmp_pe_v[0]]).astype(F32).reshape(2, 2, 1, CMP_STRIDE * HEAD_DIM)
    w1 = jnp.stack([w_cmp_k1[0], w_cmp_v1[0]]).astype(BF16)
    w2 = jnp.pad(jnp.stack([w_cmp_k2[0], w_cmp_v2[0]]).astype(BF16), ((0, 0), (0, 0), (0, LANES - HEAD_DIM)))
    cmp = _compress(u, pe, w1, w2).reshape(bsz, 2, KV_GROUPS, rows, LANES)
    kc = cmp[:, 0].reshape(bsz * KV_GROUPS, rows, LANES)
    vc_t = jnp.swapaxes(cmp[:, 1], -1, -2)[:, :, :V_ROWS].reshape(bsz * KV_GROUPS, V_ROWS, rows)

    flat = lambda a: a.reshape((bsz * KV_GROUPS,) + a.shape[2:])
    o_t = _nsa(qn_t, qr_t, kc, vc_t, flat(ks), flat(vs_t), flat(kw), flat(vw_t), raw_gate, bsz, s)

    out = _merge_mlp(o_t, y_ssm, raw_gate, x2d, w_attn_branch[0].astype(BF16), w_o[0].astype(BF16),
                     g_norm_mlp[0], w_up[0].astype(BF16), w_down[0].astype(BF16), g_norm_final, s)
    return out.reshape(bsz, s, d)
```

```python
import functools

import jax
import jax.numpy as jnp
from jax import lax
from jax.experimental import pallas as pl
from jax.experimental.pallas import tpu as pltpu

F32 = jnp.float32
BF16 = jnp.bfloat16

D_MODEL = 1024
D_INNER = 2048
SSM_HEADS = 32
SSM_GROUPS = 8
SSM_HPG = 4
SSM_HEAD_DIM = 64
SSM_STATE = 128
SSM_CONV = 4
CHUNK = 128
SSM_XBC = 4096
HEAD_DIM = 64
ATT_HEADS = 16
KV_GROUPS = 4
ATT_HPG = 4
CMP_LEN = 32
CMP_STRIDE = 16
CMP_HIDDEN = 256
SEL_BLOCK = 64
SEL_TOPK = 16
WINDOW = 512
Q_BLOCK = 128
ROPE_THETA = 500000.0
ROPE_DIM = 16
MLP_HIDDEN = 4096
NORM_EPS = 1e-6
NEG = -1e30
BIG = 1e30
REMOVED = -3e38
LOG2E = 1.4426950408889634
SCALE = HEAD_DIM ** -0.5 * LOG2E
KEY_STEP = 256
NSA_GROUPS = 4

LANES = 128
ROW_TILE = 512
VMEM_LIMIT = 56 * 1024 * 1024

RAW_Z, RAW_XS, RAW_BC = 0, 2048, 4096
SSM_COLS, SSM_TN = 6144, 1536
GATE_MIX, GATE_MISC, GATE_COLS = 0, 2048, 2176
MISC_GATE0 = SSM_HEADS
V_ROWS = HEAD_DIM + 16


def _cparams(sem):
    return pltpu.CompilerParams(dimension_semantics=sem, vmem_limit_bytes=VMEM_LIMIT)


def _sigmoid(x):
    return 0.5 * jnp.tanh(0.5 * x) + 0.5


def _silu(x):
    h = 0.5 * x
    return h + h * jnp.tanh(h)


def _split3(x):
    hi = x.astype(BF16)
    r1 = x - hi.astype(F32)
    mid = r1.astype(BF16)
    lo = (r1 - mid.astype(F32)).astype(BF16)
    return hi, mid, lo


def _rmsnorm_kernel(x_ref, g_ref, o_ref):
    x = x_ref[...]
    ms = jnp.mean(x * x, axis=-1, keepdims=True)
    o_ref[...] = (x * lax.rsqrt(ms + NORM_EPS) * g_ref[...]).astype(o_ref.dtype)


def _rmsnorm(x2d, g):
    t, d = x2d.shape
    return pl.pallas_call(
        _rmsnorm_kernel,
        grid=(t // ROW_TILE,),
        in_specs=[pl.BlockSpec((ROW_TILE, d), lambda i: (i, 0)),
                  pl.BlockSpec((1, d), lambda i: (0, 0))],
        out_specs=pl.BlockSpec((ROW_TILE, d), lambda i: (i, 0)),
        out_shape=jax.ShapeDtypeStruct((t, d), BF16),
        compiler_params=_cparams(("parallel",)),
        name="rmsnorm",
    )(x2d, g.reshape(1, d))


def _proj_raw_kernel(a_ref, w_ref, o_ref):
    o_ref[...] = jnp.dot(a_ref[...], w_ref[...], preferred_element_type=F32)


def _proj_raw(h, w, n, tn, tm, name):
    t, k = h.shape
    return pl.pallas_call(
        _proj_raw_kernel,
        grid=(n // tn, t // tm),
        in_specs=[pl.BlockSpec((tm, k), lambda j, i: (i, 0)),
                  pl.BlockSpec((k, tn), lambda j, i: (0, j))],
        out_specs=pl.BlockSpec((tm, tn), lambda j, i: (i, j)),
        out_shape=jax.ShapeDtypeStruct((t, n), F32),
        compiler_params=_cparams(("parallel", "parallel")),
        name=name,
    )(h, w)


def _rope_t(t_t, cos_t, sin_t):
    half = ROPE_DIM // 2
    rows = []
    for base in range(0, t_t.shape[0], HEAD_DIM):
        t1, t2 = t_t[base:base + half], t_t[base + half:base + ROPE_DIM]
        rows += [t1 * cos_t - t2 * sin_t, t2 * cos_t + t1 * sin_t, t_t[base + ROPE_DIM:base + HEAD_DIM]]
    return jnp.concatenate(rows, axis=0)


def _proj_q_kernel(a_ref, w_ref, cos_ref, sin_ref, qn_ref, qr_ref):
    acc = jnp.dot(a_ref[...], w_ref[...], preferred_element_type=F32)
    cos_t, sin_t = cos_ref[...], sin_ref[...]
    for c in range(acc.shape[1] // LANES):
        t_t = acc[:, c * LANES:(c + 1) * LANES].T
        qn_ref[0, c * LANES:(c + 1) * LANES, :] = (t_t * SCALE).astype(qn_ref.dtype)
        qr_ref[0, c * LANES:(c + 1) * LANES, :] = (_rope_t(t_t, cos_t, sin_t) * SCALE).astype(qr_ref.dtype)


def _proj_q(h, w, tabs, bsz, s):
    t, k = h.shape
    n = w.shape[1]
    spb = s // ROW_TILE
    tab_spec = pl.BlockSpec((ROPE_DIM // 2, ROW_TILE), lambda i: (0, i))
    out_spec = pl.BlockSpec((1, n, ROW_TILE), lambda i: (i // spb, 0, i % spb))
    return pl.pallas_call(
        _proj_q_kernel,
        grid=(t // ROW_TILE,),
        in_specs=[pl.BlockSpec((ROW_TILE, k), lambda i: (i, 0)),
                  pl.BlockSpec((k, n), lambda i: (0, 0)),
                  tab_spec, tab_spec],
        out_specs=[out_spec, out_spec],
        out_shape=[jax.ShapeDtypeStruct((bsz, n, s), BF16)] * 2,
        compiler_params=_cparams(("parallel",)),
        name="proj_q",
    )(h, w, *tabs)


def _proj_kv_kernel(a_ref, w_ref, cos_ref, sin_ref, kvc_ref, ks_ref, kw_ref, vst_ref, vwt_ref, kvc_scr):
    acc = jnp.dot(a_ref[...], w_ref[...], preferred_element_type=F32)
    cos_t, sin_t = cos_ref[...], sin_ref[...]
    tm = acc.shape[0]
    gw = KV_GROUPS * HEAD_DIM
    rows = tm // CMP_STRIDE
    low = lax.broadcasted_iota(jnp.int32, (rows, LANES), 1) < HEAD_DIM
    for c in range(2 * gw // LANES):
        kvc_scr[c] = acc[:, c * LANES:(c + 1) * LANES]
        for j in range(CMP_STRIDE // 2):
            e = kvc_scr[c, pl.ds(2 * j, rows, stride=CMP_STRIDE), :]
            o = kvc_scr[c, pl.ds(2 * j + 1, rows, stride=CMP_STRIDE), :]
            kvc_ref[0, 2 * c, :, j * LANES:(j + 1) * LANES] = jnp.where(low, e, pltpu.roll(o, HEAD_DIM, 1))
            kvc_ref[0, 2 * c + 1, :, j * LANES:(j + 1) * LANES] = jnp.where(low, pltpu.roll(e, HEAD_DIM, 1), o)
    lane = lax.broadcasted_iota(jnp.int32, (tm, LANES), 1)
    key_pad = jnp.where(lane == HEAD_DIM, 1.0, 0.0)
    ones_rows = jnp.where(lax.broadcasted_iota(jnp.int32, (V_ROWS - HEAD_DIM, tm), 0) == 0, 1.0, 0.0)
    for base, k_ref, vt_ref in ((2 * gw, ks_ref, vst_ref), (4 * gw, kw_ref, vwt_ref)):
        vtile = vt_ref.shape[-1]
        for c in range(gw // LANES):
            kk = _rope_t(acc[:, base + c * LANES: base + (c + 1) * LANES].T, cos_t, sin_t).T
            vv = acc[:, base + gw + c * LANES: base + gw + (c + 1) * LANES].T
            for half in range(2):
                g = 2 * c + half
                k_lo = kk if half == 0 else pltpu.roll(kk, HEAD_DIM, 1)
                k_ref[0, g] = jnp.where(lane < HEAD_DIM, k_lo, key_pad).astype(k_ref.dtype)
                v_aug = jnp.concatenate([vv[half * HEAD_DIM:(half + 1) * HEAD_DIM], ones_rows], axis=0)
                for j in range(ROW_TILE // vtile):
                    vt_ref[0, g, j] = v_aug[:, j * vtile:(j + 1) * vtile].astype(vt_ref.dtype)


def _proj_kv(h, w, tabs, bsz, s):
    t, k = h.shape
    n = w.shape[1]
    spb = s // ROW_TILE
    tab_spec = pl.BlockSpec((ROPE_DIM // 2, ROW_TILE), lambda i: (0, i))
    k_spec = pl.BlockSpec((1, KV_GROUPS, ROW_TILE, LANES), lambda i: (i // spb, 0, i % spb, 0))
    k_shape = jax.ShapeDtypeStruct((bsz, KV_GROUPS, s, LANES), BF16)

    def vt(tile):
        return (pl.BlockSpec((1, KV_GROUPS, ROW_TILE // tile, V_ROWS, tile), lambda i: (i // spb, 0, i % spb, 0, 0)),
                jax.ShapeDtypeStruct((bsz, KV_GROUPS, s // tile, V_ROWS, tile), BF16))

    (vs_spec, vs_shape), (vw_spec, vw_shape) = vt(KEY_STEP), vt(Q_BLOCK)
    return pl.pallas_call(
        _proj_kv_kernel,
        grid=(t // ROW_TILE,),
        in_specs=[pl.BlockSpec((ROW_TILE, k), lambda i: (i, 0)),
                  pl.BlockSpec((k, n), lambda i: (0, 0)),
                  tab_spec, tab_spec],
        out_specs=[pl.BlockSpec((1, 2 * KV_GROUPS, ROW_TILE // CMP_STRIDE, CMP_STRIDE * HEAD_DIM),
                                lambda i: (i // spb, 0, i % spb, 0)),
                   k_spec, k_spec, vs_spec, vw_spec],
        out_shape=[jax.ShapeDtypeStruct((bsz, 2 * KV_GROUPS, s // CMP_STRIDE, CMP_STRIDE * HEAD_DIM), F32),
                   k_shape, k_shape, vs_shape, vw_shape],
        scratch_shapes=[pltpu.VMEM((2 * KV_GROUPS * HEAD_DIM // LANES, ROW_TILE, LANES), F32)],
        compiler_params=_cparams(("parallel",)),
        name="proj_kv",
    )(h, w, *tabs)


def _ssd_kernel(z_ref, xs_ref, bc_ref, misc_ref, convw_ref, convb_ref, dtb_ref, alog_ref,
                dskip_ref, gnorm_ref, selb_ref, sele_ref, wout_ref, o_ref,
                tail_ref, act_ref, state_ref):
    c = pl.program_id(1)
    L = CHUNK
    gw = SSM_HPG * SSM_HEAD_DIM

    @pl.when(c == 0)
    def _():
        state_ref[...] = jnp.zeros_like(state_ref)
        tail_ref[...] = jnp.zeros_like(tail_ref)

    cw = 512
    first_row = lax.broadcasted_iota(jnp.int32, (8, cw), 0) == 0
    for cc in range(SSM_XBC // cw):
        cols = slice(cc * cw, (cc + 1) * cw)
        u = xs_ref[:, cols] if cc < D_INNER // cw else bc_ref[:, cc * cw - D_INNER:(cc + 1) * cw - D_INNER]
        a = convw_ref[0:1, cols] * u
        for k in range(1, SSM_CONV):
            shifted = pltpu.roll(a, 1, 0)
            shifted = jnp.concatenate([jnp.where(first_row, tail_ref[k - 1:k, cols], shifted[0:8]), shifted[8:]], axis=0)
            tail_ref[k - 1:k, cols] = a[L - 1:L]
            a = convw_ref[k:k + 1, cols] * u + shifted
        act_ref[:, cols] = _silu(a + convb_ref[:, cols])

    lane = lax.broadcasted_iota(jnp.int32, (L, LANES), 1)
    row = lax.broadcasted_iota(jnp.int32, (L, LANES), 0)
    head_lane = lane < SSM_HEADS
    raw = misc_ref[...] + dtb_ref[...]
    dt = jnp.where(head_lane, jnp.maximum(raw, 0.0) + jnp.log1p(jnp.exp(-jnp.abs(raw))), 0.0)
    a_row = jnp.where(head_lane[0:1], -jnp.exp(alog_ref[...]), 0.0)
    cs = dt * a_row
    sh = 1
    while sh < L:
        cs = cs + jnp.where(row >= sh, pltpu.roll(cs, sh, 0), 0.0)
        sh *= 2
    tot = cs[L - 1:L, :]
    w_state = dt * jnp.exp(tot - cs)
    cs_t, dt_t, w_t = cs.T, dt.T, w_state.T

    hi, mid, lo = _split3(cs)
    packed = (hi.astype(F32) + pltpu.roll(mid.astype(F32), SSM_HEADS, 1)
              + pltpu.roll(lo.astype(F32), 2 * SSM_HEADS, 1)).astype(BF16)

    tri = lax.broadcasted_iota(jnp.int32, (L, L), 0) >= lax.broadcasted_iota(jnp.int32, (L, L), 1)
    brow = lax.broadcasted_iota(jnp.int32, (SSM_HPG * L, gw), 0) // L
    bcol = lax.broadcasted_iota(jnp.int32, (SSM_HPG * L, gw), 1) // SSM_HEAD_DIM
    blockmask = brow == bcol

    for g in range(SSM_GROUPS):
        col_b = jnp.dot(packed, selb_ref[:, g * SSM_HPG * L:(g + 1) * SSM_HPG * L], preferred_element_type=F32)
        exp_e = jnp.exp(jnp.dot(packed, sele_ref[:, g * gw:(g + 1) * gw], preferred_element_type=F32))
        etot_e = exp_e[L - 1:L, :]
        xs_g = act_ref[:, g * gw:(g + 1) * gw]
        b_g = act_ref[:, D_INNER + g * SSM_STATE: D_INNER + (g + 1) * SSM_STATE]
        c_g = act_ref[:, D_INNER + SSM_GROUPS * SSM_STATE + g * SSM_STATE:
                      D_INNER + SSM_GROUPS * SSM_STATE + (g + 1) * SSM_STATE]
        c_bf = c_g.astype(BF16)
        cb = lax.dot_general(c_bf, b_g.astype(BF16), (((1,), (1,)), ((), ())),
                             preferred_element_type=F32)
        b_t = b_g.T
        tops, bots = [], []
        for j in range(SSM_HPG):
            h = g * SSM_HPG + j
            decay = jnp.exp(jnp.where(tri, col_b[:, j * L:(j + 1) * L] - cs_t[h:h + 1, :], NEG))
            tops.append((cb * decay * dt_t[h:h + 1, :]).astype(BF16))
            bots.append((b_t * w_t[h:h + 1, :]).astype(BF16))
        lhs = jnp.concatenate([jnp.concatenate(tops, axis=1), jnp.concatenate(bots, axis=1)], axis=0)
        xs_bf = xs_g.astype(BF16)
        xs_bd = jnp.where(blockmask, jnp.concatenate([xs_bf] * SSM_HPG, axis=0), jnp.zeros((), BF16))
        res = jnp.dot(lhs, xs_bd, preferred_element_type=F32)
        st = state_ref[g]
        y_off = jnp.dot(c_bf, st.astype(BF16), preferred_element_type=F32) * exp_e
        state_ref[g] = st * etot_e + res[L:2 * L]
        y = res[0:L] + y_off + dskip_ref[:, g * gw:(g + 1) * gw] * xs_g
        zg = z_ref[:, g * gw:(g + 1) * gw]
        y = y * _silu(zg)
        ms = jnp.mean(y * y, axis=-1, keepdims=True)
        act_ref[:, g * gw:(g + 1) * gw] = y * lax.rsqrt(ms + NORM_EPS) * gnorm_ref[:, g * gw:(g + 1) * gw]

    o_ref[...] = jnp.dot(act_ref[:, 0:D_INNER].astype(BF16), wout_ref[...], preferred_element_type=F32)


def _ssd(raw, raw_gate, conv_w, conv_b, dt_bias, a_log, d_skip, g_ssm_norm, w_ssm, bsz, s):
    t = raw.shape[0]
    nc = s // CHUNK
    blk = lambda cb: pl.BlockSpec((CHUNK, 2048), lambda b, c, cb=cb: (b * nc + c, cb))
    const = lambda shape: pl.BlockSpec(shape, lambda b, c: (0,) * len(shape))
    k = jnp.arange(LANES)
    selb = ((k[:, None] < 3 * SSM_HEADS) & ((k[:, None] % SSM_HEADS) == (jnp.arange(SSM_HEADS * LANES)[None, :] // LANES)))
    sele = ((k[:, None] < 3 * SSM_HEADS) & ((k[:, None] % SSM_HEADS) == (jnp.arange(D_INNER)[None, :] // SSM_HEAD_DIM)))
    pad = lambda v: jnp.pad(v.astype(F32), (0, LANES - SSM_HEADS)).reshape(1, LANES)
    return pl.pallas_call(
        _ssd_kernel,
        grid=(bsz, nc),
        in_specs=[blk(RAW_Z // 2048), blk(RAW_XS // 2048), blk(RAW_BC // 2048),
                  pl.BlockSpec((CHUNK, LANES), lambda b, c: (b * nc + c, GATE_MISC // LANES)),
                  const((SSM_CONV, SSM_XBC)), const((1, SSM_XBC)), const((1, LANES)), const((1, LANES)),
                  const((1, D_INNER)), const((1, D_INNER)),
                  const((LANES, SSM_HEADS * LANES)), const((LANES, D_INNER)),
                  const((D_INNER, D_MODEL))],
        out_specs=pl.BlockSpec((CHUNK, D_MODEL), lambda b, c: (b * nc + c, 0)),
        out_shape=jax.ShapeDtypeStruct((t, D_MODEL), F32),
        scratch_shapes=[pltpu.VMEM((8, SSM_XBC), F32),
                        pltpu.VMEM((CHUNK, SSM_XBC), F32),
                        pltpu.VMEM((SSM_GROUPS, SSM_STATE, SSM_HPG * SSM_HEAD_DIM), F32)],
        compiler_params=_cparams(("parallel", "arbitrary")),
        name="ssd_mixer",
    )(raw, raw, raw, raw_gate, conv_w.astype(F32), conv_b.reshape(1, SSM_XBC).astype(F32),
      pad(dt_bias), pad(a_log), jnp.repeat(d_skip.astype(F32), SSM_HEAD_DIM).reshape(1, D_INNER),
      g_ssm_norm.reshape(1, D_INNER).astype(F32), selb.astype(BF16), sele.astype(BF16), w_ssm.astype(BF16))


def _compress_kernel(u_ref, pelo_ref, pehi_ref, w1_ref, w2_ref, o_ref):
    u = u_ref[0]
    half = CMP_STRIDE * HEAD_DIM
    a = jnp.dot((u + pelo_ref[0, 0]).astype(BF16), w1_ref[0, 0:half, :], preferred_element_type=F32)
    b = jnp.dot((u + pehi_ref[0, 0]).astype(BF16), w1_ref[0, half:2 * half, :], preferred_element_type=F32)
    pre = a + pltpu.roll(b, u.shape[0] - 1, 0)
    hidden = _silu(pre)
    out = jnp.dot(hidden.astype(BF16), w2_ref[0], preferred_element_type=F32)
    lane = lax.broadcasted_iota(jnp.int32, out.shape, 1)
    o_ref[0] = jnp.where(lane == HEAD_DIM, 1.0, out).astype(o_ref.dtype)


def _compress(u, pe, w1, w2):
    n, rows, width = u.shape
    kv_of = lambda i: (i // KV_GROUPS) % 2
    return pl.pallas_call(
        _compress_kernel,
        grid=(n,),
        in_specs=[pl.BlockSpec((1, rows, width), lambda i: (i, 0, 0)),
                  pl.BlockSpec((1, 1, 1, width), lambda i: (kv_of(i), 0, 0, 0)),
                  pl.BlockSpec((1, 1, 1, width), lambda i: (kv_of(i), 1, 0, 0)),
                  pl.BlockSpec((1, 2 * width, CMP_HIDDEN), lambda i: (kv_of(i), 0, 0)),
                  pl.BlockSpec((1, CMP_HIDDEN, LANES), lambda i: (kv_of(i), 0, 0))],
        out_specs=pl.BlockSpec((1, rows, LANES), lambda i: (i, 0, 0)),
        out_shape=jax.ShapeDtypeStruct((n, rows, LANES), BF16),
        compiler_params=_cparams(("parallel",)),
        name="compress",
    )(u, pe, pe, w1, w2)


def _nsa_kernel(qn_ref, qr_ref, kc_ref, vct_ref, ks_ref, vst_ref, kw_ref, vwt_ref, gate_ref, ovt_ref, oh_ref,
                eye_ref, wlo_ref, whi_ref,
                o_ref, gt_ref, qa_ref, sa_ref, sb_ref, pa_ref, pb_ref, sc_ref, accc_ref, invc_ref, imp_ref, sw_ref,
                *, topk):
    gp = pl.program_id(1)
    qb = pl.program_id(2)
    groups = range(NSA_GROUPS)
    nq = ATT_HPG * Q_BLOCK
    ncmp = kc_ref.shape[1]
    nsel = ovt_ref.shape[0]
    q0 = qb * Q_BLOCK
    tq = q0 + lax.broadcasted_iota(jnp.int32, (1, nq), 1) % Q_BLOCK

    def heads_on_lanes(ref, gi):
        base = gi * ATT_HPG * HEAD_DIM
        return jnp.concatenate([ref[0, base + r * HEAD_DIM:base + (r + 1) * HEAD_DIM, :]
                                for r in range(ATT_HPG)], axis=1)

    qn_t = [heads_on_lanes(qn_ref, gi) for gi in groups]
    qr_t = [heads_on_lanes(qr_ref, gi) for gi in groups]

    def q_operand(q_t, flag, table):
        flag_rows = jnp.concatenate([flag, jnp.zeros((LANES - HEAD_DIM - 1, nq), F32)], axis=0).astype(BF16)
        return jnp.concatenate([q_t, flag_rows] + ([table] if table is not None else []), axis=0)

    no_flag = jnp.zeros((1, nq), F32)

    n_chunks = ncmp // LANES
    chunk_span = LANES * CMP_STRIDE
    c_hi = jnp.minimum((q0 + Q_BLOCK - CMP_LEN) // chunk_span, n_chunks - 1)

    def compressed_branch(n_live):
        live = n_live * LANES
        ov_live = jnp.concatenate([ovt_ref[:, piece * ncmp:piece * ncmp + live] for piece in range(3)], axis=1)
        for c in range(n_live):
            for gi in groups:
                s = jnp.dot(kc_ref[gi, c * LANES:(c + 1) * LANES, :], q_operand(qn_t[gi], no_flag, None),
                            preferred_element_type=F32)
                if c >= n_live - 2:
                    ends = (c * chunk_span + CMP_LEN - 1
                            + lax.broadcasted_iota(jnp.int32, (LANES, nq), 0) * CMP_STRIDE)
                    s = jnp.where(ends <= tq, s, NEG)
                sc_ref[gi, c * LANES:(c + 1) * LANES, :] = s
        for gi in groups:
            s_c = sc_ref[gi, 0:live, :]
            m_c = jnp.max(s_c, axis=0, keepdims=True)
            p_c = jnp.exp2(s_c - m_c)
            acc = jnp.dot(vct_ref[gi, :, 0:live], p_c.astype(BF16), preferred_element_type=F32)
            inv = jnp.where(m_c > 0.5 * NEG, 1.0 / acc[HEAD_DIM:HEAD_DIM + 1], 0.0)
            accc_ref[gi] = acc
            invc_ref[gi] = jnp.broadcast_to(inv, (8, nq))
            p_sum = p_c[:, 0:Q_BLOCK] * inv[:, 0:Q_BLOCK]
            for r in range(1, ATT_HPG):
                p_sum = p_sum + p_c[:, r * Q_BLOCK:(r + 1) * Q_BLOCK] * inv[:, r * Q_BLOCK:(r + 1) * Q_BLOCK]
            imp_ref[gi] = jnp.dot(ov_live, jnp.concatenate(_split3(p_sum), axis=0), preferred_element_type=F32)

    for k in range(n_chunks):
        pl.when(c_hi == k)(functools.partial(compressed_branch, k + 1))
    acc_c = [accc_ref[gi] for gi in groups]
    inv_c = [invc_ref[gi, 0:1, :] for gi in groups]
    imp = [imp_ref[gi] for gi in groups]

    n_wt = WINDOW // Q_BLOCK + 1
    eye = eye_ref[...]
    for gi in groups:
        for i in range(n_wt):
            kt = qb - (n_wt - 1) + i
            k0 = pl.multiple_of(jnp.maximum(kt, 0) * Q_BLOCK, Q_BLOCK)
            keys = kw_ref[gi, pl.ds(k0, Q_BLOCK), :]
            flag = jnp.where(kt >= 0, no_flag, NEG)
            if i == 0:
                s = jnp.dot(jnp.concatenate([keys, eye], axis=1), q_operand(qr_t[gi], flag, wlo_ref[...]),
                            preferred_element_type=F32)
            elif i == n_wt - 1:
                s = jnp.dot(jnp.concatenate([keys, eye], axis=1), q_operand(qr_t[gi], flag, whi_ref[...]),
                            preferred_element_type=F32)
            else:
                s = jnp.dot(keys, q_operand(qr_t[gi], flag, None), preferred_element_type=F32)
            sw_ref[gi, i * Q_BLOCK:(i + 1) * Q_BLOCK, :] = s
    acc_w = []
    for gi in groups:
        v_w = [vwt_ref[gi, jnp.maximum(qb - (n_wt - 1) + i, 0)] for i in range(n_wt)]
        s_w = sw_ref[gi]
        p_w = jnp.exp2(s_w - jnp.max(s_w, axis=0, keepdims=True))
        acc_w.append(jnp.dot(jnp.concatenate(v_w, axis=1), p_w.astype(BF16), preferred_element_type=F32))

    blk = lax.broadcasted_iota(jnp.int32, (nsel, Q_BLOCK), 0)
    cur = (q0 + lax.broadcasted_iota(jnp.int32, (nsel, Q_BLOCK), 1)) // SEL_BLOCK
    valid = blk <= cur
    forced = valid & ((blk == 0) | (blk == cur) | (blk == cur - 1))
    val = [jnp.where(forced, REMOVED, jnp.where(valid, imp[gi], NEG)) for gi in groups]
    blk_f = blk.astype(F32)
    for _ in range(topk - 3):
        for gi in groups:
            mx = jnp.max(val[gi], axis=0, keepdims=True)
            first = jnp.min(jnp.where(val[gi] == mx, blk_f, float(nsel)), axis=0, keepdims=True)
            val[gi] = jnp.where(blk_f == first, REMOVED, val[gi])

    pad_rows = [jnp.zeros((LANES - nsel, nq), F32)] if nsel < LANES else []
    for gi in groups:
        bias = jnp.concatenate([jnp.where(val[gi] == REMOVED, 0.0, NEG)] * ATT_HPG, axis=1)
        qa_ref[gi] = q_operand(qr_t[gi], no_flag, jnp.concatenate([bias] + pad_rows, axis=0).astype(BF16))

    def sel_scores(gi, i):
        k0 = pl.multiple_of(i * KEY_STEP, KEY_STEP)
        lhs = jnp.concatenate([ks_ref[gi, pl.ds(k0, KEY_STEP), :], oh_ref[pl.ds(k0, KEY_STEP), :]], axis=1)
        return jnp.dot(lhs, qa_ref[gi], preferred_element_type=F32)

    row_k = lax.broadcasted_iota(jnp.int32, (KEY_STEP, nq), 0)

    def softmax_step(s_ref, p_ref, gi, m, first_key):
        s = s_ref[gi]
        if first_key is not None:
            s = jnp.where(row_k <= tq - first_key, s, NEG)
        m_new = jnp.maximum(m, jnp.max(s, axis=0, keepdims=True))
        p_ref[gi] = jnp.exp2(s - m_new).astype(BF16)
        return m_new, jnp.exp2(m - m_new)

    def pv(gi, acc, alpha, p_ref, i):
        return acc * alpha + jnp.dot(vst_ref[gi, i], p_ref[gi], preferred_element_type=F32)

    def sel_pair(j, carries, final):
        a = 2 * j
        m, acc, alpha_b = ([c[k] for c in carries] for k in range(3))
        alpha_a = [None] * NSA_GROUPS
        for gi in groups:
            sb_ref[gi] = sel_scores(gi, a + 1)
            acc[gi] = pv(gi, acc[gi], alpha_b[gi], pb_ref, jnp.maximum(a - 1, 0))
            m[gi], alpha_a[gi] = softmax_step(sa_ref, pa_ref, gi, m[gi], a * KEY_STEP if final else None)
            if not final:
                sa_ref[gi] = sel_scores(gi, a + 2)
            acc[gi] = pv(gi, acc[gi], alpha_a[gi], pa_ref, a)
            m[gi], alpha_b[gi] = softmax_step(sb_ref, pb_ref, gi, m[gi], (a + 1) * KEY_STEP if final else None)
        return tuple((m[gi], acc[gi], alpha_b[gi]) for gi in groups)

    n_pairs = (q0 + Q_BLOCK - 1) // (2 * KEY_STEP) + 1
    for gi in groups:
        sa_ref[gi] = sel_scores(gi, 0)
    pb_ref[...] = jnp.zeros_like(pb_ref)
    init = tuple((jnp.full((1, nq), NEG, F32), jnp.zeros((V_ROWS, nq), F32), jnp.ones((1, nq), F32))
                 for gi in groups)
    carries = lax.fori_loop(0, n_pairs - 1, lambda j, c: sel_pair(j, c, False), init)
    carries = sel_pair(n_pairs - 1, carries, True)
    acc_s = [pv(gi, carries[gi][1], carries[gi][2], pb_ref, 2 * n_pairs - 1) for gi in groups]

    gt_ref[...] = _sigmoid(gate_ref[...]).T
    hd = HEAD_DIM
    for gi in groups:
        def gate_row(branch, gi=gi):
            base = MISC_GATE0 + branch * ATT_HEADS + (gp * NSA_GROUPS + gi) * ATT_HPG
            return jnp.concatenate([gt_ref[pl.ds(base + r, 1), :] for r in range(ATT_HPG)], axis=1)

        o = ((gate_row(0) * inv_c[gi]) * acc_c[gi][0:hd] + (gate_row(1) / acc_s[gi][hd:hd + 1]) * acc_s[gi][0:hd]
             + (gate_row(2) / acc_w[gi][hd:hd + 1]) * acc_w[gi][0:hd])
        for r in range(ATT_HPG):
            row0 = (gi * ATT_HPG + r) * HEAD_DIM
            o_ref[0, row0:row0 + HEAD_DIM, :] = o[:, r * Q_BLOCK:(r + 1) * Q_BLOCK].astype(o_ref.dtype)


def _nsa(qn_t, qr_t, kc, vc_t, ks, vs_t, kw, vw_t, raw_gate, bsz, s):
    nqb = s // Q_BLOCK
    ncmp = s // CMP_STRIDE
    nsel = s // SEL_BLOCK
    topk = min(SEL_TOPK, nsel)
    gw = ATT_HPG * HEAD_DIM
    ci = jnp.arange(ncmp)[None, :]
    sj = jnp.arange(nsel)[:, None]
    ov_t = (ci * CMP_STRIDE < (sj + 1) * SEL_BLOCK) & (ci * CMP_STRIDE + CMP_LEN > sj * SEL_BLOCK) \
        & (ci < ncmp - 1)
    ov_t3 = jnp.concatenate([ov_t.astype(BF16)] * 3, axis=1)
    assert nsel <= LANES
    onehot = (jnp.arange(s)[:, None] // SEL_BLOCK == jnp.arange(LANES)[None, :]).astype(BF16)
    eye = jnp.eye(Q_BLOCK, dtype=BF16)
    key_row = jnp.arange(Q_BLOCK)[:, None]
    q_lane = jnp.arange(ATT_HPG * Q_BLOCK)[None, :] % Q_BLOCK
    win_lo = jnp.where(key_row > q_lane, 0.0, NEG).astype(BF16)
    win_hi = jnp.where(key_row <= q_lane, 0.0, NEG).astype(BF16)
    once = pl.Buffered(1)
    const2 = lambda shape: pl.BlockSpec(shape, lambda b, g, i: (0, 0), pipeline_mode=once)
    ng = NSA_GROUPS
    gpb = KV_GROUPS // ng
    q_spec = pl.BlockSpec((1, ng * gw, Q_BLOCK), lambda b, g, i: (b, g, i))
    per_bg = lambda shape: pl.BlockSpec((ng,) + shape, lambda b, g, i: (b * gpb + g,) + (0,) * len(shape),
                                        pipeline_mode=once)
    nql = ATT_HPG * Q_BLOCK
    return pl.pallas_call(
        functools.partial(_nsa_kernel, topk=topk),
        grid=(bsz, gpb, nqb),
        in_specs=[q_spec, q_spec,
                  per_bg((ncmp, LANES)), per_bg((V_ROWS, ncmp)),
                  per_bg((s, LANES)), per_bg((s // KEY_STEP, V_ROWS, KEY_STEP)),
                  per_bg((s, LANES)), per_bg((s // Q_BLOCK, V_ROWS, Q_BLOCK)),
                  pl.BlockSpec((Q_BLOCK, LANES), lambda b, g, i: (b * nqb + i, GATE_MISC // LANES)),
                  const2((nsel, 3 * ncmp)), const2((s, LANES)),
                  const2((Q_BLOCK, Q_BLOCK)), const2((Q_BLOCK, ATT_HPG * Q_BLOCK)),
                  const2((Q_BLOCK, ATT_HPG * Q_BLOCK))],
        out_specs=pl.BlockSpec((1, ng * gw, Q_BLOCK), lambda b, g, i: (b, g, i)),
        out_shape=jax.ShapeDtypeStruct((bsz, ATT_HEADS * HEAD_DIM, s), BF16),
        scratch_shapes=[pltpu.VMEM((LANES, Q_BLOCK), F32),
                        pltpu.VMEM((ng, 2 * LANES, nql), BF16),
                        pltpu.VMEM((ng, KEY_STEP, nql), F32),
                        pltpu.VMEM((ng, KEY_STEP, nql), F32),
                        pltpu.VMEM((ng, KEY_STEP, nql), BF16),
                        pltpu.VMEM((ng, KEY_STEP, nql), BF16),
                        pltpu.VMEM((ng, ncmp, nql), F32),
                        pltpu.VMEM((ng, V_ROWS, nql), F32),
                        pltpu.VMEM((ng, 8, nql), F32),
                        pltpu.VMEM((ng, nsel, Q_BLOCK), F32),
                        pltpu.VMEM((ng, (WINDOW // Q_BLOCK + 1) * Q_BLOCK, nql), F32)],
        compiler_params=_cparams(("parallel", "parallel", "arbitrary")),
        name="nsa_sweep",
    )(qn_t, qr_t, kc, vc_t, ks, vs_t, kw, vw_t, raw_gate, ov_t3, onehot, eye, win_lo, win_hi)


def _merge_mlp_kernel(ot_ref, yssm_ref, mix_ref, x_ref, wab_ref, wo_ref, g1_ref, wup_ref, wdn_ref, g2_ref, o_ref):
    o = ot_ref[0].astype(F32).T.astype(BF16)
    y_att = jnp.dot(o, wab_ref[...], preferred_element_type=F32)
    gates = _sigmoid(mix_ref[...])
    mixed = gates[:, 0:D_MODEL] * yssm_ref[...] + gates[:, D_MODEL:2 * D_MODEL] * y_att
    x = x_ref[...] + jnp.dot(mixed.astype(BF16), wo_ref[...], preferred_element_type=F32)
    ms = jnp.mean(x * x, axis=-1, keepdims=True)
    h = (x * lax.rsqrt(ms + NORM_EPS) * g1_ref[...]).astype(BF16)
    up = jnp.maximum(jnp.dot(h, wup_ref[...], preferred_element_type=F32), 0.0)
    y = x + jnp.dot((up * up).astype(BF16), wdn_ref[...], preferred_element_type=F32)
    ms2 = jnp.mean(y * y, axis=-1, keepdims=True)
    o_ref[...] = y * lax.rsqrt(ms2 + NORM_EPS) * g2_ref[...]


def _merge_mlp(o_t, y_ssm, raw_gate, x2d, w_ab, w_o, g1, w_up, w_down, g2, s):
    t = x2d.shape[0]
    tm = ROW_TILE
    spb = s // tm
    row_spec = pl.BlockSpec((tm, D_MODEL), lambda i: (i, 0))
    const = lambda shape: pl.BlockSpec(shape, lambda i: (0, 0), pipeline_mode=pl.Buffered(1))
    return pl.pallas_call(
        _merge_mlp_kernel,
        grid=(t // tm,),
        in_specs=[pl.BlockSpec((1, D_MODEL, tm), lambda i: (i // spb, 0, i % spb)),
                  row_spec,
                  pl.BlockSpec((tm, 2 * D_MODEL), lambda i: (i, GATE_MIX // (2 * D_MODEL))),
                  row_spec, const((D_MODEL, D_MODEL)), const((D_MODEL, D_MODEL)),
                  const((1, D_MODEL)), const((D_MODEL, MLP_HIDDEN)), const((MLP_HIDDEN, D_MODEL)),
                  const((1, D_MODEL))],
        out_specs=row_spec,
        out_shape=jax.ShapeDtypeStruct((t, D_MODEL), F32),
        compiler_params=_cparams(("parallel",)),
        name="merge_mlp",
    )(o_t, y_ssm, raw_gate, x2d, w_ab, w_o, g1.reshape(1, D_MODEL), w_up, w_down, g2.reshape(1, D_MODEL))


def _rope_tables(positions):
    inv_freq = ROPE_THETA ** (-jnp.arange(0, ROPE_DIM, 2, dtype=F32) / ROPE_DIM)
    ang = positions.astype(F32).reshape(1, -1) * inv_freq[:, None]
    return jnp.cos(ang), jnp.sin(ang)


def kernel(x, positions, g_norm_mix, w_in, conv_w, conv_b, dt_bias, a_log, d_skip, g_ssm_norm, w_ssm_branch, cmp_pe_k, cmp_pe_v, w_cmp_k1, w_cmp_k2, w_cmp_v1, w_cmp_v2, w_attn_branch, w_o, g_norm_mlp, w_up, w_down, g_norm_final):
    bsz, s, d = x.shape
    assert d == D_MODEL and s % (2 * ROW_TILE) == 0 and w_in.shape[0] == 1
    t = bsz * s
    x2d = x.reshape(t, d)
    tabs = _rope_tables(positions)

    w = w_in[0].astype(BF16)
    o_dt, o_q, o_kv, o_ag, o_mg = 6144, 6176, 7200, 8736, 8784
    misc_pad = jnp.zeros((d, GATE_COLS - GATE_MISC - SSM_HEADS - 3 * ATT_HEADS), BF16)
    w_gate = jnp.concatenate([w[:, o_mg:], w[:, o_dt:o_q], w[:, o_ag:o_mg], misc_pad], axis=1)
    w_q = w[:, o_q:o_kv]
    w_kv = w[:, o_kv:o_ag]

    h = _rmsnorm(x2d, g_norm_mix[0])
    raw = _proj_raw(h, w, SSM_COLS, SSM_TN, 2 * ROW_TILE, "proj_ssm")
    raw_gate = _proj_raw(h, w_gate, GATE_COLS, GATE_COLS, 2 * ROW_TILE, "proj_gate")
    qn_t, qr_t = _proj_q(h, w_q, tabs, bsz, s)
    kvc, ks, kw, vs_t, vw_t = _proj_kv(h, w_kv, tabs, bsz, s)

    y_ssm = _ssd(raw, raw_gate, conv_w[0], conv_b[0], dt_bias[0], a_log[0], d_skip[0], g_ssm_norm[0],
                 w_ssm_branch[0], bsz, s)

    rows = s // CMP_STRIDE
    u = kvc.reshape(bsz * 2 * KV_GROUPS, rows, CMP_STRIDE * HEAD_DIM)
    pe = jnp.stack([cmp_pe_k[0], cmp_pe_v[0]]).astype(F32).reshape(2, 2, 1, CMP_STRIDE * HEAD_DIM)
    w1 = jnp.stack([w_cmp_k1[0], w_cmp_v1[0]]).astype(BF16)
    w2 = jnp.pad(jnp.stack([w_cmp_k2[0], w_cmp_v2[0]]).astype(BF16), ((0, 0), (0, 0), (0, LANES - HEAD_DIM)))
    cmp = _compress(u, pe, w1, w2).reshape(bsz, 2, KV_GROUPS, rows, LANES)
    kc = cmp[:, 0].reshape(bsz * KV_GROUPS, rows, LANES)
    vc_t = jnp.swapaxes(cmp[:, 1], -1, -2)[:, :, :V_ROWS].reshape(bsz * KV_GROUPS, V_ROWS, rows)

    flat = lambda a: a.reshape((bsz * KV_GROUPS,) + a.shape[2:])
    o_t = _nsa(qn_t, qr_t, kc, vc_t, flat(ks), flat(vs_t), flat(kw), flat(vw_t), raw_gate, bsz, s)

    out = _merge_mlp(o_t, y_ssm, raw_gate, x2d, w_attn_branch[0].astype(BF16), w_o[0].astype(BF16),
                     g_norm_mlp[0], w_up[0].astype(BF16), w_down[0].astype(BF16), g_norm_final, s)
    return out.reshape(bsz, s, d)
```

```python
import functools

import jax
import jax.numpy as jnp
from jax import lax
from jax.experimental import pallas as pl
from jax.experimental.pallas import tpu as pltpu

F32 = jnp.float32
BF16 = jnp.bfloat16

D_MODEL = 1024
D_INNER = 2048
SSM_HEADS = 32
SSM_GROUPS = 8
SSM_HPG = 4
SSM_HEAD_DIM = 64
SSM_STATE = 128
SSM_CONV = 4
CHUNK = 128
SSM_XBC = 4096
HEAD_DIM = 64
ATT_HEADS = 16
KV_GROUPS = 4
ATT_HPG = 4
CMP_LEN = 32
CMP_STRIDE = 16
CMP_HIDDEN = 256
SEL_BLOCK = 64
SEL_TOPK = 16
WINDOW = 512
Q_BLOCK = 128
ROPE_THETA = 500000.0
ROPE_DIM = 16
MLP_HIDDEN = 4096
NORM_EPS = 1e-6
NEG = -1e30
BIG = 1e30
REMOVED = -3e38
LOG2E = 1.4426950408889634
SCALE = HEAD_DIM ** -0.5 * LOG2E
KEY_STEP = 256
NSA_GROUPS = 4

LANES = 128
ROW_TILE = 512
VMEM_LIMIT = 56 * 1024 * 1024

RAW_Z, RAW_XS, RAW_BC = 0, 2048, 4096
SSM_COLS, SSM_TN = 6144, 1536
GATE_MIX, GATE_MISC, GATE_COLS = 0, 2048, 2176
MISC_GATE0 = SSM_HEADS
V_ROWS = HEAD_DIM + 16


def _cparams(sem):
    return pltpu.CompilerParams(dimension_semantics=sem, vmem_limit_bytes=VMEM_LIMIT)


def _sigmoid(x):
    return 0.5 * jnp.tanh(0.5 * x) + 0.5


def _silu(x):
    h = 0.5 * x
    return h + h * jnp.tanh(h)


def _split3(x):
    hi = x.astype(BF16)
    r1 = x - hi.astype(F32)
    mid = r1.astype(BF16)
    lo = (r1 - mid.astype(F32)).astype(BF16)
    return hi, mid, lo


def _rmsnorm_kernel(x_ref, g_ref, o_ref):
    x = x_ref[...]
    ms = jnp.mean(x * x, axis=-1, keepdims=True)
    o_ref[...] = (x * lax.rsqrt(ms + NORM_EPS) * g_ref[...]).astype(o_ref.dtype)


def _rmsnorm(x2d, g):
    t, d = x2d.shape
    return pl.pallas_call(
        _rmsnorm_kernel,
        grid=(t // ROW_TILE,),
        in_specs=[pl.BlockSpec((ROW_TILE, d), lambda i: (i, 0)),
                  pl.BlockSpec((1, d), lambda i: (0, 0))],
        out_specs=pl.BlockSpec((ROW_TILE, d), lambda i: (i, 0)),
        out_shape=jax.ShapeDtypeStruct((t, d), BF16),
        compiler_params=_cparams(("parallel",)),
        name="rmsnorm",
    )(x2d, g.reshape(1, d))


def _proj_raw_kernel(a_ref, w_ref, o_ref):
    o_ref[...] = jnp.dot(a_ref[...], w_ref[...], preferred_element_type=F32)


def _proj_raw(h, w, n, tn, tm, name):
    t, k = h.shape
    return pl.pallas_call(
        _proj_raw_kernel,
        grid=(n // tn, t // tm),
        in_specs=[pl.BlockSpec((tm, k), lambda j, i: (i, 0)),
                  pl.BlockSpec((k, tn), lambda j, i: (0, j))],
        out_specs=pl.BlockSpec((tm, tn), lambda j, i: (i, j)),
        out_shape=jax.ShapeDtypeStruct((t, n), F32),
        compiler_params=_cparams(("parallel", "parallel")),
        name=name,
    )(h, w)


def _rope_t(t_t, cos_t, sin_t):
    half = ROPE_DIM // 2
    rows = []
    for base in range(0, t_t.shape[0], HEAD_DIM):
        t1, t2 = t_t[base:base + half], t_t[base + half:base + ROPE_DIM]
        rows += [t1 * cos_t - t2 * sin_t, t2 * cos_t + t1 * sin_t, t_t[base + ROPE_DIM:base + HEAD_DIM]]
    return jnp.concatenate(rows, axis=0)


def _proj_q_kernel(a_ref, w_ref, cos_ref, sin_ref, qn_ref, qr_ref):
    acc = jnp.dot(a_ref[...], w_ref[...], preferred_element_type=F32)
    cos_t, sin_t = cos_ref[...], sin_ref[...]
    for c in range(acc.shape[1] // LANES):
        t_t = acc[:, c * LANES:(c + 1) * LANES].T
        qn_ref[0, c * LANES:(c + 1) * LANES, :] = (t_t * SCALE).astype(qn_ref.dtype)
        qr_ref[0, c * LANES:(c + 1) * LANES, :] = (_rope_t(t_t, cos_t, sin_t) * SCALE).astype(qr_ref.dtype)


def _proj_q(h, w, tabs, bsz, s):
    t, k = h.shape
    n = w.shape[1]
    spb = s // ROW_TILE
    tab_spec = pl.BlockSpec((ROPE_DIM // 2, ROW_TILE), lambda i: (0, i))
    out_spec = pl.BlockSpec((1, n, ROW_TILE), lambda i: (i // spb, 0, i % spb))
    return pl.pallas_call(
        _proj_q_kernel,
        grid=(t // ROW_TILE,),
        in_specs=[pl.BlockSpec((ROW_TILE, k), lambda i: (i, 0)),
                  pl.BlockSpec((k, n), lambda i: (0, 0)),
                  tab_spec, tab_spec],
        out_specs=[out_spec, out_spec],
        out_shape=[jax.ShapeDtypeStruct((bsz, n, s), BF16)] * 2,
        compiler_params=_cparams(("parallel",)),
        name="proj_q",
    )(h, w, *tabs)


def _proj_kv_kernel(a_ref, w_ref, cos_ref, sin_ref, kvc_ref, ks_ref, kw_ref, vst_ref, vwt_ref, kvc_scr):
    acc = jnp.dot(a_ref[...], w_ref[...], preferred_element_type=F32)
    cos_t, sin_t = cos_ref[...], sin_ref[...]
    tm = acc.shape[0]
    gw = KV_GROUPS * HEAD_DIM
    rows = tm // CMP_STRIDE
    low = lax.broadcasted_iota(jnp.int32, (rows, LANES), 1) < HEAD_DIM
    for c in range(2 * gw // LANES):
        kvc_scr[c] = acc[:, c * LANES:(c + 1) * LANES]
        for j in range(CMP_STRIDE // 2):
            e = kvc_scr[c, pl.ds(2 * j, rows, stride=CMP_STRIDE), :]
            o = kvc_scr[c, pl.ds(2 * j + 1, rows, stride=CMP_STRIDE), :]
            kvc_ref[0, 2 * c, :, j * LANES:(j + 1) * LANES] = jnp.where(low, e, pltpu.roll(o, HEAD_DIM, 1))
            kvc_ref[0, 2 * c + 1, :, j * LANES:(j + 1) * LANES] = jnp.where(low, pltpu.roll(e, HEAD_DIM, 1), o)
    lane = lax.broadcasted_iota(jnp.int32, (tm, LANES), 1)
    key_pad = jnp.where(lane == HEAD_DIM, 1.0, 0.0)
    ones_rows = jnp.where(lax.broadcasted_iota(jnp.int32, (V_ROWS - HEAD_DIM, tm), 0) == 0, 1.0, 0.0)
    for base, k_ref, vt_ref in ((2 * gw, ks_ref, vst_ref), (4 * gw, kw_ref, vwt_ref)):
        vtile = vt_ref.shape[-1]
        for c in range(gw // LANES):
            kk = _rope_t(acc[:, base + c * LANES: base + (c + 1) * LANES].T, cos_t, sin_t).T
            vv = acc[:, base + gw + c * LANES: base + gw + (c + 1) * LANES].T
            for half in range(2):
                g = 2 * c + half
                k_lo = kk if half == 0 else pltpu.roll(kk, HEAD_DIM, 1)
                k_ref[0, g] = jnp.where(lane < HEAD_DIM, k_lo, key_pad).astype(k_ref.dtype)
                v_aug = jnp.concatenate([vv[half * HEAD_DIM:(half + 1) * HEAD_DIM], ones_rows], axis=0)
                for j in range(ROW_TILE // vtile):
                    vt_ref[0, g, j] = v_aug[:, j * vtile:(j + 1) * vtile].astype(vt_ref.dtype)


def _proj_kv(h, w, tabs, bsz, s):
    t, k = h.shape
    n = w.shape[1]
    spb = s // ROW_TILE
    tab_spec = pl.BlockSpec((ROPE_DIM // 2, ROW_TILE), lambda i: (0, i))
    k_spec = pl.BlockSpec((1, KV_GROUPS, ROW_TILE, LANES), lambda i: (i // spb, 0, i % spb, 0))
    k_shape = jax.ShapeDtypeStruct((bsz, KV_GROUPS, s, LANES), BF16)

    def vt(tile):
        return (pl.BlockSpec((1, KV_GROUPS, ROW_TILE // tile, V_ROWS, tile), lambda i: (i // spb, 0, i % spb, 0, 0)),
                jax.ShapeDtypeStruct((bsz, KV_GROUPS, s // tile, V_ROWS, tile), BF16))

    (vs_spec, vs_shape), (vw_spec, vw_shape) = vt(KEY_STEP), vt(Q_BLOCK)
    return pl.pallas_call(
        _proj_kv_kernel,
        grid=(t // ROW_TILE,),
        in_specs=[pl.BlockSpec((ROW_TILE, k), lambda i: (i, 0)),
                  pl.BlockSpec((k, n), lambda i: (0, 0)),
                  tab_spec, tab_spec],
        out_specs=[pl.BlockSpec((1, 2 * KV_GROUPS, ROW_TILE // CMP_STRIDE, CMP_STRIDE * HEAD_DIM),
                                lambda i: (i // spb, 0, i % spb, 0)),
                   k_spec, k_spec, vs_spec, vw_spec],
        out_shape=[jax.ShapeDtypeStruct((bsz, 2 * KV_GROUPS, s // CMP_STRIDE, CMP_STRIDE * HEAD_DIM), F32),
                   k_shape, k_shape, vs_shape, vw_shape],
        scratch_shapes=[pltpu.VMEM((2 * KV_GROUPS * HEAD_DIM // LANES, ROW_TILE, LANES), F32)],
        compiler_params=_cparams(("parallel",)),
        name="proj_kv",
    )(h, w, *tabs)


def _ssd_kernel(z_ref, xs_ref, bc_ref, misc_ref, convw_ref, convb_ref, dtb_ref, alog_ref,
                dskip_ref, gnorm_ref, selb_ref, sele_ref, wout_ref, o_ref,
                tail_ref, act_ref, state_ref):
    c = pl.program_id(1)
    L = CHUNK
    gw = SSM_HPG * SSM_HEAD_DIM

    @pl.when(c == 0)
    def _():
        state_ref[...] = jnp.zeros_like(state_ref)
        tail_ref[...] = jnp.zeros_like(tail_ref)

    cw = 512
    first_row = lax.broadcasted_iota(jnp.int32, (8, cw), 0) == 0
    for cc in range(SSM_XBC // cw):
        cols = slice(cc * cw, (cc + 1) * cw)
        u = xs_ref[:, cols] if cc < D_INNER // cw else bc_ref[:, cc * cw - D_INNER:(cc + 1) * cw - D_INNER]
        a = convw_ref[0:1, cols] * u
        for k in range(1, SSM_CONV):
            shifted = pltpu.roll(a, 1, 0)
            shifted = jnp.concatenate([jnp.where(first_row, tail_ref[k - 1:k, cols], shifted[0:8]), shifted[8:]], axis=0)
            tail_ref[k - 1:k, cols] = a[L - 1:L]
            a = convw_ref[k:k + 1, cols] * u + shifted
        act_ref[:, cols] = _silu(a + convb_ref[:, cols])

    lane = lax.broadcasted_iota(jnp.int32, (L, LANES), 1)
    row = lax.broadcasted_iota(jnp.int32, (L, LANES), 0)
    head_lane = lane < SSM_HEADS
    raw = misc_ref[...] + dtb_ref[...]
    dt = jnp.where(head_lane, jnp.maximum(raw, 0.0) + jnp.log1p(jnp.exp(-jnp.abs(raw))), 0.0)
    a_row = jnp.where(head_lane[0:1], -jnp.exp(alog_ref[...]), 0.0)
    cs = dt * a_row
    sh = 1
    while sh < L:
        cs = cs + jnp.where(row >= sh, pltpu.roll(cs, sh, 0), 0.0)
        sh *= 2
    tot = cs[L - 1:L, :]
    w_state = dt * jnp.exp(tot - cs)
    cs_t, dt_t, w_t = cs.T, dt.T, w_state.T

    hi, mid, lo = _split3(cs)
    packed = (hi.astype(F32) + pltpu.roll(mid.astype(F32), SSM_HEADS, 1)
              + pltpu.roll(lo.astype(F32), 2 * SSM_HEADS, 1)).astype(BF16)

    tri = lax.broadcasted_iota(jnp.int32, (L, L), 0) >= lax.broadcasted_iota(jnp.int32, (L, L), 1)
    brow = lax.broadcasted_iota(jnp.int32, (SSM_HPG * L, gw), 0) // L
    bcol = lax.broadcasted_iota(jnp.int32, (SSM_HPG * L, gw), 1) // SSM_HEAD_DIM
    blockmask = brow == bcol

    for g in range(SSM_GROUPS):
        col_b = jnp.dot(packed, selb_ref[:, g * SSM_HPG * L:(g + 1) * SSM_HPG * L], preferred_element_type=F32)
        exp_e = jnp.exp(jnp.dot(packed, sele_ref[:, g * gw:(g + 1) * gw], preferred_element_type=F32))
        etot_e = exp_e[L - 1:L, :]
        xs_g = act_ref[:, g * gw:(g + 1) * gw]
        b_g = act_ref[:, D_INNER + g * SSM_STATE: D_INNER + (g + 1) * SSM_STATE]
        c_g = act_ref[:, D_INNER + SSM_GROUPS * SSM_STATE + g * SSM_STATE:
                      D_INNER + SSM_GROUPS * SSM_STATE + (g + 1) * SSM_STATE]
        c_bf = c_g.astype(BF16)
        cb = lax.dot_general(c_bf, b_g.astype(BF16), (((1,), (1,)), ((), ())),
                             preferred_element_type=F32)
        b_t = b_g.T
        tops, bots = [], []
        for j in range(SSM_HPG):
            h = g * SSM_HPG + j
            decay = jnp.exp(jnp.where(tri, col_b[:, j * L:(j + 1) * L] - cs_t[h:h + 1, :], NEG))
            tops.append((cb * decay * dt_t[h:h + 1, :]).astype(BF16))
            bots.append((b_t * w_t[h:h + 1, :]).astype(BF16))
        lhs = jnp.concatenate([jnp.concatenate(tops, axis=1), jnp.concatenate(bots, axis=1)], axis=0)
        xs_bf = xs_g.astype(BF16)
        xs_bd = jnp.where(blockmask, jnp.concatenate([xs_bf] * SSM_HPG, axis=0), jnp.zeros((), BF16))
        res = jnp.dot(lhs, xs_bd, preferred_element_type=F32)
        st = state_ref[g]
        y_off = jnp.dot(c_bf, st.astype(BF16), preferred_element_type=F32) * exp_e
        state_ref[g] = st * etot_e + res[L:2 * L]
        y = res[0:L] + y_off + dskip_ref[:, g * gw:(g + 1) * gw] * xs_g
        zg = z_ref[:, g * gw:(g + 1) * gw]
        y = y * _silu(zg)
        ms = jnp.mean(y * y, axis=-1, keepdims=True)
        act_ref[:, g * gw:(g + 1) * gw] = y * lax.rsqrt(ms + NORM_EPS) * gnorm_ref[:, g * gw:(g + 1) * gw]

    o_ref[...] = jnp.dot(act_ref[:, 0:D_INNER].astype(BF16), wout_ref[...], preferred_element_type=F32)


def _ssd(raw, raw_gate, conv_w, conv_b, dt_bias, a_log, d_skip, g_ssm_norm, w_ssm, bsz, s):
    t = raw.shape[0]
    nc = s // CHUNK
    blk = lambda cb: pl.BlockSpec((CHUNK, 2048), lambda b, c, cb=cb: (b * nc + c, cb))
    const = lambda shape: pl.BlockSpec(shape, lambda b, c: (0,) * len(shape))
    k = jnp.arange(LANES)
    selb = ((k[:, None] < 3 * SSM_HEADS) & ((k[:, None] % SSM_HEADS) == (jnp.arange(SSM_HEADS * LANES)[None, :] // LANES)))
    sele = ((k[:, None] < 3 * SSM_HEADS) & ((k[:, None] % SSM_HEADS) == (jnp.arange(D_INNER)[None, :] // SSM_HEAD_DIM)))
    pad = lambda v: jnp.pad(v.astype(F32), (0, LANES - SSM_HEADS)).reshape(1, LANES)
    return pl.pallas_call(
        _ssd_kernel,
        grid=(bsz, nc),
        in_specs=[blk(RAW_Z // 2048), blk(RAW_XS // 2048), blk(RAW_BC // 2048),
                  pl.BlockSpec((CHUNK, LANES), lambda b, c: (b * nc + c, GATE_MISC // LANES)),
                  const((SSM_CONV, SSM_XBC)), const((1, SSM_XBC)), const((1, LANES)), const((1, LANES)),
                  const((1, D_INNER)), const((1, D_INNER)),
                  const((LANES, SSM_HEADS * LANES)), const((LANES, D_INNER)),
                  const((D_INNER, D_MODEL))],
        out_specs=pl.BlockSpec((CHUNK, D_MODEL), lambda b, c: (b * nc + c, 0)),
        out_shape=jax.ShapeDtypeStruct((t, D_MODEL), F32),
        scratch_shapes=[pltpu.VMEM((8, SSM_XBC), F32),
                        pltpu.VMEM((CHUNK, SSM_XBC), F32),
                        pltpu.VMEM((SSM_GROUPS, SSM_STATE, SSM_HPG * SSM_HEAD_DIM), F32)],
        compiler_params=_cparams(("parallel", "arbitrary")),
        name="ssd_mixer",
    )(raw, raw, raw, raw_gate, conv_w.astype(F32), conv_b.reshape(1, SSM_XBC).astype(F32),
      pad(dt_bias), pad(a_log), jnp.repeat(d_skip.astype(F32), SSM_HEAD_DIM).reshape(1, D_INNER),
      g_ssm_norm.reshape(1, D_INNER).astype(F32), selb.astype(BF16), sele.astype(BF16), w_ssm.astype(BF16))


def _compress_kernel(u_ref, pelo_ref, pehi_ref, w1_ref, w2_ref, o_ref):
    u = u_ref[0]
    half = CMP_STRIDE * HEAD_DIM
    a = jnp.dot((u + pelo_ref[0, 0]).astype(BF16), w1_ref[0, 0:half, :], preferred_element_type=F32)
    b = jnp.dot((u + pehi_ref[0, 0]).astype(BF16), w1_ref[0, half:2 * half, :], preferred_element_type=F32)
    pre = a + pltpu.roll(b, u.shape[0] - 1, 0)
    hidden = _silu(pre)
    out = jnp.dot(hidden.astype(BF16), w2_ref[0], preferred_element_type=F32)
    lane = lax.broadcasted_iota(jnp.int32, out.shape, 1)
    o_ref[0] = jnp.where(lane == HEAD_DIM, 1.0, out).astype(o_ref.dtype)


def _compress(u, pe, w1, w2):
    n, rows, width = u.shape
    kv_of = lambda i: (i // KV_GROUPS) % 2
    return pl.pallas_call(
        _compress_kernel,
        grid=(n,),
        in_specs=[pl.BlockSpec((1, rows, width), lambda i: (i, 0, 0)),
                  pl.BlockSpec((1, 1, 1, width), lambda i: (kv_of(i), 0, 0, 0)),
                  pl.BlockSpec((1, 1, 1, width), lambda i: (kv_of(i), 1, 0, 0)),
                  pl.BlockSpec((1, 2 * width, CMP_HIDDEN), lambda i: (kv_of(i), 0, 0)),
                  pl.BlockSpec((1, CMP_HIDDEN, LANES), lambda i: (kv_of(i), 0, 0))],
        out_specs=pl.BlockSpec((1, rows, LANES), lambda i: (i, 0, 0)),
        out_shape=jax.ShapeDtypeStruct((n, rows, LANES), BF16),
        compiler_params=_cparams(("parallel",)),
        name="compress",
    )(u, pe, pe, w1, w2)


def _nsa_kernel(qn_ref, qr_ref, kc_ref, vct_ref, ks_ref, vst_ref, kw_ref, vwt_ref, gate_ref, ovt_ref, oh_ref,
                eye_ref, wlo_ref, whi_ref,
                o_ref, gt_ref, qa_ref, sa_ref, sb_ref, pa_ref, pb_ref, sc_ref, accc_ref, invc_ref, imp_ref, sw_ref, pc_ref,
                *, topk):
    gp = pl.program_id(1)
    qb = pl.program_id(2)
    groups = range(NSA_GROUPS)
    nq = ATT_HPG * Q_BLOCK
    ncmp = kc_ref.shape[1]
    nsel = ovt_ref.shape[0]
    q0 = qb * Q_BLOCK
    tq = q0 + lax.broadcasted_iota(jnp.int32, (1, nq), 1) % Q_BLOCK

    def heads_on_lanes(ref, gi):
        base = gi * ATT_HPG * HEAD_DIM
        return jnp.concatenate([ref[0, base + r * HEAD_DIM:base + (r + 1) * HEAD_DIM, :]
                                for r in range(ATT_HPG)], axis=1)

    qn_t = [heads_on_lanes(qn_ref, gi) for gi in groups]
    qr_t = [heads_on_lanes(qr_ref, gi) for gi in groups]

    def q_operand(q_t, flag, table):
        flag_rows = jnp.concatenate([flag, jnp.zeros((LANES - HEAD_DIM - 1, nq), F32)], axis=0).astype(BF16)
        return jnp.concatenate([q_t, flag_rows] + ([table] if table is not None else []), axis=0)

    no_flag = jnp.zeros((1, nq), F32)

    n_chunks = ncmp // LANES
    chunk_span = LANES * CMP_STRIDE
    c_hi = jnp.minimum((q0 + Q_BLOCK - CMP_LEN) // chunk_span, n_chunks - 1)

    def compressed_branch(n_live):
        live = n_live * LANES
        ov_live = jnp.concatenate([ovt_ref[:, piece * ncmp:piece * ncmp + live] for piece in range(3)], axis=1)
        for c in range(n_live):
            for gi in groups:
                s = jnp.dot(kc_ref[gi, c * LANES:(c + 1) * LANES, :], q_operand(qn_t[gi], no_flag, None),
                            preferred_element_type=F32)
                if c >= n_live - 2:
                    ends = (c * chunk_span + CMP_LEN - 1
                            + lax.broadcasted_iota(jnp.int32, (LANES, nq), 0) * CMP_STRIDE)
                    s = jnp.where(ends <= tq, s, NEG)
                sc_ref[gi, c * LANES:(c + 1) * LANES, :] = s
        m_cs, invs = [], []
        for gi in groups:
            s_c = sc_ref[gi, 0:live, :]
            m_cs.append(jnp.max(s_c, axis=0, keepdims=True))
            p_c = jnp.exp2(s_c - m_cs[gi])
            sc_ref[gi, 0:live, :] = p_c
            pc_ref[gi, 0:live, :] = p_c.astype(BF16)
        for gi in groups:
            acc = jnp.dot(vct_ref[gi, :, 0:live], pc_ref[gi, 0:live, :], preferred_element_type=F32)
            invs.append(jnp.where(m_cs[gi] > 0.5 * NEG, 1.0 / acc[HEAD_DIM:HEAD_DIM + 1], 0.0))
            accc_ref[gi] = acc
            invc_ref[gi] = jnp.broadcast_to(invs[gi], (8, nq))
        for gi in groups:
            inv = invs[gi]
            p_sum = sc_ref[gi, 0:live, 0:Q_BLOCK] * inv[:, 0:Q_BLOCK]
            for r in range(1, ATT_HPG):
                p_sum = p_sum + sc_ref[gi, 0:live, r * Q_BLOCK:(r + 1) * Q_BLOCK] * inv[:, r * Q_BLOCK:(r + 1) * Q_BLOCK]
            imp_ref[gi] = jnp.dot(ov_live, jnp.concatenate(_split3(p_sum), axis=0), preferred_element_type=F32)

    for k in range(n_chunks):
        pl.when(c_hi == k)(functools.partial(compressed_branch, k + 1))
    acc_c = [accc_ref[gi] for gi in groups]
    inv_c = [invc_ref[gi, 0:1, :] for gi in groups]
    imp = [imp_ref[gi] for gi in groups]

    n_wt = WINDOW // Q_BLOCK + 1
    eye = eye_ref[...]
    for gi in groups:
        for i in range(n_wt):
            kt = qb - (n_wt - 1) + i
            k0 = pl.multiple_of(jnp.maximum(kt, 0) * Q_BLOCK, Q_BLOCK)
            keys = kw_ref[gi, pl.ds(k0, Q_BLOCK), :]
            flag = jnp.where(kt >= 0, no_flag, NEG)
            if i == 0:
                s = jnp.dot(jnp.concatenate([keys, eye], axis=1), q_operand(qr_t[gi], flag, wlo_ref[...]),
                            preferred_element_type=F32)
            elif i == n_wt - 1:
                s = jnp.dot(jnp.concatenate([keys, eye], axis=1), q_operand(qr_t[gi], flag, whi_ref[...]),
                            preferred_element_type=F32)
            else:
                s = jnp.dot(keys, q_operand(qr_t[gi], flag, None), preferred_element_type=F32)
            sw_ref[gi, i * Q_BLOCK:(i + 1) * Q_BLOCK, :] = s
    acc_w = []
    for gi in groups:
        v_w = [vwt_ref[gi, jnp.maximum(qb - (n_wt - 1) + i, 0)] for i in range(n_wt)]
        s_w = sw_ref[gi]
        p_w = jnp.exp2(s_w - jnp.max(s_w, axis=0, keepdims=True))
        acc_w.append(jnp.dot(jnp.concatenate(v_w, axis=1), p_w.astype(BF16), preferred_element_type=F32))

    blk = lax.broadcasted_iota(jnp.int32, (nsel, Q_BLOCK), 0)
    cur = (q0 + lax.broadcasted_iota(jnp.int32, (nsel, Q_BLOCK), 1)) // SEL_BLOCK
    valid = blk <= cur
    forced = valid & ((blk == 0) | (blk == cur) | (blk == cur - 1))
    val = [jnp.where(forced, REMOVED, jnp.where(valid, imp[gi], NEG)) for gi in groups]
    blk_f = blk.astype(F32)
    for _ in range(topk - 3):
        for gi in groups:
            mx = jnp.max(val[gi], axis=0, keepdims=True)
            first = jnp.min(jnp.where(val[gi] == mx, blk_f, float(nsel)), axis=0, keepdims=True)
            val[gi] = jnp.where(blk_f == first, REMOVED, val[gi])

    pad_rows = [jnp.zeros((LANES - nsel, nq), F32)] if nsel < LANES else []
    for gi in groups:
        bias = jnp.concatenate([jnp.where(val[gi] == REMOVED, 0.0, NEG)] * ATT_HPG, axis=1)
        qa_ref[gi] = q_operand(qr_t[gi], no_flag, jnp.concatenate([bias] + pad_rows, axis=0).astype(BF16))

    def sel_scores(gi, i):
        k0 = pl.multiple_of(i * KEY_STEP, KEY_STEP)
        lhs = jnp.concatenate([ks_ref[gi, pl.ds(k0, KEY_STEP), :], oh_ref[pl.ds(k0, KEY_STEP), :]], axis=1)
        return jnp.dot(lhs, qa_ref[gi], preferred_element_type=F32)

    row_k = lax.broadcasted_iota(jnp.int32, (KEY_STEP, nq), 0)

    def softmax_step(s_ref, p_ref, gi, m, first_key):
        s = s_ref[gi]
        if first_key is not None:
            s = jnp.where(row_k <= tq - first_key, s, NEG)
        m_new = jnp.maximum(m, jnp.max(s, axis=0, keepdims=True))
        p_ref[gi] = jnp.exp2(s - m_new).astype(BF16)
        return m_new, jnp.exp2(m - m_new)

    def pv(gi, acc, alpha, p_ref, i):
        return acc * alpha + jnp.dot(vst_ref[gi, i], p_ref[gi], preferred_element_type=F32)

    def sel_pair(j, carries, final):
        a = 2 * j
        m, acc, alpha_b = ([c[k] for c in carries] for k in range(3))
        alpha_a = [None] * NSA_GROUPS
        for gi in groups:
            sb_ref[gi] = sel_scores(gi, a + 1)
            acc[gi] = pv(gi, acc[gi], alpha_b[gi], pb_ref, jnp.maximum(a - 1, 0))
            m[gi], alpha_a[gi] = softmax_step(sa_ref, pa_ref, gi, m[gi], a * KEY_STEP if final else None)
            if not final:
                sa_ref[gi] = sel_scores(gi, a + 2)
            acc[gi] = pv(gi, acc[gi], alpha_a[gi], pa_ref, a)
            m[gi], alpha_b[gi] = softmax_step(sb_ref, pb_ref, gi, m[gi], (a + 1) * KEY_STEP if final else None)
        return tuple((m[gi], acc[gi], alpha_b[gi]) for gi in groups)

    n_pairs = (q0 + Q_BLOCK - 1) // (2 * KEY_STEP) + 1
    for gi in groups:
        sa_ref[gi] = sel_scores(gi, 0)
    pb_ref[...] = jnp.zeros_like(pb_ref)
    init = tuple((jnp.full((1, nq), NEG, F32), jnp.zeros((V_ROWS, nq), F32), jnp.ones((1, nq), F32))
                 for gi in groups)
    carries = lax.fori_loop(0, n_pairs - 1, lambda j, c: sel_pair(j, c, False), init)
    carries = sel_pair(n_pairs - 1, carries, True)
    acc_s = [pv(gi, carries[gi][1], carries[gi][2], pb_ref, 2 * n_pairs - 1) for gi in groups]

    gt_ref[...] = _sigmoid(gate_ref[...]).T
    hd = HEAD_DIM
    for gi in groups:
        def gate_row(branch, gi=gi):
            base = MISC_GATE0 + branch * ATT_HEADS + (gp * NSA_GROUPS + gi) * ATT_HPG
            return jnp.concatenate([gt_ref[pl.ds(base + r, 1), :] for r in range(ATT_HPG)], axis=1)

        o = ((gate_row(0) * inv_c[gi]) * acc_c[gi][0:hd] + (gate_row(1) / acc_s[gi][hd:hd + 1]) * acc_s[gi][0:hd]
             + (gate_row(2) / acc_w[gi][hd:hd + 1]) * acc_w[gi][0:hd])
        for r in range(ATT_HPG):
            row0 = (gi * ATT_HPG + r) * HEAD_DIM
            o_ref[0, row0:row0 + HEAD_DIM, :] = o[:, r * Q_BLOCK:(r + 1) * Q_BLOCK].astype(o_ref.dtype)


def _nsa(qn_t, qr_t, kc, vc_t, ks, vs_t, kw, vw_t, raw_gate, bsz, s):
    nqb = s // Q_BLOCK
    ncmp = s // CMP_STRIDE
    nsel = s // SEL_BLOCK
    topk = min(SEL_TOPK, nsel)
    gw = ATT_HPG * HEAD_DIM
    ci = jnp.arange(ncmp)[None, :]
    sj = jnp.arange(nsel)[:, None]
    ov_t = (ci * CMP_STRIDE < (sj + 1) * SEL_BLOCK) & (ci * CMP_STRIDE + CMP_LEN > sj * SEL_BLOCK) \
        & (ci < ncmp - 1)
    ov_t3 = jnp.concatenate([ov_t.astype(BF16)] * 3, axis=1)
    assert nsel <= LANES
    onehot = (jnp.arange(s)[:, None] // SEL_BLOCK == jnp.arange(LANES)[None, :]).astype(BF16)
    eye = jnp.eye(Q_BLOCK, dtype=BF16)
    key_row = jnp.arange(Q_BLOCK)[:, None]
    q_lane = jnp.arange(ATT_HPG * Q_BLOCK)[None, :] % Q_BLOCK
    win_lo = jnp.where(key_row > q_lane, 0.0, NEG).astype(BF16)
    win_hi = jnp.where(key_row <= q_lane, 0.0, NEG).astype(BF16)
    once = pl.Buffered(1)
    const2 = lambda shape: pl.BlockSpec(shape, lambda b, g, i: (0, 0), pipeline_mode=once)
    ng = NSA_GROUPS
    gpb = KV_GROUPS // ng
    q_spec = pl.BlockSpec((1, ng * gw, Q_BLOCK), lambda b, g, i: (b, g, i))
    per_bg = lambda shape: pl.BlockSpec((ng,) + shape, lambda b, g, i: (b * gpb + g,) + (0,) * len(shape),
                                        pipeline_mode=once)
    nql = ATT_HPG * Q_BLOCK
    return pl.pallas_call(
        functools.partial(_nsa_kernel, topk=topk),
        grid=(bsz, gpb, nqb),
        in_specs=[q_spec, q_spec,
                  per_bg((ncmp, LANES)), per_bg((V_ROWS, ncmp)),
                  per_bg((s, LANES)), per_bg((s // KEY_STEP, V_ROWS, KEY_STEP)),
                  per_bg((s, LANES)), per_bg((s // Q_BLOCK, V_ROWS, Q_BLOCK)),
                  pl.BlockSpec((Q_BLOCK, LANES), lambda b, g, i: (b * nqb + i, GATE_MISC // LANES)),
                  const2((nsel, 3 * ncmp)), const2((s, LANES)),
                  const2((Q_BLOCK, Q_BLOCK)), const2((Q_BLOCK, ATT_HPG * Q_BLOCK)),
                  const2((Q_BLOCK, ATT_HPG * Q_BLOCK))],
        out_specs=pl.BlockSpec((1, ng * gw, Q_BLOCK), lambda b, g, i: (b, g, i)),
        out_shape=jax.ShapeDtypeStruct((bsz, ATT_HEADS * HEAD_DIM, s), BF16),
        scratch_shapes=[pltpu.VMEM((LANES, Q_BLOCK), F32),
                        pltpu.VMEM((ng, 2 * LANES, nql), BF16),
                        pltpu.VMEM((ng, KEY_STEP, nql), F32),
                        pltpu.VMEM((ng, KEY_STEP, nql), F32),
                        pltpu.VMEM((ng, KEY_STEP, nql), BF16),
                        pltpu.VMEM((ng, KEY_STEP, nql), BF16),
                        pltpu.VMEM((ng, ncmp, nql), F32),
                        pltpu.VMEM((ng, V_ROWS, nql), F32),
                        pltpu.VMEM((ng, 8, nql), F32),
                        pltpu.VMEM((ng, nsel, Q_BLOCK), F32),
                        pltpu.VMEM((ng, (WINDOW // Q_BLOCK + 1) * Q_BLOCK, nql), F32),
                        pltpu.VMEM((ng, ncmp, nql), BF16)],
        compiler_params=_cparams(("parallel", "parallel", "arbitrary")),
        name="nsa_sweep",
    )(qn_t, qr_t, kc, vc_t, ks, vs_t, kw, vw_t, raw_gate, ov_t3, onehot, eye, win_lo, win_hi)


def _merge_mlp_kernel(ot_ref, yssm_ref, mix_ref, x_ref, wab_ref, wo_ref, g1_ref, wup_ref, wdn_ref, g2_ref, o_ref):
    o = ot_ref[0].astype(F32).T.astype(BF16)
    y_att = jnp.dot(o, wab_ref[...], preferred_element_type=F32)
    gates = _sigmoid(mix_ref[...])
    mixed = gates[:, 0:D_MODEL] * yssm_ref[...] + gates[:, D_MODEL:2 * D_MODEL] * y_att
    x = x_ref[...] + jnp.dot(mixed.astype(BF16), wo_ref[...], preferred_element_type=F32)
    ms = jnp.mean(x * x, axis=-1, keepdims=True)
    h = (x * lax.rsqrt(ms + NORM_EPS) * g1_ref[...]).astype(BF16)
    up = jnp.maximum(jnp.dot(h, wup_ref[...], preferred_element_type=F32), 0.0)
    y = x + jnp.dot((up * up).astype(BF16), wdn_ref[...], preferred_element_type=F32)
    ms2 = jnp.mean(y * y, axis=-1, keepdims=True)
    o_ref[...] = y * lax.rsqrt(ms2 + NORM_EPS) * g2_ref[...]


def _merge_mlp(o_t, y_ssm, raw_gate, x2d, w_ab, w_o, g1, w_up, w_down, g2, s):
    t = x2d.shape[0]
    tm = ROW_TILE
    spb = s // tm
    row_spec = pl.BlockSpec((tm, D_MODEL), lambda i: (i, 0))
    const = lambda shape: pl.BlockSpec(shape, lambda i: (0, 0), pipeline_mode=pl.Buffered(1))
    return pl.pallas_call(
        _merge_mlp_kernel,
        grid=(t // tm,),
        in_specs=[pl.BlockSpec((1, D_MODEL, tm), lambda i: (i // spb, 0, i % spb)),
                  row_spec,
                  pl.BlockSpec((tm, 2 * D_MODEL), lambda i: (i, GATE_MIX // (2 * D_MODEL))),
                  row_spec, const((D_MODEL, D_MODEL)), const((D_MODEL, D_MODEL)),
                  const((1, D_MODEL)), const((D_MODEL, MLP_HIDDEN)), const((MLP_HIDDEN, D_MODEL)),
                  const((1, D_MODEL))],
        out_specs=row_spec,
        out_shape=jax.ShapeDtypeStruct((t, D_MODEL), F32),
        compiler_params=_cparams(("parallel",)),
        name="merge_mlp",
    )(o_t, y_ssm, raw_gate, x2d, w_ab, w_o, g1.reshape(1, D_MODEL), w_up, w_down, g2.reshape(1, D_MODEL))


def _rope_tables(positions):
    inv_freq = ROPE_THETA ** (-jnp.arange(0, ROPE_DIM, 2, dtype=F32) / ROPE_DIM)
    ang = positions.astype(F32).reshape(1, -1) * inv_freq[:, None]
    return jnp.cos(ang), jnp.sin(ang)


def kernel(x, positions, g_norm_mix, w_in, conv_w, conv_b, dt_bias, a_log, d_skip, g_ssm_norm, w_ssm_branch, cmp_pe_k, cmp_pe_v, w_cmp_k1, w_cmp_k2, w_cmp_v1, w_cmp_v2, w_attn_branch, w_o, g_norm_mlp, w_up, w_down, g_norm_final):
    bsz, s, d = x.shape
    assert d == D_MODEL and s % (2 * ROW_TILE) == 0 and w_in.shape[0] == 1
    t = bsz * s
    x2d = x.reshape(t, d)
    tabs = _rope_tables(positions)

    w = w_in[0].astype(BF16)
    o_dt, o_q, o_kv, o_ag, o_mg = 6144, 6176, 7200, 8736, 8784
    misc_pad = jnp.zeros((d, GATE_COLS - GATE_MISC - SSM_HEADS - 3 * ATT_HEADS), BF16)
    w_gate = jnp.concatenate([w[:, o_mg:], w[:, o_dt:o_q], w[:, o_ag:o_mg], misc_pad], axis=1)
    w_q = w[:, o_q:o_kv]
    w_kv = w[:, o_kv:o_ag]

    h = _rmsnorm(x2d, g_norm_mix[0])
    raw = _proj_raw(h, w, SSM_COLS, SSM_TN, 2 * ROW_TILE, "proj_ssm")
    raw_gate = _proj_raw(h, w_gate, GATE_COLS, GATE_COLS, 2 * ROW_TILE, "proj_gate")
    qn_t, qr_t = _proj_q(h, w_q, tabs, bsz, s)
    kvc, ks, kw, vs_t, vw_t = _proj_kv(h, w_kv, tabs, bsz, s)

    y_ssm = _ssd(raw, raw_gate, conv_w[0], conv_b[0], dt_bias[0], a_log[0], d_skip[0], g_ssm_norm[0],
                 w_ssm_branch[0], bsz, s)

    rows = s // CMP_STRIDE
    u = kvc.reshape(bsz * 2 * KV_GROUPS, rows, CMP_STRIDE * HEAD_DIM)
    pe = jnp.stack([cmp_pe_k[0], cmp_pe_v[0]]).astype(F32).reshape(2, 2, 1, CMP_STRIDE * HEAD_DIM)
    w1 = jnp.stack([w_cmp_k1[0], w_cmp_v1[0]]).astype(BF16)
    w2 = jnp.pad(jnp.stack([w_cmp_k2[0], w_cmp_v2[0]]).astype(BF16), ((0, 0), (0, 0), (0, LANES - HEAD_DIM)))
    cmp = _compress(u, pe, w1, w2).reshape(bsz, 2, KV_GROUPS, rows, LANES)
    kc = cmp[:, 0].reshape(bsz * KV_GROUPS, rows, LANES)
    vc_t = jnp.swapaxes(cmp[:, 1], -1, -2)[:, :, :V_ROWS].reshape(bsz * KV_GROUPS, V_ROWS, rows)

    flat = lambda a: a.reshape((bsz * KV_GROUPS,) + a.shape[2:])
    o_t = _nsa(qn_t, qr_t, kc, vc_t, flat(ks), flat(vs_t), flat(kw), flat(vw_t), raw_gate, bsz, s)

    out = _merge_mlp(o_t, y_ssm, raw_gate, x2d, w_attn_branch[0].astype(BF16), w_o[0].astype(BF16),
                     g_norm_mlp[0], w_up[0].astype(BF16), w_down[0].astype(BF16), g_norm_final, s)
    return out.reshape(bsz, s, d)
```

```python
import functools

import jax
import jax.numpy as jnp
from jax import lax
from jax.experimental import pallas as pl
from jax.experimental.pallas import tpu as pltpu

F32 = jnp.float32
BF16 = jnp.bfloat16

D_MODEL = 1024
D_INNER = 2048
SSM_HEADS = 32
SSM_GROUPS = 8
SSM_HPG = 4
SSM_HEAD_DIM = 64
SSM_STATE = 128
SSM_CONV = 4
CHUNK = 128
SSM_XBC = 4096
HEAD_DIM = 64
ATT_HEADS = 16
KV_GROUPS = 4
ATT_HPG = 4
CMP_LEN = 32
CMP_STRIDE = 16
CMP_HIDDEN = 256
SEL_BLOCK = 64
SEL_TOPK = 16
WINDOW = 512
Q_BLOCK = 128
ROPE_THETA = 500000.0
ROPE_DIM = 16
MLP_HIDDEN = 4096
NORM_EPS = 1e-6
NEG = -1e30
BIG = 1e30
REMOVED = -3e38
LOG2E = 1.4426950408889634
SCALE = HEAD_DIM ** -0.5 * LOG2E
KEY_STEP = 256
NSA_GROUPS = 4

LANES = 128
ROW_TILE = 512
VMEM_LIMIT = 56 * 1024 * 1024

RAW_Z, RAW_XS, RAW_BC = 0, 2048, 4096
SSM_COLS, SSM_TN = 6144, 1536
GATE_MIX, GATE_MISC, GATE_COLS = 0, 2048, 2176
MISC_GATE0 = SSM_HEADS
V_ROWS = HEAD_DIM + 16


def _cparams(sem):
    return pltpu.CompilerParams(dimension_semantics=sem, vmem_limit_bytes=VMEM_LIMIT)


def _sigmoid(x):
    return 0.5 * jnp.tanh(0.5 * x) + 0.5


def _silu(x):
    h = 0.5 * x
    return h + h * jnp.tanh(h)


def _split3(x):
    hi = x.astype(BF16)
    r1 = x - hi.astype(F32)
    mid = r1.astype(BF16)
    lo = (r1 - mid.astype(F32)).astype(BF16)
    return hi, mid, lo


def _rmsnorm_kernel(x_ref, g_ref, o_ref):
    x = x_ref[...]
    ms = jnp.mean(x * x, axis=-1, keepdims=True)
    o_ref[...] = (x * lax.rsqrt(ms + NORM_EPS) * g_ref[...]).astype(o_ref.dtype)


def _rmsnorm(x2d, g):
    t, d = x2d.shape
    return pl.pallas_call(
        _rmsnorm_kernel,
        grid=(t // ROW_TILE,),
        in_specs=[pl.BlockSpec((ROW_TILE, d), lambda i: (i, 0)),
                  pl.BlockSpec((1, d), lambda i: (0, 0))],
        out_specs=pl.BlockSpec((ROW_TILE, d), lambda i: (i, 0)),
        out_shape=jax.ShapeDtypeStruct((t, d), BF16),
        compiler_params=_cparams(("parallel",)),
        name="rmsnorm",
    )(x2d, g.reshape(1, d))


def _proj_raw_kernel(a_ref, w_ref, o_ref):
    o_ref[...] = jnp.dot(a_ref[...], w_ref[...], preferred_element_type=F32)


def _proj_raw(h, w, n, tn, tm, name):
    t, k = h.shape
    return pl.pallas_call(
        _proj_raw_kernel,
        grid=(n // tn, t // tm),
        in_specs=[pl.BlockSpec((tm, k), lambda j, i: (i, 0)),
                  pl.BlockSpec((k, tn), lambda j, i: (0, j))],
        out_specs=pl.BlockSpec((tm, tn), lambda j, i: (i, j)),
        out_shape=jax.ShapeDtypeStruct((t, n), F32),
        compiler_params=_cparams(("parallel", "parallel")),
        name=name,
    )(h, w)


def _rope_t(t_t, cos_t, sin_t):
    half = ROPE_DIM // 2
    rows = []
    for base in range(0, t_t.shape[0], HEAD_DIM):
        t1, t2 = t_t[base:base + half], t_t[base + half:base + ROPE_DIM]
        rows += [t1 * cos_t - t2 * sin_t, t2 * cos_t + t1 * sin_t, t_t[base + ROPE_DIM:base + HEAD_DIM]]
    return jnp.concatenate(rows, axis=0)


def _proj_q_kernel(a_ref, w_ref, cos_ref, sin_ref, qn_ref, qr_ref):
    acc = jnp.dot(a_ref[...], w_ref[...], preferred_element_type=F32)
    cos_t, sin_t = cos_ref[...], sin_ref[...]
    for c in range(acc.shape[1] // LANES):
        t_t = acc[:, c * LANES:(c + 1) * LANES].T
        qn_ref[0, c * LANES:(c + 1) * LANES, :] = (t_t * SCALE).astype(qn_ref.dtype)
        qr_ref[0, c * LANES:(c + 1) * LANES, :] = (_rope_t(t_t, cos_t, sin_t) * SCALE).astype(qr_ref.dtype)


def _proj_q(h, w, tabs, bsz, s):
    t, k = h.shape
    n = w.shape[1]
    spb = s // ROW_TILE
    tab_spec = pl.BlockSpec((ROPE_DIM // 2, ROW_TILE), lambda i: (0, i))
    out_spec = pl.BlockSpec((1, n, ROW_TILE), lambda i: (i // spb, 0, i % spb))
    return pl.pallas_call(
        _proj_q_kernel,
        grid=(t // ROW_TILE,),
        in_specs=[pl.BlockSpec((ROW_TILE, k), lambda i: (i, 0)),
                  pl.BlockSpec((k, n), lambda i: (0, 0)),
                  tab_spec, tab_spec],
        out_specs=[out_spec, out_spec],
        out_shape=[jax.ShapeDtypeStruct((bsz, n, s), BF16)] * 2,
        compiler_params=_cparams(("parallel",)),
        name="proj_q",
    )(h, w, *tabs)


def _proj_kv_kernel(a_ref, w_ref, cos_ref, sin_ref, kvc_ref, ks_ref, kw_ref, vst_ref, vwt_ref, kvc_scr):
    acc = jnp.dot(a_ref[...], w_ref[...], preferred_element_type=F32)
    cos_t, sin_t = cos_ref[...], sin_ref[...]
    tm = acc.shape[0]
    gw = KV_GROUPS * HEAD_DIM
    rows = tm // CMP_STRIDE
    low = lax.broadcasted_iota(jnp.int32, (rows, LANES), 1) < HEAD_DIM
    for c in range(2 * gw // LANES):
        kvc_scr[c] = acc[:, c * LANES:(c + 1) * LANES]
        for j in range(CMP_STRIDE // 2):
            e = kvc_scr[c, pl.ds(2 * j, rows, stride=CMP_STRIDE), :]
            o = kvc_scr[c, pl.ds(2 * j + 1, rows, stride=CMP_STRIDE), :]
            kvc_ref[0, 2 * c, :, j * LANES:(j + 1) * LANES] = jnp.where(low, e, pltpu.roll(o, HEAD_DIM, 1))
            kvc_ref[0, 2 * c + 1, :, j * LANES:(j + 1) * LANES] = jnp.where(low, pltpu.roll(e, HEAD_DIM, 1), o)
    lane = lax.broadcasted_iota(jnp.int32, (tm, LANES), 1)
    key_pad = jnp.where(lane == HEAD_DIM, 1.0, 0.0)
    ones_rows = jnp.where(lax.broadcasted_iota(jnp.int32, (V_ROWS - HEAD_DIM, tm), 0) == 0, 1.0, 0.0)
    for base, k_ref, vt_ref in ((2 * gw, ks_ref, vst_ref), (4 * gw, kw_ref, vwt_ref)):
        vtile = vt_ref.shape[-1]
        for c in range(gw // LANES):
            kk = _rope_t(acc[:, base + c * LANES: base + (c + 1) * LANES].T, cos_t, sin_t).T
            vv = acc[:, base + gw + c * LANES: base + gw + (c + 1) * LANES].T
            for half in range(2):
                g = 2 * c + half
                k_lo = kk if half == 0 else pltpu.roll(kk, HEAD_DIM, 1)
                k_ref[0, g] = jnp.where(lane < HEAD_DIM, k_lo, key_pad).astype(k_ref.dtype)
                v_aug = jnp.concatenate([vv[half * HEAD_DIM:(half + 1) * HEAD_DIM], ones_rows], axis=0)
                for j in range(ROW_TILE // vtile):
                    vt_ref[0, g, j] = v_aug[:, j * vtile:(j + 1) * vtile].astype(vt_ref.dtype)


def _proj_kv(h, w, tabs, bsz, s):
    t, k = h.shape
    n = w.shape[1]
    spb = s // ROW_TILE
    tab_spec = pl.BlockSpec((ROPE_DIM // 2, ROW_TILE), lambda i: (0, i))
    k_spec = pl.BlockSpec((1, KV_GROUPS, ROW_TILE, LANES), lambda i: (i // spb, 0, i % spb, 0))
    k_shape = jax.ShapeDtypeStruct((bsz, KV_GROUPS, s, LANES), BF16)

    def vt(tile):
        return (pl.BlockSpec((1, KV_GROUPS, ROW_TILE // tile, V_ROWS, tile), lambda i: (i // spb, 0, i % spb, 0, 0)),
                jax.ShapeDtypeStruct((bsz, KV_GROUPS, s // tile, V_ROWS, tile), BF16))

    (vs_spec, vs_shape), (vw_spec, vw_shape) = vt(KEY_STEP), vt(Q_BLOCK)
    return pl.pallas_call(
        _proj_kv_kernel,
        grid=(t // ROW_TILE,),
        in_specs=[pl.BlockSpec((ROW_TILE, k), lambda i: (i, 0)),
                  pl.BlockSpec((k, n), lambda i: (0, 0)),
                  tab_spec, tab_spec],
        out_specs=[pl.BlockSpec((1, 2 * KV_GROUPS, ROW_TILE // CMP_STRIDE, CMP_STRIDE * HEAD_DIM),
                                lambda i: (i // spb, 0, i % spb, 0)),
                   k_spec, k_spec, vs_spec, vw_spec],
        out_shape=[jax.ShapeDtypeStruct((bsz, 2 * KV_GROUPS, s // CMP_STRIDE, CMP_STRIDE * HEAD_DIM), F32),
                   k_shape, k_shape, vs_shape, vw_shape],
        scratch_shapes=[pltpu.VMEM((2 * KV_GROUPS * HEAD_DIM // LANES, ROW_TILE, LANES), F32)],
        compiler_params=_cparams(("parallel",)),
        name="proj_kv",
    )(h, w, *tabs)


def _ssd_kernel(z_ref, xs_ref, bc_ref, misc_ref, convw_ref, convb_ref, dtb_ref, alog_ref,
                dskip_ref, gnorm_ref, selb_ref, sele_ref, wout_ref, o_ref,
                tail_ref, act_ref, state_ref, cb_ref, yoff_ref):
    c = pl.program_id(1)
    L = CHUNK
    gw = SSM_HPG * SSM_HEAD_DIM

    @pl.when(c == 0)
    def _():
        state_ref[...] = jnp.zeros_like(state_ref)
        tail_ref[...] = jnp.zeros_like(tail_ref)

    cw = 512
    first_row = lax.broadcasted_iota(jnp.int32, (8, cw), 0) == 0
    for cc in range(SSM_XBC // cw):
        cols = slice(cc * cw, (cc + 1) * cw)
        u = xs_ref[:, cols] if cc < D_INNER // cw else bc_ref[:, cc * cw - D_INNER:(cc + 1) * cw - D_INNER]
        a = convw_ref[0:1, cols] * u
        for k in range(1, SSM_CONV):
            shifted = pltpu.roll(a, 1, 0)
            shifted = jnp.concatenate([jnp.where(first_row, tail_ref[k - 1:k, cols], shifted[0:8]), shifted[8:]], axis=0)
            tail_ref[k - 1:k, cols] = a[L - 1:L]
            a = convw_ref[k:k + 1, cols] * u + shifted
        act_ref[:, cols] = _silu(a + convb_ref[:, cols])

    lane = lax.broadcasted_iota(jnp.int32, (L, LANES), 1)
    row = lax.broadcasted_iota(jnp.int32, (L, LANES), 0)
    head_lane = lane < SSM_HEADS
    raw = misc_ref[...] + dtb_ref[...]
    dt = jnp.where(head_lane, jnp.maximum(raw, 0.0) + jnp.log1p(jnp.exp(-jnp.abs(raw))), 0.0)
    a_row = jnp.where(head_lane[0:1], -jnp.exp(alog_ref[...]), 0.0)
    cs = dt * a_row
    sh = 1
    while sh < L:
        cs = cs + jnp.where(row >= sh, pltpu.roll(cs, sh, 0), 0.0)
        sh *= 2
    tot = cs[L - 1:L, :]
    w_state = dt * jnp.exp(tot - cs)
    cs_t, dt_t, w_t = cs.T, dt.T, w_state.T

    hi, mid, lo = _split3(cs)
    packed = (hi.astype(F32) + pltpu.roll(mid.astype(F32), SSM_HEADS, 1)
              + pltpu.roll(lo.astype(F32), 2 * SSM_HEADS, 1)).astype(BF16)

    tri = lax.broadcasted_iota(jnp.int32, (L, L), 0) >= lax.broadcasted_iota(jnp.int32, (L, L), 1)
    brow = lax.broadcasted_iota(jnp.int32, (SSM_HPG * L, gw), 0) // L
    bcol = lax.broadcasted_iota(jnp.int32, (SSM_HPG * L, gw), 1) // SSM_HEAD_DIM
    blockmask = brow == bcol

    def group_bc(g):
        b_g = act_ref[:, D_INNER + g * SSM_STATE: D_INNER + (g + 1) * SSM_STATE]
        c_g = act_ref[:, D_INNER + SSM_GROUPS * SSM_STATE + g * SSM_STATE:
                      D_INNER + SSM_GROUPS * SSM_STATE + (g + 1) * SSM_STATE]
        return b_g, c_g.astype(BF16)

    for g in range(SSM_GROUPS):
        b_g, c_bf = group_bc(g)
        cb_ref[g] = lax.dot_general(c_bf, b_g.astype(BF16), (((1,), (1,)), ((), ())),
                                    preferred_element_type=F32)
        yoff_ref[g] = jnp.dot(c_bf, state_ref[g].astype(BF16), preferred_element_type=F32)

    for g in range(SSM_GROUPS):
        col_b = jnp.dot(packed, selb_ref[:, g * SSM_HPG * L:(g + 1) * SSM_HPG * L], preferred_element_type=F32)
        exp_e = jnp.exp(jnp.dot(packed, sele_ref[:, g * gw:(g + 1) * gw], preferred_element_type=F32))
        etot_e = exp_e[L - 1:L, :]
        xs_g = act_ref[:, g * gw:(g + 1) * gw]
        b_g, _ = group_bc(g)
        cb = cb_ref[g]
        b_t = b_g.T
        tops, bots = [], []
        for j in range(SSM_HPG):
            h = g * SSM_HPG + j
            decay = jnp.exp(jnp.where(tri, col_b[:, j * L:(j + 1) * L] - cs_t[h:h + 1, :], NEG))
            tops.append((cb * decay * dt_t[h:h + 1, :]).astype(BF16))
            bots.append((b_t * w_t[h:h + 1, :]).astype(BF16))
        lhs = jnp.concatenate([jnp.concatenate(tops, axis=1), jnp.concatenate(bots, axis=1)], axis=0)
        xs_bf = xs_g.astype(BF16)
        xs_bd = jnp.where(blockmask, jnp.concatenate([xs_bf] * SSM_HPG, axis=0), jnp.zeros((), BF16))
        res = jnp.dot(lhs, xs_bd, preferred_element_type=F32)
        y_off = yoff_ref[g] * exp_e
        state_ref[g] = state_ref[g] * etot_e + res[L:2 * L]
        y = res[0:L] + y_off + dskip_ref[:, g * gw:(g + 1) * gw] * xs_g
        zg = z_ref[:, g * gw:(g + 1) * gw]
        y = y * _silu(zg)
        ms = jnp.mean(y * y, axis=-1, keepdims=True)
        act_ref[:, g * gw:(g + 1) * gw] = y * lax.rsqrt(ms + NORM_EPS) * gnorm_ref[:, g * gw:(g + 1) * gw]

    o_ref[...] = jnp.dot(act_ref[:, 0:D_INNER].astype(BF16), wout_ref[...], preferred_element_type=F32)


def _ssd(raw, raw_gate, conv_w, conv_b, dt_bias, a_log, d_skip, g_ssm_norm, w_ssm, bsz, s):
    t = raw.shape[0]
    nc = s // CHUNK
    blk = lambda cb: pl.BlockSpec((CHUNK, 2048), lambda b, c, cb=cb: (b * nc + c, cb))
    const = lambda shape: pl.BlockSpec(shape, lambda b, c: (0,) * len(shape))
    k = jnp.arange(LANES)
    selb = ((k[:, None] < 3 * SSM_HEADS) & ((k[:, None] % SSM_HEADS) == (jnp.arange(SSM_HEADS * LANES)[None, :] // LANES)))
    sele = ((k[:, None] < 3 * SSM_HEADS) & ((k[:, None] % SSM_HEADS) == (jnp.arange(D_INNER)[None, :] // SSM_HEAD_DIM)))
    pad = lambda v: jnp.pad(v.astype(F32), (0, LANES - SSM_HEADS)).reshape(1, LANES)
    return pl.pallas_call(
        _ssd_kernel,
        grid=(bsz, nc),
        in_specs=[blk(RAW_Z // 2048), blk(RAW_XS // 2048), blk(RAW_BC // 2048),
                  pl.BlockSpec((CHUNK, LANES), lambda b, c: (b * nc + c, GATE_MISC // LANES)),
                  const((SSM_CONV, SSM_XBC)), const((1, SSM_XBC)), const((1, LANES)), const((1, LANES)),
                  const((1, D_INNER)), const((1, D_INNER)),
                  const((LANES, SSM_HEADS * LANES)), const((LANES, D_INNER)),
                  const((D_INNER, D_MODEL))],
        out_specs=pl.BlockSpec((CHUNK, D_MODEL), lambda b, c: (b * nc + c, 0)),
        out_shape=jax.ShapeDtypeStruct((t, D_MODEL), F32),
        scratch_shapes=[pltpu.VMEM((8, SSM_XBC), F32),
                        pltpu.VMEM((CHUNK, SSM_XBC), F32),
                        pltpu.VMEM((SSM_GROUPS, SSM_STATE, SSM_HPG * SSM_HEAD_DIM), F32),
                        pltpu.VMEM((SSM_GROUPS, CHUNK, CHUNK), F32),
                        pltpu.VMEM((SSM_GROUPS, CHUNK, SSM_HPG * SSM_HEAD_DIM), F32)],
        compiler_params=_cparams(("parallel", "arbitrary")),
        name="ssd_mixer",
    )(raw, raw, raw, raw_gate, conv_w.astype(F32), conv_b.reshape(1, SSM_XBC).astype(F32),
      pad(dt_bias), pad(a_log), jnp.repeat(d_skip.astype(F32), SSM_HEAD_DIM).reshape(1, D_INNER),
      g_ssm_norm.reshape(1, D_INNER).astype(F32), selb.astype(BF16), sele.astype(BF16), w_ssm.astype(BF16))


def _compress_kernel(u_ref, pelo_ref, pehi_ref, w1_ref, w2_ref, o_ref):
    u = u_ref[0]
    half = CMP_STRIDE * HEAD_DIM
    a = jnp.dot((u + pelo_ref[0, 0]).astype(BF16), w1_ref[0, 0:half, :], preferred_element_type=F32)
    b = jnp.dot((u + pehi_ref[0, 0]).astype(BF16), w1_ref[0, half:2 * half, :], preferred_element_type=F32)
    pre = a + pltpu.roll(b, u.shape[0] - 1, 0)
    hidden = _silu(pre)
    out = jnp.dot(hidden.astype(BF16), w2_ref[0], preferred_element_type=F32)
    lane = lax.broadcasted_iota(jnp.int32, out.shape, 1)
    o_ref[0] = jnp.where(lane == HEAD_DIM, 1.0, out).astype(o_ref.dtype)


def _compress(u, pe, w1, w2):
    n, rows, width = u.shape
    kv_of = lambda i: (i // KV_GROUPS) % 2
    return pl.pallas_call(
        _compress_kernel,
        grid=(n,),
        in_specs=[pl.BlockSpec((1, rows, width), lambda i: (i, 0, 0)),
                  pl.BlockSpec((1, 1, 1, width), lambda i: (kv_of(i), 0, 0, 0)),
                  pl.BlockSpec((1, 1, 1, width), lambda i: (kv_of(i), 1, 0, 0)),
                  pl.BlockSpec((1, 2 * width, CMP_HIDDEN), lambda i: (kv_of(i), 0, 0)),
                  pl.BlockSpec((1, CMP_HIDDEN, LANES), lambda i: (kv_of(i), 0, 0))],
        out_specs=pl.BlockSpec((1, rows, LANES), lambda i: (i, 0, 0)),
        out_shape=jax.ShapeDtypeStruct((n, rows, LANES), BF16),
        compiler_params=_cparams(("parallel",)),
        name="compress",
    )(u, pe, pe, w1, w2)


def _nsa_kernel(qn_ref, qr_ref, kc_ref, vct_ref, ks_ref, vst_ref, kw_ref, vwt_ref, gate_ref, ovt_ref, oh_ref,
                eye_ref, wlo_ref, whi_ref,
                o_ref, gt_ref, qa_ref, sa_ref, sb_ref, pa_ref, pb_ref, sc_ref, accc_ref, invc_ref, imp_ref, sw_ref, pc_ref,
                *, topk):
    gp = pl.program_id(1)
    qb = pl.program_id(2)
    groups = range(NSA_GROUPS)
    nq = ATT_HPG * Q_BLOCK
    ncmp = kc_ref.shape[1]
    nsel = ovt_ref.shape[0]
    q0 = qb * Q_BLOCK
    tq = q0 + lax.broadcasted_iota(jnp.int32, (1, nq), 1) % Q_BLOCK

    def heads_on_lanes(ref, gi):
        base = gi * ATT_HPG * HEAD_DIM
        return jnp.concatenate([ref[0, base + r * HEAD_DIM:base + (r + 1) * HEAD_DIM, :]
                                for r in range(ATT_HPG)], axis=1)

    qn_t = [heads_on_lanes(qn_ref, gi) for gi in groups]
    qr_t = [heads_on_lanes(qr_ref, gi) for gi in groups]

    def q_operand(q_t, flag, table):
        flag_rows = jnp.concatenate([flag, jnp.zeros((LANES - HEAD_DIM - 1, nq), F32)], axis=0).astype(BF16)
        return jnp.concatenate([q_t, flag_rows] + ([table] if table is not None else []), axis=0)

    no_flag = jnp.zeros((1, nq), F32)

    n_chunks = ncmp // LANES
    chunk_span = LANES * CMP_STRIDE
    c_hi = jnp.minimum((q0 + Q_BLOCK - CMP_LEN) // chunk_span, n_chunks - 1)

    def compressed_branch(n_live):
        live = n_live * LANES
        ov_live = jnp.concatenate([ovt_ref[:, piece * ncmp:piece * ncmp + live] for piece in range(3)], axis=1)
        for c in range(n_live):
            for gi in groups:
                s = jnp.dot(kc_ref[gi, c * LANES:(c + 1) * LANES, :], q_operand(qn_t[gi], no_flag, None),
                            preferred_element_type=F32)
                if c >= n_live - 2:
                    ends = (c * chunk_span + CMP_LEN - 1
                            + lax.broadcasted_iota(jnp.int32, (LANES, nq), 0) * CMP_STRIDE)
                    s = jnp.where(ends <= tq, s, NEG)
                sc_ref[gi, c * LANES:(c + 1) * LANES, :] = s
        m_cs, invs = [], []
        for gi in groups:
            s_c = sc_ref[gi, 0:live, :]
            m_cs.append(jnp.max(s_c, axis=0, keepdims=True))
            p_c = jnp.exp2(s_c - m_cs[gi])
            sc_ref[gi, 0:live, :] = p_c
            pc_ref[gi, 0:live, :] = p_c.astype(BF16)
        for gi in groups:
            acc = jnp.dot(vct_ref[gi, :, 0:live], pc_ref[gi, 0:live, :], preferred_element_type=F32)
            invs.append(jnp.where(m_cs[gi] > 0.5 * NEG, 1.0 / acc[HEAD_DIM:HEAD_DIM + 1], 0.0))
            accc_ref[gi] = acc
            invc_ref[gi] = jnp.broadcast_to(invs[gi], (8, nq))
        for gi in groups:
            inv = invs[gi]
            p_sum = sc_ref[gi, 0:live, 0:Q_BLOCK] * inv[:, 0:Q_BLOCK]
            for r in range(1, ATT_HPG):
                p_sum = p_sum + sc_ref[gi, 0:live, r * Q_BLOCK:(r + 1) * Q_BLOCK] * inv[:, r * Q_BLOCK:(r + 1) * Q_BLOCK]
            imp_ref[gi] = jnp.dot(ov_live, jnp.concatenate(_split3(p_sum), axis=0), preferred_element_type=F32)

    for k in range(n_chunks):
        pl.when(c_hi == k)(functools.partial(compressed_branch, k + 1))
    acc_c = [accc_ref[gi] for gi in groups]
    inv_c = [invc_ref[gi, 0:1, :] for gi in groups]
    imp = [imp_ref[gi] for gi in groups]

    n_wt = WINDOW // Q_BLOCK + 1
    eye = eye_ref[...]
    for gi in groups:
        for i in range(n_wt):
            kt = qb - (n_wt - 1) + i
            k0 = pl.multiple_of(jnp.maximum(kt, 0) * Q_BLOCK, Q_BLOCK)
            keys = kw_ref[gi, pl.ds(k0, Q_BLOCK), :]
            flag = jnp.where(kt >= 0, no_flag, NEG)
            if i == 0:
                s = jnp.dot(jnp.concatenate([keys, eye], axis=1), q_operand(qr_t[gi], flag, wlo_ref[...]),
                            preferred_element_type=F32)
            elif i == n_wt - 1:
                s = jnp.dot(jnp.concatenate([keys, eye], axis=1), q_operand(qr_t[gi], flag, whi_ref[...]),
                            preferred_element_type=F32)
            else:
                s = jnp.dot(keys, q_operand(qr_t[gi], flag, None), preferred_element_type=F32)
            sw_ref[gi, i * Q_BLOCK:(i + 1) * Q_BLOCK, :] = s
    acc_w = []
    for gi in groups:
        v_w = [vwt_ref[gi, jnp.maximum(qb - (n_wt - 1) + i, 0)] for i in range(n_wt)]
        s_w = sw_ref[gi]
        p_w = jnp.exp2(s_w - jnp.max(s_w, axis=0, keepdims=True))
        acc_w.append(jnp.dot(jnp.concatenate(v_w, axis=1), p_w.astype(BF16), preferred_element_type=F32))

    blk = lax.broadcasted_iota(jnp.int32, (nsel, Q_BLOCK), 0)
    cur = (q0 + lax.broadcasted_iota(jnp.int32, (nsel, Q_BLOCK), 1)) // SEL_BLOCK
    valid = blk <= cur
    forced = valid & ((blk == 0) | (blk == cur) | (blk == cur - 1))
    val = [jnp.where(forced, REMOVED, jnp.where(valid, imp[gi], NEG)) for gi in groups]
    blk_f = blk.astype(F32)
    for _ in range(topk - 3):
        for gi in groups:
            mx = jnp.max(val[gi], axis=0, keepdims=True)
            first = jnp.min(jnp.where(val[gi] == mx, blk_f, float(nsel)), axis=0, keepdims=True)
            val[gi] = jnp.where(blk_f == first, REMOVED, val[gi])

    pad_rows = [jnp.zeros((LANES - nsel, nq), F32)] if nsel < LANES else []
    for gi in groups:
        bias = jnp.concatenate([jnp.where(val[gi] == REMOVED, 0.0, NEG)] * ATT_HPG, axis=1)
        qa_ref[gi] = q_operand(qr_t[gi], no_flag, jnp.concatenate([bias] + pad_rows, axis=0).astype(BF16))

    def sel_scores(gi, i):
        k0 = pl.multiple_of(i * KEY_STEP, KEY_STEP)
        lhs = jnp.concatenate([ks_ref[gi, pl.ds(k0, KEY_STEP), :], oh_ref[pl.ds(k0, KEY_STEP), :]], axis=1)
        return jnp.dot(lhs, qa_ref[gi], preferred_element_type=F32)

    row_k = lax.broadcasted_iota(jnp.int32, (KEY_STEP, nq), 0)

    def softmax_step(s_ref, p_ref, gi, m, first_key):
        s = s_ref[gi]
        if first_key is not None:
            s = jnp.where(row_k <= tq - first_key, s, NEG)
        m_new = jnp.maximum(m, jnp.max(s, axis=0, keepdims=True))
        p_ref[gi] = jnp.exp2(s - m_new).astype(BF16)
        return m_new, jnp.exp2(m - m_new)

    def pv(gi, acc, alpha, p_ref, i):
        return acc * alpha + jnp.dot(vst_ref[gi, i], p_ref[gi], preferred_element_type=F32)

    def sel_pair(j, carries, final):
        a = 2 * j
        m, acc, alpha_b = ([c[k] for c in carries] for k in range(3))
        alpha_a = [None] * NSA_GROUPS
        for gi in groups:
            sb_ref[gi] = sel_scores(gi, a + 1)
            acc[gi] = pv(gi, acc[gi], alpha_b[gi], pb_ref, jnp.maximum(a - 1, 0))
            m[gi], alpha_a[gi] = softmax_step(sa_ref, pa_ref, gi, m[gi], a * KEY_STEP if final else None)
            if not final:
                sa_ref[gi] = sel_scores(gi, a + 2)
            acc[gi] = pv(gi, acc[gi], alpha_a[gi], pa_ref, a)
            m[gi], alpha_b[gi] = softmax_step(sb_ref, pb_ref, gi, m[gi], (a + 1) * KEY_STEP if final else None)
        return tuple((m[gi], acc[gi], alpha_b[gi]) for gi in groups)

    n_pairs = (q0 + Q_BLOCK - 1) // (2 * KEY_STEP) + 1
    for gi in groups:
        sa_ref[gi] = sel_scores(gi, 0)
    pb_ref[...] = jnp.zeros_like(pb_ref)
    init = tuple((jnp.full((1, nq), NEG, F32), jnp.zeros((V_ROWS, nq), F32), jnp.ones((1, nq), F32))
                 for gi in groups)
    carries = lax.fori_loop(0, n_pairs - 1, lambda j, c: sel_pair(j, c, False), init)
    carries = sel_pair(n_pairs - 1, carries, True)
    acc_s = [pv(gi, carries[gi][1], carries[gi][2], pb_ref, 2 * n_pairs - 1) for gi in groups]

    gt_ref[...] = _sigmoid(gate_ref[...]).T
    hd = HEAD_DIM
    for gi in groups:
        def gate_row(branch, gi=gi):
            base = MISC_GATE0 + branch * ATT_HEADS + (gp * NSA_GROUPS + gi) * ATT_HPG
            return jnp.concatenate([gt_ref[pl.ds(base + r, 1), :] for r in range(ATT_HPG)], axis=1)

        o = ((gate_row(0) * inv_c[gi]) * acc_c[gi][0:hd] + (gate_row(1) / acc_s[gi][hd:hd + 1]) * acc_s[gi][0:hd]
             + (gate_row(2) / acc_w[gi][hd:hd + 1]) * acc_w[gi][0:hd])
        for r in range(ATT_HPG):
            row0 = (gi * ATT_HPG + r) * HEAD_DIM
            o_ref[0, row0:row0 + HEAD_DIM, :] = o[:, r * Q_BLOCK:(r + 1) * Q_BLOCK].astype(o_ref.dtype)


def _nsa(qn_t, qr_t, kc, vc_t, ks, vs_t, kw, vw_t, raw_gate, bsz, s):
    nqb = s // Q_BLOCK
    ncmp = s // CMP_STRIDE
    nsel = s // SEL_BLOCK
    topk = min(SEL_TOPK, nsel)
    gw = ATT_HPG * HEAD_DIM
    ci = jnp.arange(ncmp)[None, :]
    sj = jnp.arange(nsel)[:, None]
    ov_t = (ci * CMP_STRIDE < (sj + 1) * SEL_BLOCK) & (ci * CMP_STRIDE + CMP_LEN > sj * SEL_BLOCK) \
        & (ci < ncmp - 1)
    ov_t3 = jnp.concatenate([ov_t.astype(BF16)] * 3, axis=1)
    assert nsel <= LANES
    onehot = (jnp.arange(s)[:, None] // SEL_BLOCK == jnp.arange(LANES)[None, :]).astype(BF16)
    eye = jnp.eye(Q_BLOCK, dtype=BF16)
    key_row = jnp.arange(Q_BLOCK)[:, None]
    q_lane = jnp.arange(ATT_HPG * Q_BLOCK)[None, :] % Q_BLOCK
    win_lo = jnp.where(key_row > q_lane, 0.0, NEG).astype(BF16)
    win_hi = jnp.where(key_row <= q_lane, 0.0, NEG).astype(BF16)
    once = pl.Buffered(1)
    const2 = lambda shape: pl.BlockSpec(shape, lambda b, g, i: (0, 0), pipeline_mode=once)
    ng = NSA_GROUPS
    gpb = KV_GROUPS // ng
    q_spec = pl.BlockSpec((1, ng * gw, Q_BLOCK), lambda b, g, i: (b, g, i))
    per_bg = lambda shape: pl.BlockSpec((ng,) + shape, lambda b, g, i: (b * gpb + g,) + (0,) * len(shape),
                                        pipeline_mode=once)
    nql = ATT_HPG * Q_BLOCK
    return pl.pallas_call(
        functools.partial(_nsa_kernel, topk=topk),
        grid=(bsz, gpb, nqb),
        in_specs=[q_spec, q_spec,
                  per_bg((ncmp, LANES)), per_bg((V_ROWS, ncmp)),
                  per_bg((s, LANES)), per_bg((s // KEY_STEP, V_ROWS, KEY_STEP)),
                  per_bg((s, LANES)), per_bg((s // Q_BLOCK, V_ROWS, Q_BLOCK)),
                  pl.BlockSpec((Q_BLOCK, LANES), lambda b, g, i: (b * nqb + i, GATE_MISC // LANES)),
                  const2((nsel, 3 * ncmp)), const2((s, LANES)),
                  const2((Q_BLOCK, Q_BLOCK)), const2((Q_BLOCK, ATT_HPG * Q_BLOCK)),
                  const2((Q_BLOCK, ATT_HPG * Q_BLOCK))],
        out_specs=pl.BlockSpec((1, ng * gw, Q_BLOCK), lambda b, g, i: (b, g, i)),
        out_shape=jax.ShapeDtypeStruct((bsz, ATT_HEADS * HEAD_DIM, s), BF16),
        scratch_shapes=[pltpu.VMEM((LANES, Q_BLOCK), F32),
                        pltpu.VMEM((ng, 2 * LANES, nql), BF16),
                        pltpu.VMEM((ng, KEY_STEP, nql), F32),
                        pltpu.VMEM((ng, KEY_STEP, nql), F32),
                        pltpu.VMEM((ng, KEY_STEP, nql), BF16),
                        pltpu.VMEM((ng, KEY_STEP, nql), BF16),
                        pltpu.VMEM((ng, ncmp, nql), F32),
                        pltpu.VMEM((ng, V_ROWS, nql), F32),
                        pltpu.VMEM((ng, 8, nql), F32),
                        pltpu.VMEM((ng, nsel, Q_BLOCK), F32),
                        pltpu.VMEM((ng, (WINDOW // Q_BLOCK + 1) * Q_BLOCK, nql), F32),
                        pltpu.VMEM((ng, ncmp, nql), BF16)],
        compiler_params=_cparams(("parallel", "parallel", "arbitrary")),
        name="nsa_sweep",
    )(qn_t, qr_t, kc, vc_t, ks, vs_t, kw, vw_t, raw_gate, ov_t3, onehot, eye, win_lo, win_hi)


def _merge_mlp_kernel(ot_ref, yssm_ref, mix_ref, x_ref, wab_ref, wo_ref, g1_ref, wup_ref, wdn_ref, g2_ref, o_ref):
    o = ot_ref[0].astype(F32).T.astype(BF16)
    y_att = jnp.dot(o, wab_ref[...], preferred_element_type=F32)
    gates = _sigmoid(mix_ref[...])
    mixed = gates[:, 0:D_MODEL] * yssm_ref[...] + gates[:, D_MODEL:2 * D_MODEL] * y_att
    x = x_ref[...] + jnp.dot(mixed.astype(BF16), wo_ref[...], preferred_element_type=F32)
    ms = jnp.mean(x * x, axis=-1, keepdims=True)
    h = (x * lax.rsqrt(ms + NORM_EPS) * g1_ref[...]).astype(BF16)
    up = jnp.maximum(jnp.dot(h, wup_ref[...], preferred_element_type=F32), 0.0)
    y = x + jnp.dot((up * up).astype(BF16), wdn_ref[...], preferred_element_type=F32)
    ms2 = jnp.mean(y * y, axis=-1, keepdims=True)
    o_ref[...] = y * lax.rsqrt(ms2 + NORM_EPS) * g2_ref[...]


def _merge_mlp(o_t, y_ssm, raw_gate, x2d, w_ab, w_o, g1, w_up, w_down, g2, s):
    t = x2d.shape[0]
    tm = ROW_TILE
    spb = s // tm
    row_spec = pl.BlockSpec((tm, D_MODEL), lambda i: (i, 0))
    const = lambda shape: pl.BlockSpec(shape, lambda i: (0, 0), pipeline_mode=pl.Buffered(1))
    return pl.pallas_call(
        _merge_mlp_kernel,
        grid=(t // tm,),
        in_specs=[pl.BlockSpec((1, D_MODEL, tm), lambda i: (i // spb, 0, i % spb)),
                  row_spec,
                  pl.BlockSpec((tm, 2 * D_MODEL), lambda i: (i, GATE_MIX // (2 * D_MODEL))),
                  row_spec, const((D_MODEL, D_MODEL)), const((D_MODEL, D_MODEL)),
                  const((1, D_MODEL)), const((D_MODEL, MLP_HIDDEN)), const((MLP_HIDDEN, D_MODEL)),
                  const((1, D_MODEL))],
        out_specs=row_spec,
        out_shape=jax.ShapeDtypeStruct((t, D_MODEL), F32),
        compiler_params=_cparams(("parallel",)),
        name="merge_mlp",
    )(o_t, y_ssm, raw_gate, x2d, w_ab, w_o, g1.reshape(1, D_MODEL), w_up, w_down, g2.reshape(1, D_MODEL))


def _rope_tables(positions):
    inv_freq = ROPE_THETA ** (-jnp.arange(0, ROPE_DIM, 2, dtype=F32) / ROPE_DIM)
    ang = positions.astype(F32).reshape(1, -1) * inv_freq[:, None]
    return jnp.cos(ang), jnp.sin(ang)


def kernel(x, positions, g_norm_mix, w_in, conv_w, conv_b, dt_bias, a_log, d_skip, g_ssm_norm, w_ssm_branch, cmp_pe_k, cmp_pe_v, w_cmp_k1, w_cmp_k2, w_cmp_v1, w_cmp_v2, w_attn_branch, w_o, g_norm_mlp, w_up, w_down, g_norm_final):
    bsz, s, d = x.shape
    assert d == D_MODEL and s % (2 * ROW_TILE) == 0 and w_in.shape[0] == 1
    t = bsz * s
    x2d = x.reshape(t, d)
    tabs = _rope_tables(positions)

    w = w_in[0].astype(BF16)
    o_dt, o_q, o_kv, o_ag, o_mg = 6144, 6176, 7200, 8736, 8784
    misc_pad = jnp.zeros((d, GATE_COLS - GATE_MISC - SSM_HEADS - 3 * ATT_HEADS), BF16)
    w_gate = jnp.concatenate([w[:, o_mg:], w[:, o_dt:o_q], w[:, o_ag:o_mg], misc_pad], axis=1)
    w_q = w[:, o_q:o_kv]
    w_kv = w[:, o_kv:o_ag]

    h = _rmsnorm(x2d, g_norm_mix[0])
    raw = _proj_raw(h, w, SSM_COLS, SSM_TN, 2 * ROW_TILE, "proj_ssm")
    raw_gate = _proj_raw(h, w_gate, GATE_COLS, GATE_COLS, 2 * ROW_TILE, "proj_gate")
    qn_t, qr_t = _proj_q(h, w_q, tabs, bsz, s)
    kvc, ks, kw, vs_t, vw_t = _proj_kv(h, w_kv, tabs, bsz, s)

    y_ssm = _ssd(raw, raw_gate, conv_w[0], conv_b[0], dt_bias[0], a_log[0], d_skip[0], g_ssm_norm[0],
                 w_ssm_branch[0], bsz, s)

    rows = s // CMP_STRIDE
    u = kvc.reshape(bsz * 2 * KV_GROUPS, rows, CMP_STRIDE * HEAD_DIM)
    pe = jnp.stack([cmp_pe_k[0], cmp_pe_v[0]]).astype(F32).reshape(2, 2, 1, CMP_STRIDE * HEAD_DIM)
    w1 = jnp.stack([w_cmp_k1[0], w_cmp_v1[0]]).astype(BF16)
    w2 = jnp.pad(jnp.stack([w_cmp_k2[0], w_cmp_v2[0]]).astype(BF16), ((0, 0), (0, 0), (0, LANES - HEAD_DIM)))
    cmp = _compress(u, pe, w1, w2).reshape(bsz, 2, KV_GROUPS, rows, LANES)
    kc = cmp[:, 0].reshape(bsz * KV_GROUPS, rows, LANES)
    vc_t = jnp.swapaxes(cmp[:, 1], -1, -2)[:, :, :V_ROWS].reshape(bsz * KV_GROUPS, V_ROWS, rows)

    flat = lambda a: a.reshape((bsz * KV_GROUPS,) + a.shape[2:])
    o_t = _nsa(qn_t, qr_t, kc, vc_t, flat(ks), flat(vs_t), flat(kw), flat(vw_t), raw_gate, bsz, s)

    out = _merge_mlp(o_t, y_ssm, raw_gate, x2d, w_attn_branch[0].astype(BF16), w_o[0].astype(BF16),
                     g_norm_mlp[0], w_up[0].astype(BF16), w_down[0].astype(BF16), g_norm_final, s)
    return out.reshape(bsz, s, d)
```

```python
import functools

import jax
import jax.numpy as jnp
from jax import lax
from jax.experimental import pallas as pl
from jax.experimental.pallas import tpu as pltpu

F32 = jnp.float32
BF16 = jnp.bfloat16

D_MODEL = 1024
D_INNER = 2048
SSM_HEADS = 32
SSM_GROUPS = 8
SSM_HPG = 4
SSM_HEAD_DIM = 64
SSM_STATE = 128
SSM_CONV = 4
CHUNK = 128
SSM_XBC = 4096
HEAD_DIM = 64
ATT_HEADS = 16
KV_GROUPS = 4
ATT_HPG = 4
CMP_LEN = 32
CMP_STRIDE = 16
CMP_HIDDEN = 256
SEL_BLOCK = 64
SEL_TOPK = 16
WINDOW = 512
Q_BLOCK = 128
ROPE_THETA = 500000.0
ROPE_DIM = 16
MLP_HIDDEN = 4096
NORM_EPS = 1e-6
NEG = -1e30
BIG = 1e30
REMOVED = -3e38
LOG2E = 1.4426950408889634
SCALE = HEAD_DIM ** -0.5 * LOG2E
KEY_STEP = 256
NSA_GROUPS = 4

LANES = 128
ROW_TILE = 512
VMEM_LIMIT = 56 * 1024 * 1024

RAW_Z, RAW_XS, RAW_BC = 0, 2048, 4096
SSM_COLS, SSM_TN = 6144, 1536
GATE_MIX, GATE_MISC, GATE_COLS = 0, 2048, 2176
MISC_GATE0 = SSM_HEADS
V_ROWS = HEAD_DIM + 16


def _cparams(sem):
    return pltpu.CompilerParams(dimension_semantics=sem, vmem_limit_bytes=VMEM_LIMIT)


def _sigmoid(x):
    return 0.5 * jnp.tanh(0.5 * x) + 0.5


def _silu(x):
    h = 0.5 * x
    return h + h * jnp.tanh(h)


def _split3(x):
    hi = x.astype(BF16)
    r1 = x - hi.astype(F32)
    mid = r1.astype(BF16)
    lo = (r1 - mid.astype(F32)).astype(BF16)
    return hi, mid, lo


def _rmsnorm_kernel(x_ref, g_ref, o_ref):
    x = x_ref[...]
    ms = jnp.mean(x * x, axis=-1, keepdims=True)
    o_ref[...] = (x * lax.rsqrt(ms + NORM_EPS) * g_ref[...]).astype(o_ref.dtype)


def _rmsnorm(x2d, g):
    t, d = x2d.shape
    return pl.pallas_call(
        _rmsnorm_kernel,
        grid=(t // ROW_TILE,),
        in_specs=[pl.BlockSpec((ROW_TILE, d), lambda i: (i, 0)),
                  pl.BlockSpec((1, d), lambda i: (0, 0))],
        out_specs=pl.BlockSpec((ROW_TILE, d), lambda i: (i, 0)),
        out_shape=jax.ShapeDtypeStruct((t, d), BF16),
        compiler_params=_cparams(("parallel",)),
        name="rmsnorm",
    )(x2d, g.reshape(1, d))


def _proj_raw_kernel(a_ref, w_ref, o_ref):
    o_ref[...] = jnp.dot(a_ref[...], w_ref[...], preferred_element_type=F32)


def _proj_raw(h, w, n, tn, tm, name):
    t, k = h.shape
    return pl.pallas_call(
        _proj_raw_kernel,
        grid=(n // tn, t // tm),
        in_specs=[pl.BlockSpec((tm, k), lambda j, i: (i, 0)),
                  pl.BlockSpec((k, tn), lambda j, i: (0, j))],
        out_specs=pl.BlockSpec((tm, tn), lambda j, i: (i, j)),
        out_shape=jax.ShapeDtypeStruct((t, n), F32),
        compiler_params=_cparams(("parallel", "parallel")),
        name=name,
    )(h, w)


def _rope_t(t_t, cos_t, sin_t):
    half = ROPE_DIM // 2
    rows = []
    for base in range(0, t_t.shape[0], HEAD_DIM):
        t1, t2 = t_t[base:base + half], t_t[base + half:base + ROPE_DIM]
        rows += [t1 * cos_t - t2 * sin_t, t2 * cos_t + t1 * sin_t, t_t[base + ROPE_DIM:base + HEAD_DIM]]
    return jnp.concatenate(rows, axis=0)


def _proj_q_kernel(a_ref, w_ref, cos_ref, sin_ref, qn_ref, qr_ref):
    acc = jnp.dot(a_ref[...], w_ref[...], preferred_element_type=F32)
    cos_t, sin_t = cos_ref[...], sin_ref[...]
    for c in range(acc.shape[1] // LANES):
        t_t = acc[:, c * LANES:(c + 1) * LANES].T
        qn_ref[0, c * LANES:(c + 1) * LANES, :] = (t_t * SCALE).astype(qn_ref.dtype)
        qr_ref[0, c * LANES:(c + 1) * LANES, :] = (_rope_t(t_t, cos_t, sin_t) * SCALE).astype(qr_ref.dtype)


def _proj_q(h, w, tabs, bsz, s):
    t, k = h.shape
    n = w.shape[1]
    spb = s // ROW_TILE
    tab_spec = pl.BlockSpec((ROPE_DIM // 2, ROW_TILE), lambda i: (0, i))
    out_spec = pl.BlockSpec((1, n, ROW_TILE), lambda i: (i // spb, 0, i % spb))
    return pl.pallas_call(
        _proj_q_kernel,
        grid=(t // ROW_TILE,),
        in_specs=[pl.BlockSpec((ROW_TILE, k), lambda i: (i, 0)),
                  pl.BlockSpec((k, n), lambda i: (0, 0)),
                  tab_spec, tab_spec],
        out_specs=[out_spec, out_spec],
        out_shape=[jax.ShapeDtypeStruct((bsz, n, s), BF16)] * 2,
        compiler_params=_cparams(("parallel",)),
        name="proj_q",
    )(h, w, *tabs)


def _proj_kv_kernel(a_ref, w_ref, cos_ref, sin_ref, kvc_ref, ks_ref, kw_ref, vst_ref, vwt_ref, kvc_scr):
    acc = jnp.dot(a_ref[...], w_ref[...], preferred_element_type=F32)
    cos_t, sin_t = cos_ref[...], sin_ref[...]
    tm = acc.shape[0]
    gw = KV_GROUPS * HEAD_DIM
    rows = tm // CMP_STRIDE
    low = lax.broadcasted_iota(jnp.int32, (rows, LANES), 1) < HEAD_DIM
    for c in range(2 * gw // LANES):
        kvc_scr[c] = acc[:, c * LANES:(c + 1) * LANES]
        for j in range(CMP_STRIDE // 2):
            e = kvc_scr[c, pl.ds(2 * j, rows, stride=CMP_STRIDE), :]
            o = kvc_scr[c, pl.ds(2 * j + 1, rows, stride=CMP_STRIDE), :]
            kvc_ref[0, 2 * c, :, j * LANES:(j + 1) * LANES] = jnp.where(low, e, pltpu.roll(o, HEAD_DIM, 1))
            kvc_ref[0, 2 * c + 1, :, j * LANES:(j + 1) * LANES] = jnp.where(low, pltpu.roll(e, HEAD_DIM, 1), o)
    lane = lax.broadcasted_iota(jnp.int32, (tm, LANES), 1)
    key_pad = jnp.where(lane == HEAD_DIM, 1.0, 0.0)
    ones_rows = jnp.where(lax.broadcasted_iota(jnp.int32, (V_ROWS - HEAD_DIM, tm), 0) == 0, 1.0, 0.0)
    for base, k_ref, vt_ref in ((2 * gw, ks_ref, vst_ref), (4 * gw, kw_ref, vwt_ref)):
        vtile = vt_ref.shape[-1]
        for c in range(gw // LANES):
            kk = _rope_t(acc[:, base + c * LANES: base + (c + 1) * LANES].T, cos_t, sin_t).T
            vv = acc[:, base + gw + c * LANES: base + gw + (c + 1) * LANES].T
            for half in range(2):
                g = 2 * c + half
                k_lo = kk if half == 0 else pltpu.roll(kk, HEAD_DIM, 1)
                k_ref[0, g] = jnp.where(lane < HEAD_DIM, k_lo, key_pad).astype(k_ref.dtype)
                v_aug = jnp.concatenate([vv[half * HEAD_DIM:(half + 1) * HEAD_DIM], ones_rows], axis=0)
                for j in range(ROW_TILE // vtile):
                    vt_ref[0, g, j] = v_aug[:, j * vtile:(j + 1) * vtile].astype(vt_ref.dtype)


def _proj_kv(h, w, tabs, bsz, s):
    t, k = h.shape
    n = w.shape[1]
    spb = s // ROW_TILE
    tab_spec = pl.BlockSpec((ROPE_DIM // 2, ROW_TILE), lambda i: (0, i))
    k_spec = pl.BlockSpec((1, KV_GROUPS, ROW_TILE, LANES), lambda i: (i // spb, 0, i % spb, 0))
    k_shape = jax.ShapeDtypeStruct((bsz, KV_GROUPS, s, LANES), BF16)

    def vt(tile):
        return (pl.BlockSpec((1, KV_GROUPS, ROW_TILE // tile, V_ROWS, tile), lambda i: (i // spb, 0, i % spb, 0, 0)),
                jax.ShapeDtypeStruct((bsz, KV_GROUPS, s // tile, V_ROWS, tile), BF16))

    (vs_spec, vs_shape), (vw_spec, vw_shape) = vt(KEY_STEP), vt(Q_BLOCK)
    return pl.pallas_call(
        _proj_kv_kernel,
        grid=(t // ROW_TILE,),
        in_specs=[pl.BlockSpec((ROW_TILE, k), lambda i: (i, 0)),
                  pl.BlockSpec((k, n), lambda i: (0, 0)),
                  tab_spec, tab_spec],
        out_specs=[pl.BlockSpec((1, 2 * KV_GROUPS, ROW_TILE // CMP_STRIDE, CMP_STRIDE * HEAD_DIM),
                                lambda i: (i // spb, 0, i % spb, 0)),
                   k_spec, k_spec, vs_spec, vw_spec],
        out_shape=[jax.ShapeDtypeStruct((bsz, 2 * KV_GROUPS, s // CMP_STRIDE, CMP_STRIDE * HEAD_DIM), F32),
                   k_shape, k_shape, vs_shape, vw_shape],
        scratch_shapes=[pltpu.VMEM((2 * KV_GROUPS * HEAD_DIM // LANES, ROW_TILE, LANES), F32)],
        compiler_params=_cparams(("parallel",)),
        name="proj_kv",
    )(h, w, *tabs)


def _ssd_kernel(z_ref, xs_ref, bc_ref, misc_ref, convw_ref, convb_ref, dtb_ref, alog_ref,
                dskip_ref, gnorm_ref, selb_ref, sele_ref, wout_ref, o_ref,
                tail_ref, act_ref, state_ref, cb_ref, yoff_ref, colb_ref, expe_ref):
    c = pl.program_id(1)
    L = CHUNK
    gw = SSM_HPG * SSM_HEAD_DIM

    @pl.when(c == 0)
    def _():
        state_ref[...] = jnp.zeros_like(state_ref)
        tail_ref[...] = jnp.zeros_like(tail_ref)

    cw = 512
    first_row = lax.broadcasted_iota(jnp.int32, (8, cw), 0) == 0
    for cc in range(SSM_XBC // cw):
        cols = slice(cc * cw, (cc + 1) * cw)
        u = xs_ref[:, cols] if cc < D_INNER // cw else bc_ref[:, cc * cw - D_INNER:(cc + 1) * cw - D_INNER]
        a = convw_ref[0:1, cols] * u
        for k in range(1, SSM_CONV):
            shifted = pltpu.roll(a, 1, 0)
            shifted = jnp.concatenate([jnp.where(first_row, tail_ref[k - 1:k, cols], shifted[0:8]), shifted[8:]], axis=0)
            tail_ref[k - 1:k, cols] = a[L - 1:L]
            a = convw_ref[k:k + 1, cols] * u + shifted
        act_ref[:, cols] = _silu(a + convb_ref[:, cols])

    lane = lax.broadcasted_iota(jnp.int32, (L, LANES), 1)
    row = lax.broadcasted_iota(jnp.int32, (L, LANES), 0)
    head_lane = lane < SSM_HEADS
    raw = misc_ref[...] + dtb_ref[...]
    dt = jnp.where(head_lane, jnp.maximum(raw, 0.0) + jnp.log1p(jnp.exp(-jnp.abs(raw))), 0.0)
    a_row = jnp.where(head_lane[0:1], -jnp.exp(alog_ref[...]), 0.0)
    cs = dt * a_row
    sh = 1
    while sh < L:
        cs = cs + jnp.where(row >= sh, pltpu.roll(cs, sh, 0), 0.0)
        sh *= 2
    tot = cs[L - 1:L, :]
    w_state = dt * jnp.exp(tot - cs)
    cs_t, dt_t, w_t = cs.T, dt.T, w_state.T

    hi, mid, lo = _split3(cs)
    packed = (hi.astype(F32) + pltpu.roll(mid.astype(F32), SSM_HEADS, 1)
              + pltpu.roll(lo.astype(F32), 2 * SSM_HEADS, 1)).astype(BF16)

    tri = lax.broadcasted_iota(jnp.int32, (L, L), 0) >= lax.broadcasted_iota(jnp.int32, (L, L), 1)
    brow = lax.broadcasted_iota(jnp.int32, (SSM_HPG * L, gw), 0) // L
    bcol = lax.broadcasted_iota(jnp.int32, (SSM_HPG * L, gw), 1) // SSM_HEAD_DIM
    blockmask = brow == bcol

    def group_bc(g):
        b_g = act_ref[:, D_INNER + g * SSM_STATE: D_INNER + (g + 1) * SSM_STATE]
        c_g = act_ref[:, D_INNER + SSM_GROUPS * SSM_STATE + g * SSM_STATE:
                      D_INNER + SSM_GROUPS * SSM_STATE + (g + 1) * SSM_STATE]
        return b_g, c_g.astype(BF16)

    for g in range(SSM_GROUPS):
        b_g, c_bf = group_bc(g)
        cb_ref[g] = lax.dot_general(c_bf, b_g.astype(BF16), (((1,), (1,)), ((), ())),
                                    preferred_element_type=F32)
        yoff_ref[g] = jnp.dot(c_bf, state_ref[g].astype(BF16), preferred_element_type=F32)
        colb_ref[g] = jnp.dot(packed, selb_ref[:, g * SSM_HPG * L:(g + 1) * SSM_HPG * L], preferred_element_type=F32)
        expe_ref[g] = jnp.exp(jnp.dot(packed, sele_ref[:, g * gw:(g + 1) * gw], preferred_element_type=F32))

    for g in range(SSM_GROUPS):
        col_b = colb_ref[g]
        exp_e = expe_ref[g]
        etot_e = exp_e[L - 1:L, :]
        xs_g = act_ref[:, g * gw:(g + 1) * gw]
        b_g, _ = group_bc(g)
        cb = cb_ref[g]
        b_t = b_g.T
        tops, bots = [], []
        for j in range(SSM_HPG):
            h = g * SSM_HPG + j
            decay = jnp.exp(jnp.where(tri, col_b[:, j * L:(j + 1) * L] - cs_t[h:h + 1, :], NEG))
            tops.append((cb * decay * dt_t[h:h + 1, :]).astype(BF16))
            bots.append((b_t * w_t[h:h + 1, :]).astype(BF16))
        lhs = jnp.concatenate([jnp.concatenate(tops, axis=1), jnp.concatenate(bots, axis=1)], axis=0)
        xs_bf = xs_g.astype(BF16)
        xs_bd = jnp.where(blockmask, jnp.concatenate([xs_bf] * SSM_HPG, axis=0), jnp.zeros((), BF16))
        res = jnp.dot(lhs, xs_bd, preferred_element_type=F32)
        y_off = yoff_ref[g] * exp_e
        state_ref[g] = state_ref[g] * etot_e + res[L:2 * L]
        y = res[0:L] + y_off + dskip_ref[:, g * gw:(g + 1) * gw] * xs_g
        zg = z_ref[:, g * gw:(g + 1) * gw]
        y = y * _silu(zg)
        ms = jnp.mean(y * y, axis=-1, keepdims=True)
        act_ref[:, g * gw:(g + 1) * gw] = y * lax.rsqrt(ms + NORM_EPS) * gnorm_ref[:, g * gw:(g + 1) * gw]

    o_ref[...] = jnp.dot(act_ref[:, 0:D_INNER].astype(BF16), wout_ref[...], preferred_element_type=F32)


def _ssd(raw, raw_gate, conv_w, conv_b, dt_bias, a_log, d_skip, g_ssm_norm, w_ssm, bsz, s):
    t = raw.shape[0]
    nc = s // CHUNK
    blk = lambda cb: pl.BlockSpec((CHUNK, 2048), lambda b, c, cb=cb: (b * nc + c, cb))
    const = lambda shape: pl.BlockSpec(shape, lambda b, c: (0,) * len(shape))
    k = jnp.arange(LANES)
    selb = ((k[:, None] < 3 * SSM_HEADS) & ((k[:, None] % SSM_HEADS) == (jnp.arange(SSM_HEADS * LANES)[None, :] // LANES)))
    sele = ((k[:, None] < 3 * SSM_HEADS) & ((k[:, None] % SSM_HEADS) == (jnp.arange(D_INNER)[None, :] // SSM_HEAD_DIM)))
    pad = lambda v: jnp.pad(v.astype(F32), (0, LANES - SSM_HEADS)).reshape(1, LANES)
    return pl.pallas_call(
        _ssd_kernel,
        grid=(bsz, nc),
        in_specs=[blk(RAW_Z // 2048), blk(RAW_XS // 2048), blk(RAW_BC // 2048),
                  pl.BlockSpec((CHUNK, LANES), lambda b, c: (b * nc + c, GATE_MISC // LANES)),
                  const((SSM_CONV, SSM_XBC)), const((1, SSM_XBC)), const((1, LANES)), const((1, LANES)),
                  const((1, D_INNER)), const((1, D_INNER)),
                  const((LANES, SSM_HEADS * LANES)), const((LANES, D_INNER)),
                  const((D_INNER, D_MODEL))],
        out_specs=pl.BlockSpec((CHUNK, D_MODEL), lambda b, c: (b * nc + c, 0)),
        out_shape=jax.ShapeDtypeStruct((t, D_MODEL), F32),
        scratch_shapes=[pltpu.VMEM((8, SSM_XBC), F32),
                        pltpu.VMEM((CHUNK, SSM_XBC), F32),
                        pltpu.VMEM((SSM_GROUPS, SSM_STATE, SSM_HPG * SSM_HEAD_DIM), F32),
                        pltpu.VMEM((SSM_GROUPS, CHUNK, CHUNK), F32),
                        pltpu.VMEM((SSM_GROUPS, CHUNK, SSM_HPG * SSM_HEAD_DIM), F32),
                        pltpu.VMEM((SSM_GROUPS, CHUNK, SSM_HPG * CHUNK), F32),
                        pltpu.VMEM((SSM_GROUPS, CHUNK, SSM_HPG * SSM_HEAD_DIM), F32)],
        compiler_params=_cparams(("parallel", "arbitrary")),
        name="ssd_mixer",
    )(raw, raw, raw, raw_gate, conv_w.astype(F32), conv_b.reshape(1, SSM_XBC).astype(F32),
      pad(dt_bias), pad(a_log), jnp.repeat(d_skip.astype(F32), SSM_HEAD_DIM).reshape(1, D_INNER),
      g_ssm_norm.reshape(1, D_INNER).astype(F32), selb.astype(BF16), sele.astype(BF16), w_ssm.astype(BF16))


def _compress_kernel(u_ref, pelo_ref, pehi_ref, w1_ref, w2_ref, o_ref):
    u = u_ref[0]
    half = CMP_STRIDE * HEAD_DIM
    a = jnp.dot((u + pelo_ref[0, 0]).astype(BF16), w1_ref[0, 0:half, :], preferred_element_type=F32)
    b = jnp.dot((u + pehi_ref[0, 0]).astype(BF16), w1_ref[0, half:2 * half, :], preferred_element_type=F32)
    pre = a + pltpu.roll(b, u.shape[0] - 1, 0)
    hidden = _silu(pre)
    out = jnp.dot(hidden.astype(BF16), w2_ref[0], preferred_element_type=F32)
    lane = lax.broadcasted_iota(jnp.int32, out.shape, 1)
    o_ref[0] = jnp.where(lane == HEAD_DIM, 1.0, out).astype(o_ref.dtype)


def _compress(u, pe, w1, w2):
    n, rows, width = u.shape
    kv_of = lambda i: (i // KV_GROUPS) % 2
    return pl.pallas_call(
        _compress_kernel,
        grid=(n,),
        in_specs=[pl.BlockSpec((1, rows, width), lambda i: (i, 0, 0)),
                  pl.BlockSpec((1, 1, 1, width), lambda i: (kv_of(i), 0, 0, 0)),
                  pl.BlockSpec((1, 1, 1, width), lambda i: (kv_of(i), 1, 0, 0)),
                  pl.BlockSpec((1, 2 * width, CMP_HIDDEN), lambda i: (kv_of(i), 0, 0)),
                  pl.BlockSpec((1, CMP_HIDDEN, LANES), lambda i: (kv_of(i), 0, 0))],
        out_specs=pl.BlockSpec((1, rows, LANES), lambda i: (i, 0, 0)),
        out_shape=jax.ShapeDtypeStruct((n, rows, LANES), BF16),
        compiler_params=_cparams(("parallel",)),
        name="compress",
    )(u, pe, pe, w1, w2)


def _nsa_kernel(qn_ref, qr_ref, kc_ref, vct_ref, ks_ref, vst_ref, kw_ref, vwt_ref, gate_ref, ovt_ref, oh_ref,
                eye_ref, wlo_ref, whi_ref,
                o_ref, gt_ref, qa_ref, sa_ref, sb_ref, pa_ref, pb_ref, sc_ref, accc_ref, invc_ref, imp_ref, sw_ref, pc_ref,
                *, topk):
    gp = pl.program_id(1)
    qb = pl.program_id(2)
    groups = range(NSA_GROUPS)
    nq = ATT_HPG * Q_BLOCK
    ncmp = kc_ref.shape[1]
    nsel = ovt_ref.shape[0]
    q0 = qb * Q_BLOCK
    tq = q0 + lax.broadcasted_iota(jnp.int32, (1, nq), 1) % Q_BLOCK

    def heads_on_lanes(ref, gi):
        base = gi * ATT_HPG * HEAD_DIM
        return jnp.concatenate([ref[0, base + r * HEAD_DIM:base + (r + 1) * HEAD_DIM, :]
                                for r in range(ATT_HPG)], axis=1)

    qn_t = [heads_on_lanes(qn_ref, gi) for gi in groups]
    qr_t = [heads_on_lanes(qr_ref, gi) for gi in groups]

    def q_operand(q_t, flag, table):
        flag_rows = jnp.concatenate([flag, jnp.zeros((LANES - HEAD_DIM - 1, nq), F32)], axis=0).astype(BF16)
        return jnp.concatenate([q_t, flag_rows] + ([table] if table is not None else []), axis=0)

    no_flag = jnp.zeros((1, nq), F32)

    n_chunks = ncmp // LANES
    chunk_span = LANES * CMP_STRIDE
    c_hi = jnp.minimum((q0 + Q_BLOCK - CMP_LEN) // chunk_span, n_chunks - 1)

    def compressed_branch(n_live):
        live = n_live * LANES
        ov_live = jnp.concatenate([ovt_ref[:, piece * ncmp:piece * ncmp + live] for piece in range(3)], axis=1)
        for c in range(n_live):
            for gi in groups:
                s = jnp.dot(kc_ref[gi, c * LANES:(c + 1) * LANES, :], q_operand(qn_t[gi], no_flag, None),
                            preferred_element_type=F32)
                if c >= n_live - 2:
                    ends = (c * chunk_span + CMP_LEN - 1
                            + lax.broadcasted_iota(jnp.int32, (LANES, nq), 0) * CMP_STRIDE)
                    s = jnp.where(ends <= tq, s, NEG)
                sc_ref[gi, c * LANES:(c + 1) * LANES, :] = s
        m_cs, invs = [], []
        for gi in groups:
            s_c = sc_ref[gi, 0:live, :]
            m_cs.append(jnp.max(s_c, axis=0, keepdims=True))
            p_c = jnp.exp2(s_c - m_cs[gi])
            sc_ref[gi, 0:live, :] = p_c
            pc_ref[gi, 0:live, :] = p_c.astype(BF16)
        for gi in groups:
            acc = jnp.dot(vct_ref[gi, :, 0:live], pc_ref[gi, 0:live, :], preferred_element_type=F32)
            invs.append(jnp.where(m_cs[gi] > 0.5 * NEG, 1.0 / acc[HEAD_DIM:HEAD_DIM + 1], 0.0))
            accc_ref[gi] = acc
            invc_ref[gi] = jnp.broadcast_to(invs[gi], (8, nq))
        for gi in groups:
            inv = invs[gi]
            p_sum = sc_ref[gi, 0:live, 0:Q_BLOCK] * inv[:, 0:Q_BLOCK]
            for r in range(1, ATT_HPG):
                p_sum = p_sum + sc_ref[gi, 0:live, r * Q_BLOCK:(r + 1) * Q_BLOCK] * inv[:, r * Q_BLOCK:(r + 1) * Q_BLOCK]
            imp_ref[gi] = jnp.dot(ov_live, jnp.concatenate(_split3(p_sum), axis=0), preferred_element_type=F32)

    for k in range(n_chunks):
        pl.when(c_hi == k)(functools.partial(compressed_branch, k + 1))
    acc_c = [accc_ref[gi] for gi in groups]
    inv_c = [invc_ref[gi, 0:1, :] for gi in groups]
    imp = [imp_ref[gi] for gi in groups]

    n_wt = WINDOW // Q_BLOCK + 1
    eye = eye_ref[...]
    for gi in groups:
        for i in range(n_wt):
            kt = qb - (n_wt - 1) + i
            k0 = pl.multiple_of(jnp.maximum(kt, 0) * Q_BLOCK, Q_BLOCK)
            keys = kw_ref[gi, pl.ds(k0, Q_BLOCK), :]
            flag = jnp.where(kt >= 0, no_flag, NEG)
            if i == 0:
                s = jnp.dot(jnp.concatenate([keys, eye], axis=1), q_operand(qr_t[gi], flag, wlo_ref[...]),
                            preferred_element_type=F32)
            elif i == n_wt - 1:
                s = jnp.dot(jnp.concatenate([keys, eye], axis=1), q_operand(qr_t[gi], flag, whi_ref[...]),
                            preferred_element_type=F32)
            else:
                s = jnp.dot(keys, q_operand(qr_t[gi], flag, None), preferred_element_type=F32)
            sw_ref[gi, i * Q_BLOCK:(i + 1) * Q_BLOCK, :] = s
    acc_w = []
    for gi in groups:
        v_w = [vwt_ref[gi, jnp.maximum(qb - (n_wt - 1) + i, 0)] for i in range(n_wt)]
        s_w = sw_ref[gi]
        p_w = jnp.exp2(s_w - jnp.max(s_w, axis=0, keepdims=True))
        acc_w.append(jnp.dot(jnp.concatenate(v_w, axis=1), p_w.astype(BF16), preferred_element_type=F32))

    blk = lax.broadcasted_iota(jnp.int32, (nsel, Q_BLOCK), 0)
    cur = (q0 + lax.broadcasted_iota(jnp.int32, (nsel, Q_BLOCK), 1)) // SEL_BLOCK
    valid = blk <= cur
    forced = valid & ((blk == 0) | (blk == cur) | (blk == cur - 1))
    val = [jnp.where(forced, REMOVED, jnp.where(valid, imp[gi], NEG)) for gi in groups]
    blk_f = blk.astype(F32)
    for _ in range(topk - 3):
        for gi in groups:
            mx = jnp.max(val[gi], axis=0, keepdims=True)
            first = jnp.min(jnp.where(val[gi] == mx, blk_f, float(nsel)), axis=0, keepdims=True)
            val[gi] = jnp.where(blk_f == first, REMOVED, val[gi])

    pad_rows = [jnp.zeros((LANES - nsel, nq), F32)] if nsel < LANES else []
    for gi in groups:
        bias = jnp.concatenate([jnp.where(val[gi] == REMOVED, 0.0, NEG)] * ATT_HPG, axis=1)
        qa_ref[gi] = q_operand(qr_t[gi], no_flag, jnp.concatenate([bias] + pad_rows, axis=0).astype(BF16))

    def sel_scores(gi, i):
        k0 = pl.multiple_of(i * KEY_STEP, KEY_STEP)
        lhs = jnp.concatenate([ks_ref[gi, pl.ds(k0, KEY_STEP), :], oh_ref[pl.ds(k0, KEY_STEP), :]], axis=1)
        return jnp.dot(lhs, qa_ref[gi], preferred_element_type=F32)

    row_k = lax.broadcasted_iota(jnp.int32, (KEY_STEP, nq), 0)

    def softmax_step(s_ref, p_ref, gi, m, first_key):
        s = s_ref[gi]
        if first_key is not None:
            s = jnp.where(row_k <= tq - first_key, s, NEG)
        m_new = jnp.maximum(m, jnp.max(s, axis=0, keepdims=True))
        p_ref[gi] = jnp.exp2(s - m_new).astype(BF16)
        return m_new, jnp.exp2(m - m_new)

    def pv(gi, acc, alpha, p_ref, i):
        return acc * alpha + jnp.dot(vst_ref[gi, i], p_ref[gi], preferred_element_type=F32)

    def sel_pair(j, carries, final):
        a = 2 * j
        m, acc, alpha_b = ([c[k] for c in carries] for k in range(3))
        alpha_a = [None] * NSA_GROUPS
        for gi in groups:
            sb_ref[gi] = sel_scores(gi, a + 1)
            acc[gi] = pv(gi, acc[gi], alpha_b[gi], pb_ref, jnp.maximum(a - 1, 0))
            m[gi], alpha_a[gi] = softmax_step(sa_ref, pa_ref, gi, m[gi], a * KEY_STEP if final else None)
            if not final:
                sa_ref[gi] = sel_scores(gi, a + 2)
            acc[gi] = pv(gi, acc[gi], alpha_a[gi], pa_ref, a)
            m[gi], alpha_b[gi] = softmax_step(sb_ref, pb_ref, gi, m[gi], (a + 1) * KEY_STEP if final else None)
        return tuple((m[gi], acc[gi], alpha_b[gi]) for gi in groups)

    n_pairs = (q0 + Q_BLOCK - 1) // (2 * KEY_STEP) + 1
    for gi in groups:
        sa_ref[gi] = sel_scores(gi, 0)
    pb_ref[...] = jnp.zeros_like(pb_ref)
    init = tuple((jnp.full((1, nq), NEG, F32), jnp.zeros((V_ROWS, nq), F32), jnp.ones((1, nq), F32))
                 for gi in groups)
    carries = lax.fori_loop(0, n_pairs - 1, lambda j, c: sel_pair(j, c, False), init)
    carries = sel_pair(n_pairs - 1, carries, True)
    acc_s = [pv(gi, carries[gi][1], carries[gi][2], pb_ref, 2 * n_pairs - 1) for gi in groups]

    gt_ref[...] = _sigmoid(gate_ref[...]).T
    hd = HEAD_DIM
    for gi in groups:
        def gate_row(branch, gi=gi):
            base = MISC_GATE0 + branch * ATT_HEADS + (gp * NSA_GROUPS + gi) * ATT_HPG
            return jnp.concatenate([gt_ref[pl.ds(base + r, 1), :] for r in range(ATT_HPG)], axis=1)

        o = ((gate_row(0) * inv_c[gi]) * acc_c[gi][0:hd] + (gate_row(1) / acc_s[gi][hd:hd + 1]) * acc_s[gi][0:hd]
             + (gate_row(2) / acc_w[gi][hd:hd + 1]) * acc_w[gi][0:hd])
        for r in range(ATT_HPG):
            row0 = (gi * ATT_HPG + r) * HEAD_DIM
            o_ref[0, row0:row0 + HEAD_DIM, :] = o[:, r * Q_BLOCK:(r + 1) * Q_BLOCK].astype(o_ref.dtype)


def _nsa(qn_t, qr_t, kc, vc_t, ks, vs_t, kw, vw_t, raw_gate, bsz, s):
    nqb = s // Q_BLOCK
    ncmp = s // CMP_STRIDE
    nsel = s // SEL_BLOCK
    topk = min(SEL_TOPK, nsel)
    gw = ATT_HPG * HEAD_DIM
    ci = jnp.arange(ncmp)[None, :]
    sj = jnp.arange(nsel)[:, None]
    ov_t = (ci * CMP_STRIDE < (sj + 1) * SEL_BLOCK) & (ci * CMP_STRIDE + CMP_LEN > sj * SEL_BLOCK) \
        & (ci < ncmp - 1)
    ov_t3 = jnp.concatenate([ov_t.astype(BF16)] * 3, axis=1)
    assert nsel <= LANES
    onehot = (jnp.arange(s)[:, None] // SEL_BLOCK == jnp.arange(LANES)[None, :]).astype(BF16)
    eye = jnp.eye(Q_BLOCK, dtype=BF16)
    key_row = jnp.arange(Q_BLOCK)[:, None]
    q_lane = jnp.arange(ATT_HPG * Q_BLOCK)[None, :] % Q_BLOCK
    win_lo = jnp.where(key_row > q_lane, 0.0, NEG).astype(BF16)
    win_hi = jnp.where(key_row <= q_lane, 0.0, NEG).astype(BF16)
    once = pl.Buffered(1)
    const2 = lambda shape: pl.BlockSpec(shape, lambda b, g, i: (0, 0), pipeline_mode=once)
    ng = NSA_GROUPS
    gpb = KV_GROUPS // ng
    q_spec = pl.BlockSpec((1, ng * gw, Q_BLOCK), lambda b, g, i: (b, g, i))
    per_bg = lambda shape: pl.BlockSpec((ng,) + shape, lambda b, g, i: (b * gpb + g,) + (0,) * len(shape),
                                        pipeline_mode=once)
    nql = ATT_HPG * Q_BLOCK
    return pl.pallas_call(
        functools.partial(_nsa_kernel, topk=topk),
        grid=(bsz, gpb, nqb),
        in_specs=[q_spec, q_spec,
                  per_bg((ncmp, LANES)), per_bg((V_ROWS, ncmp)),
                  per_bg((s, LANES)), per_bg((s // KEY_STEP, V_ROWS, KEY_STEP)),
                  per_bg((s, LANES)), per_bg((s // Q_BLOCK, V_ROWS, Q_BLOCK)),
                  pl.BlockSpec((Q_BLOCK, LANES), lambda b, g, i: (b * nqb + i, GATE_MISC // LANES)),
                  const2((nsel, 3 * ncmp)), const2((s, LANES)),
                  const2((Q_BLOCK, Q_BLOCK)), const2((Q_BLOCK, ATT_HPG * Q_BLOCK)),
                  const2((Q_BLOCK, ATT_HPG * Q_BLOCK))],
        out_specs=pl.BlockSpec((1, ng * gw, Q_BLOCK), lambda b, g, i: (b, g, i)),
        out_shape=jax.ShapeDtypeStruct((bsz, ATT_HEADS * HEAD_DIM, s), BF16),
        scratch_shapes=[pltpu.VMEM((LANES, Q_BLOCK), F32),
                        pltpu.VMEM((ng, 2 * LANES, nql), BF16),
                        pltpu.VMEM((ng, KEY_STEP, nql), F32),
                        pltpu.VMEM((ng, KEY_STEP, nql), F32),
                        pltpu.VMEM((ng, KEY_STEP, nql), BF16),
                        pltpu.VMEM((ng, KEY_STEP, nql), BF16),
                        pltpu.VMEM((ng, ncmp, nql), F32),
                        pltpu.VMEM((ng, V_ROWS, nql), F32),
                        pltpu.VMEM((ng, 8, nql), F32),
                        pltpu.VMEM((ng, nsel, Q_BLOCK), F32),
                        pltpu.VMEM((ng, (WINDOW // Q_BLOCK + 1) * Q_BLOCK, nql), F32),
                        pltpu.VMEM((ng, ncmp, nql), BF16)],
        compiler_params=_cparams(("parallel", "parallel", "arbitrary")),
        name="nsa_sweep",
    )(qn_t, qr_t, kc, vc_t, ks, vs_t, kw, vw_t, raw_gate, ov_t3, onehot, eye, win_lo, win_hi)


def _merge_mlp_kernel(ot_ref, yssm_ref, mix_ref, x_ref, wab_ref, wo_ref, g1_ref, wup_ref, wdn_ref, g2_ref, o_ref):
    o = ot_ref[0].astype(F32).T.astype(BF16)
    y_att = jnp.dot(o, wab_ref[...], preferred_element_type=F32)
    gates = _sigmoid(mix_ref[...])
    mixed = gates[:, 0:D_MODEL] * yssm_ref[...] + gates[:, D_MODEL:2 * D_MODEL] * y_att
    x = x_ref[...] + jnp.dot(mixed.astype(BF16), wo_ref[...], preferred_element_type=F32)
    ms = jnp.mean(x * x, axis=-1, keepdims=True)
    h = (x * lax.rsqrt(ms + NORM_EPS) * g1_ref[...]).astype(BF16)
    up = jnp.maximum(jnp.dot(h, wup_ref[...], preferred_element_type=F32), 0.0)
    y = x + jnp.dot((up * up).astype(BF16), wdn_ref[...], preferred_element_type=F32)
    ms2 = jnp.mean(y * y, axis=-1, keepdims=True)
    o_ref[...] = y * lax.rsqrt(ms2 + NORM_EPS) * g2_ref[...]


def _merge_mlp(o_t, y_ssm, raw_gate, x2d, w_ab, w_o, g1, w_up, w_down, g2, s):
    t = x2d.shape[0]
    tm = ROW_TILE
    spb = s // tm
    row_spec = pl.BlockSpec((tm, D_MODEL), lambda i: (i, 0))
    const = lambda shape: pl.BlockSpec(shape, lambda i: (0, 0), pipeline_mode=pl.Buffered(1))
    return pl.pallas_call(
        _merge_mlp_kernel,
        grid=(t // tm,),
        in_specs=[pl.BlockSpec((1, D_MODEL, tm), lambda i: (i // spb, 0, i % spb)),
                  row_spec,
                  pl.BlockSpec((tm, 2 * D_MODEL), lambda i: (i, GATE_MIX // (2 * D_MODEL))),
                  row_spec, const((D_MODEL, D_MODEL)), const((D_MODEL, D_MODEL)),
                  const((1, D_MODEL)), const((D_MODEL, MLP_HIDDEN)), const((MLP_HIDDEN, D_MODEL)),
                  const((1, D_MODEL))],
        out_specs=row_spec,
        out_shape=jax.ShapeDtypeStruct((t, D_MODEL), F32),
        compiler_params=_cparams(("parallel",)),
        name="merge_mlp",
    )(o_t, y_ssm, raw_gate, x2d, w_ab, w_o, g1.reshape(1, D_MODEL), w_up, w_down, g2.reshape(1, D_MODEL))


def _rope_tables(positions):
    inv_freq = ROPE_THETA ** (-jnp.arange(0, ROPE_DIM, 2, dtype=F32) / ROPE_DIM)
    ang = positions.astype(F32).reshape(1, -1) * inv_freq[:, None]
    return jnp.cos(ang), jnp.sin(ang)


def kernel(x, positions, g_norm_mix, w_in, conv_w, conv_b, dt_bias, a_log, d_skip, g_ssm_norm, w_ssm_branch, cmp_pe_k, cmp_pe_v, w_cmp_k1, w_cmp_k2, w_cmp_v1, w_cmp_v2, w_attn_branch, w_o, g_norm_mlp, w_up, w_down, g_norm_final):
    bsz, s, d = x.shape
    assert d == D_MODEL and s % (2 * ROW_TILE) == 0 and w_in.shape[0] == 1
    t = bsz * s
    x2d = x.reshape(t, d)
    tabs = _rope_tables(positions)

    w = w_in[0].astype(BF16)
    o_dt, o_q, o_kv, o_ag, o_mg = 6144, 6176, 7200, 8736, 8784
    misc_pad = jnp.zeros((d, GATE_COLS - GATE_MISC - SSM_HEADS - 3 * ATT_HEADS), BF16)
    w_gate = jnp.concatenate([w[:, o_mg:], w[:, o_dt:o_q], w[:, o_ag:o_mg], misc_pad], axis=1)
    w_q = w[:, o_q:o_kv]
    w_kv = w[:, o_kv:o_ag]

    h = _rmsnorm(x2d, g_norm_mix[0])
    raw = _proj_raw(h, w, SSM_COLS, SSM_TN, 2 * ROW_TILE, "proj_ssm")
    raw_gate = _proj_raw(h, w_gate, GATE_COLS, GATE_COLS, 2 * ROW_TILE, "proj_gate")
    qn_t, qr_t = _proj_q(h, w_q, tabs, bsz, s)
    kvc, ks, kw, vs_t, vw_t = _proj_kv(h, w_kv, tabs, bsz, s)

    y_ssm = _ssd(raw, raw_gate, conv_w[0], conv_b[0], dt_bias[0], a_log[0], d_skip[0], g_ssm_norm[0],
                 w_ssm_branch[0], bsz, s)

    rows = s // CMP_STRIDE
    u = kvc.reshape(bsz * 2 * KV_GROUPS, rows, CMP_STRIDE * HEAD_DIM)
    pe = jnp.stack([cmp_pe_k[0], cmp_pe_v[0]]).astype(F32).reshape(2, 2, 1, CMP_STRIDE * HEAD_DIM)
    w1 = jnp.stack([w_cmp_k1[0], w_cmp_v1[0]]).astype(BF16)
    w2 = jnp.pad(jnp.stack([w_cmp_k2[0], w_cmp_v2[0]]).astype(BF16), ((0, 0), (0, 0), (0, LANES - HEAD_DIM)))
    cmp = _compress(u, pe, w1, w2).reshape(bsz, 2, KV_GROUPS, rows, LANES)
    kc = cmp[:, 0].reshape(bsz * KV_GROUPS, rows, LANES)
    vc_t = jnp.swapaxes(cmp[:, 1], -1, -2)[:, :, :V_ROWS].reshape(bsz * KV_GROUPS, V_ROWS, rows)

    flat = lambda a: a.reshape((bsz * KV_GROUPS,) + a.shape[2:])
    o_t = _nsa(qn_t, qr_t, kc, vc_t, flat(ks), flat(vs_t), flat(kw), flat(vw_t), raw_gate, bsz, s)

    out = _merge_mlp(o_t, y_ssm, raw_gate, x2d, w_attn_branch[0].astype(BF16), w_o[0].astype(BF16),
                     g_norm_mlp[0], w_up[0].astype(BF16), w_down[0].astype(BF16), g_norm_final, s)
    return out.reshape(bsz, s, d)
```
